```python
import jax, jax.numpy as jnp
from jax import lax
import numpy as np

D_MODEL = 1024
BATCH = 2
SEQ = 8192
DEPTH = 1

GRID_W = 64
CTX_LEN = 256
N_HEADS = 8
N_KV_HEADS = 2
HEAD_DIM = 64
GQA_GROUP = N_HEADS // N_KV_HEADS
ATT_WIDTH = N_HEADS * HEAD_DIM
KV_WIDTH = N_KV_HEADS * HEAD_DIM
CONV_WIDTH = D_MODEL - ATT_WIDTH
CONV_GROUPS = 8
CONV_K = 3
WINDOW = 128
BLOCK = 128
ROPE_THETA = 10000.0
N_EXPERTS = 16
EC_CAPACITY_FACTOR = 2
D_EXPERT = 1024
EPS = 1e-6
IN_WIDTH = ATT_WIDTH + 2 * KV_WIDTH + 3 * CONV_WIDTH
SPLIT_POINTS = [ATT_WIDTH, ATT_WIDTH + KV_WIDTH, ATT_WIDTH + 2 * KV_WIDTH,
                ATT_WIDTH + 2 * KV_WIDTH + CONV_WIDTH, ATT_WIDTH + 2 * KV_WIDTH + 2 * CONV_WIDTH]

kernel_name = "hybrid_dit_swa_shortconv_ecmoe"


def rmsnorm(x, g):
    xf = x.astype(jnp.float32)
    r = xf * lax.rsqrt(jnp.mean(xf * xf, axis=-1, keepdims=True) + EPS)
    return (r * g.astype(jnp.float32)).astype(x.dtype)


def modulate(h, shift, scale):
    return h * (1 + scale) + shift


def axial_rope_tables(seq_len):
    rows = seq_len // GRID_W
    row = jnp.repeat(jnp.arange(rows), GRID_W, total_repeat_length=rows * GRID_W)
    col = jnp.tile(jnp.arange(GRID_W), rows)
    n_freq = HEAD_DIM // 4
    inv = ROPE_THETA ** (-jnp.arange(n_freq, dtype=jnp.float32) / n_freq)
    ang = jnp.concatenate([row.astype(jnp.float32)[:, None] * inv,
                           col.astype(jnp.float32)[:, None] * inv], axis=-1)
    return jnp.cos(ang), jnp.sin(ang)


def apply_rope(x, cos, sin):
    xf = x.astype(jnp.float32)
    half = HEAD_DIM // 2
    x1, x2 = xf[..., :half], xf[..., half:]
    cs, sn = cos[None, :, None, :], sin[None, :, None, :]
    return jnp.concatenate([x1 * cs - x2 * sn, x1 * sn + x2 * cs], axis=-1).astype(x.dtype)


def windowed_attention(q, k, v, k_ctx, v_ctx, sink):
    bsz, seq, _, hd = q.shape
    nb = seq // BLOCK
    scale = hd ** -0.5
    qb = q.reshape(bsz, nb, BLOCK, N_KV_HEADS, GQA_GROUP, hd)
    pad = ((0, 0), (BLOCK, BLOCK), (0, 0), (0, 0))
    kp = jnp.pad(k, pad).reshape(bsz, nb + 2, BLOCK, N_KV_HEADS, hd)
    vp = jnp.pad(v, pad).reshape(bsz, nb + 2, BLOCK, N_KV_HEADS, hd)
    kb = jnp.concatenate([kp[:, :-2], kp[:, 1:-1], kp[:, 2:]], axis=2)
    vb = jnp.concatenate([vp[:, :-2], vp[:, 1:-1], vp[:, 2:]], axis=2)
    s_loc = jnp.einsum('bnqkgd,bnjkd->bnkgqj', qb, kb).astype(jnp.float32) * scale
    s_ctx = jnp.einsum('bnqkgd,bjkd->bnkgqj', qb, k_ctx).astype(jnp.float32) * scale
    i = jnp.arange(BLOCK)[None, :, None]
    j = jnp.arange(3 * BLOCK)[None, None, :]
    kpos = jnp.arange(nb)[:, None, None] * BLOCK + j - BLOCK
    valid = (jnp.abs(j - BLOCK - i) <= WINDOW) & (kpos >= 0) & (kpos < seq)
    s_loc = jnp.where(valid[None, :, None, None], s_loc, -jnp.inf)
    sk = jnp.broadcast_to(sink.astype(jnp.float32).reshape(N_KV_HEADS, GQA_GROUP)[None, None, :, :, None, None],
                          s_loc.shape[:-1] + (1,))
    p = jax.nn.softmax(jnp.concatenate([sk, s_ctx, s_loc], axis=-1), axis=-1)
    n_ctx = k_ctx.shape[1]
    p_ctx = p[..., 1:1 + n_ctx].astype(v.dtype)
    p_loc = p[..., 1 + n_ctx:].astype(v.dtype)
    out = (jnp.einsum('bnkgqj,bjkd->bnqkgd', p_ctx, v_ctx)
           + jnp.einsum('bnkgqj,bnjkd->bnqkgd', p_loc, vb))
    return out.reshape(bsz, seq, N_HEADS * hd)


def context_attention(q, k, v, sink):
    bsz, n, _, hd = q.shape
    qg = q.reshape(bsz, n, N_KV_HEADS, GQA_GROUP, hd)
    s = jnp.einsum('bqkgd,bjkd->bkgqj', qg, k).astype(jnp.float32) * hd ** -0.5
    sk = jnp.broadcast_to(sink.astype(jnp.float32).reshape(N_KV_HEADS, GQA_GROUP)[None, :, :, None, None],
                          s.shape[:-1] + (1,))
    p = jax.nn.softmax(jnp.concatenate([sk, s], axis=-1), axis=-1)[..., 1:].astype(v.dtype)
    return jnp.einsum('bkgqj,bjkd->bqkgd', p, v).reshape(bsz, n, N_HEADS * hd)


def short_conv_mixer(bg, cg, hv, conv_w):
    u = cg * hv
    up = jnp.pad(u, ((0, 0), (1, 1), (0, 0)))
    y = conv_w[0] * up[:, :-2] + conv_w[1] * up[:, 1:-1] + conv_w[2] * up[:, 2:]
    return bg * y


def expert_choice_ffn(h, w_router, w_gate, w_up, w_down):
    bsz, n, d = h.shape
    cap = EC_CAPACITY_FACTOR * n // N_EXPERTS
    aff = jax.nn.softmax((h @ w_router).astype(jnp.float32), axis=-1)
    g, idx = lax.top_k(jnp.swapaxes(aff, 1, 2), cap)
    xs = jax.vmap(lambda hb, ib: hb[ib])(h, idx)
    a = jnp.einsum('becd,edf->becf', xs, w_gate)
    u = jnp.einsum('becd,edf->becf', xs, w_up)
    y = jnp.einsum('becf,efd->becd', jax.nn.silu(a) * u, w_down) * g[..., None].astype(h.dtype)
    flat = (idx + (jnp.arange(bsz) * n)[:, None, None]).reshape(-1)
    out = jax.ops.segment_sum(y.reshape(-1, d), flat, num_segments=bsz * n)
    return out.reshape(bsz, n, d)


def setup_inputs(seed: int = 0) -> dict:
    key = jax.random.key(seed)
    ks = jax.random.split(key, 17)
    D = D_MODEL
    nrm = jax.random.normal
    f32 = jnp.float32
    return {
        "x": nrm(ks[0], (BATCH, SEQ, D), f32),
        "c": nrm(ks[1], (BATCH, D), f32),
        "ctx": nrm(ks[2], (BATCH, CTX_LEN, D), f32),
        "c_ctx": nrm(ks[3], (D,), f32),
        "w_ada": nrm(ks[4], (DEPTH, D, 6 * D), f32) * (0.5 * D ** -0.5),
        "b_ada": nrm(ks[5], (DEPTH, 6 * D), f32) * 0.01,
        "g_mix": 1.0 + 0.1 * nrm(ks[6], (DEPTH, D), f32),
        "w_in": nrm(ks[7], (DEPTH, D, IN_WIDTH), f32) * D ** -0.5,
        "conv_w": nrm(ks[8], (DEPTH, CONV_K, CONV_WIDTH), f32) * CONV_K ** -0.5,
        "sink": nrm(ks[9], (DEPTH, N_HEADS), f32),
        "w_out": nrm(ks[10], (DEPTH, D, D), f32) * D ** -0.5,
        "g_ffn": 1.0 + 0.1 * nrm(ks[11], (DEPTH, D), f32),
        "w_router": nrm(ks[12], (DEPTH, D, N_EXPERTS), f32) * D ** -0.5,
        "w_gate": nrm(ks[13], (DEPTH, N_EXPERTS, D, D_EXPERT), f32) * D ** -0.5,
        "w_up": nrm(ks[14], (DEPTH, N_EXPERTS, D, D_EXPERT), f32) * D ** -0.5,
        "w_down": nrm(ks[15], (DEPTH, N_EXPERTS, D_EXPERT, D), f32) * D_EXPERT ** -0.5,
        "g_final": 1.0 + 0.1 * nrm(ks[16], (D,), f32),
    }


def reference(x, c, ctx, c_ctx, w_ada, b_ada, g_mix, w_in, conv_w, sink, w_out, g_ffn,
              w_router, w_gate, w_up, w_down, g_final):
    bsz, seq, _ = x.shape
    cos, sin = axial_rope_tables(seq)
    for l in range(DEPTH):
        last = l == DEPTH - 1
        mod = jax.nn.silu(c) @ w_ada[l] + b_ada[l]
        sh1, sc1, gt1, sh2, sc2, gt2 = [m[:, None, :] for m in jnp.split(mod, 6, axis=-1)]
        mod_c = jax.nn.silu(c_ctx) @ w_ada[l] + b_ada[l]
        csh1, csc1, cgt1, csh2, csc2, cgt2 = jnp.split(mod_c, 6, axis=-1)

        hc = modulate(rmsnorm(ctx, g_mix[l]), csh1, csc1)
        n_ctx = ctx.shape[1]
        if last:
            kv_c = hc @ w_in[l][:, ATT_WIDTH:ATT_WIDTH + 2 * KV_WIDTH]
            k_c, v_c = jnp.split(kv_c, 2, axis=-1)
            k_c = k_c.reshape(bsz, n_ctx, N_KV_HEADS, HEAD_DIM)
            v_c = v_c.reshape(bsz, n_ctx, N_KV_HEADS, HEAD_DIM)
        else:
            q_c, k_c, v_c, bg_c, cg_c, hv_c = jnp.split(hc @ w_in[l], SPLIT_POINTS, axis=-1)
            q_c = q_c.reshape(bsz, n_ctx, N_HEADS, HEAD_DIM)
            k_c = k_c.reshape(bsz, n_ctx, N_KV_HEADS, HEAD_DIM)
            v_c = v_c.reshape(bsz, n_ctx, N_KV_HEADS, HEAD_DIM)
            att_c = context_attention(q_c, k_c, v_c, sink[l])
            conv_c = short_conv_mixer(bg_c, cg_c, hv_c, conv_w[l])
            ctx_new = ctx + cgt1 * (jnp.concatenate([att_c, conv_c], axis=-1) @ w_out[l])
            hc2 = modulate(rmsnorm(ctx_new, g_ffn[l]), csh2, csc2)
            ctx_new = ctx_new + cgt2 * expert_choice_ffn(hc2, w_router[l], w_gate[l], w_up[l], w_down[l])

        h = modulate(rmsnorm(x, g_mix[l]), sh1, sc1)
        q, k, v, bg, cg, hv = jnp.split(h @ w_in[l], SPLIT_POINTS, axis=-1)
        q = apply_rope(q.reshape(bsz, seq, N_HEADS, HEAD_DIM), cos, sin)
        k = apply_rope(k.reshape(bsz, seq, N_KV_HEADS, HEAD_DIM), cos, sin)
        v = v.reshape(bsz, seq, N_KV_HEADS, HEAD_DIM)
        att = windowed_attention(q, k, v, k_c, v_c, sink[l])
        conv = short_conv_mixer(bg, cg, hv, conv_w[l])
        x = x + gt1 * (jnp.concatenate([att, conv], axis=-1) @ w_out[l])

        h2 = modulate(rmsnorm(x, g_ffn[l]), sh2, sc2)
        x = x + gt2 * expert_choice_ffn(h2, w_router[l], w_gate[l], w_up[l], w_down[l])

        if not last:
            ctx = ctx_new
    return rmsnorm(x, g_final)
```

```python
import functools

import numpy as np
import jax
import jax.numpy as jnp
from jax import lax
from jax.experimental import pallas as pl
from jax.experimental.pallas import tpu as pltpu

GRID_W = 64
N_HEADS = 8
N_KV_HEADS = 2
HEAD_DIM = 64
GQA_GROUP = N_HEADS // N_KV_HEADS
ATT_WIDTH = N_HEADS * HEAD_DIM
KV_WIDTH = N_KV_HEADS * HEAD_DIM
WINDOW = 128
BLOCK = 128
ROPE_THETA = 10000.0
N_EXPERTS = 16
EC_CAPACITY_FACTOR = 2
EPS = 1e-6

LANES = 128
SUBLANES = 8
VMEM_LIMIT = 56 * 1024 * 1024

F32 = jnp.float32
BF16 = jnp.bfloat16
NEG = -1e30


def _cparams(sem):
    return pltpu.CompilerParams(dimension_semantics=sem, vmem_limit_bytes=VMEM_LIMIT)


def _norm_mod(xf, g, shift, scale):
    r = xf * lax.rsqrt(jnp.mean(xf * xf, axis=-1, keepdims=True) + EPS)
    return (r * g) * (1.0 + scale) + shift


def _mod_kernel(c_ref, w_ref, b_ref, o_ref):
    c = c_ref[...]
    s = c * jax.nn.sigmoid(c)
    o_ref[...] = jnp.dot(s, w_ref[...], precision=lax.Precision.HIGHEST,
                         preferred_element_type=F32) + b_ref[...]


def _mod_call(cin, w, bias, tn=1024):
    d, n = w.shape
    return pl.pallas_call(
        _mod_kernel,
        grid=(n // tn,),
        in_specs=[pl.BlockSpec((SUBLANES, d), lambda j: (0, 0)),
                  pl.BlockSpec((d, tn), lambda j: (0, j)),
                  pl.BlockSpec((1, tn), lambda j: (0, j))],
        out_specs=pl.BlockSpec((SUBLANES, tn), lambda j: (0, j)),
        out_shape=jax.ShapeDtypeStruct((SUBLANES, n), F32),
        compiler_params=_cparams(("arbitrary",)),
        name="mod",
    )(cin, w, bias)


def _ctx_kernel(x_ref, g_ref, sh_ref, sc_ref, w_ref, k_ref, v_ref):
    h = _norm_mod(x_ref[...], g_ref[...], sh_ref[...], sc_ref[...])
    kv = jnp.dot(h.astype(BF16), w_ref[...], preferred_element_type=F32)
    k_ref[...] = kv[:, :KV_WIDTH].astype(BF16)
    v_ref[...] = kv[:, KV_WIDTH:].astype(BF16)


def _ctx_call(ctx, g, sh, sc, w_kv):
    bsz, n, d = ctx.shape
    vec = pl.BlockSpec((1, d), lambda b: (0, 0))
    kv_spec = pl.BlockSpec((None, n, KV_WIDTH), lambda b: (b, 0, 0))
    return pl.pallas_call(
        _ctx_kernel,
        grid=(bsz,),
        in_specs=[pl.BlockSpec((None, n, d), lambda b: (b, 0, 0)), vec, vec, vec,
                  pl.BlockSpec((d, 2 * KV_WIDTH), lambda b: (0, 0))],
        out_specs=[kv_spec, kv_spec],
        out_shape=[jax.ShapeDtypeStruct((bsz, n, KV_WIDTH), BF16)] * 2,
        compiler_params=_cparams(("arbitrary",)),
        name="ctx_kv",
    )(ctx, g, sh, sc, w_kv)


def _rope(t, cos, sin_signed):
    lane = lax.broadcasted_iota(jnp.int32, t.shape, 1)
    first = (lane % HEAD_DIM) < (HEAD_DIM // 2)
    swapped = jnp.where(first, pltpu.roll(t, LANES - HEAD_DIM // 2, 1), pltpu.roll(t, HEAD_DIM // 2, 1))
    return t * cos + swapped * sin_signed


def _in_kernel(x_ref, g_ref, sh_ref, sc_ref, w_ref, cos_ref, sin_ref,
               q_ref, k_ref, v_ref, bg_ref, u_ref, *, conv_width):
    h = _norm_mod(x_ref[...], g_ref[...], sh_ref[...], sc_ref[...]).astype(BF16)
    cos = cos_ref[...]
    sin = sin_ref[...]
    scale = HEAD_DIM ** -0.5
    for j in range(ATT_WIDTH // LANES):
        qj = jnp.dot(h, w_ref[:, j * LANES:(j + 1) * LANES], preferred_element_type=F32)
        q_ref[:, j * LANES:(j + 1) * LANES] = (_rope(qj, cos, sin) * scale).astype(BF16)
    o = ATT_WIDTH
    kk = jnp.dot(h, w_ref[:, o:o + KV_WIDTH], preferred_element_type=F32)
    k_ref[...] = _rope(kk, cos, sin).astype(BF16)
    o += KV_WIDTH
    v_ref[...] = jnp.dot(h, w_ref[:, o:o + KV_WIDTH], preferred_element_type=F32).astype(BF16)
    o += KV_WIDTH
    bg_ref[...] = jnp.dot(h, w_ref[:, o:o + conv_width], preferred_element_type=F32)
    o += conv_width
    cg = jnp.dot(h, w_ref[:, o:o + conv_width], preferred_element_type=F32)
    o += conv_width
    hv = jnp.dot(h, w_ref[:, o:o + conv_width], preferred_element_type=F32)
    u_ref[...] = cg * hv


def _in_call(x, g, sh, sc, w_in, cos_t, sin_t, tm=512):
    bsz, seq, d = x.shape
    n_in = w_in.shape[1]
    cw = d - ATT_WIDTH
    vec = pl.BlockSpec((1, d), lambda b, i: (0, 0))
    bvec = pl.BlockSpec((None, 1, d), lambda b, i: (b, 0, 0))
    tab = pl.BlockSpec((tm, LANES), lambda b, i: (i, 0))

    def rows(width):
        return pl.BlockSpec((None, tm, width), lambda b, i: (b, i, 0))

    return pl.pallas_call(
        functools.partial(_in_kernel, conv_width=cw),
        grid=(bsz, seq // tm),
        in_specs=[rows(d), vec, bvec, bvec, pl.BlockSpec((d, n_in), lambda b, i: (0, 0)), tab, tab],
        out_specs=[rows(ATT_WIDTH), rows(KV_WIDTH), rows(KV_WIDTH), rows(cw), rows(cw)],
        out_shape=[jax.ShapeDtypeStruct((bsz, seq, ATT_WIDTH), BF16),
                   jax.ShapeDtypeStruct((bsz, seq, KV_WIDTH), BF16),
                   jax.ShapeDtypeStruct((bsz, seq, KV_WIDTH), BF16),
                   jax.ShapeDtypeStruct((bsz, seq, cw), F32),
                   jax.ShapeDtypeStruct((bsz, seq, cw), F32)],
        compiler_params=_cparams(("arbitrary", "arbitrary")),
        name="in_proj",
    )(x, g, sh, sc, w_in, cos_t, sin_t)


def _attn_kernel(sink_ref, q_ref, k_ref, v_ref, kc_ref, vc_ref, o_ref, *, seq, tq):
    i = pl.program_id(1)
    nblk = tq // BLOCK
    span = 3 * BLOCK
    nt = (((1,), (1,)), ((), ()))
    kc = kc_ref[...]
    vc = vc_ref[...]
    for blk in range(nblk):
        n = i * nblk + blk
        ws = pl.multiple_of(jnp.clip((n - 1) * BLOCK, 0, seq - span), BLOCK)
        kw = k_ref[pl.ds(ws, span), :]
        vw = v_ref[pl.ds(ws, span), :]
        qb = q_ref[blk * BLOCK:(blk + 1) * BLOCK, :]
        qpos = n * BLOCK + lax.broadcasted_iota(jnp.int32, (BLOCK, span), 0)
        kpos = ws + lax.broadcasted_iota(jnp.int32, (BLOCK, span), 1)
        valid1 = jnp.abs(kpos - qpos) <= WINDOW
        valid = jnp.concatenate([valid1] * GQA_GROUP, axis=0)
        outs = []
        for kh in range(N_KV_HEADS):
            heads = [kh * GQA_GROUP + g for g in range(GQA_GROUP)]
            qs = jnp.concatenate([qb[:, h * HEAD_DIM:(h + 1) * HEAD_DIM] for h in heads], axis=0)
            lo, hi = kh * HEAD_DIM, (kh + 1) * HEAD_DIM
            s_loc = lax.dot_general(qs, kw[:, lo:hi], nt, preferred_element_type=F32)
            s_ctx = lax.dot_general(qs, kc[:, lo:hi], nt, preferred_element_type=F32)
            s_loc = jnp.where(valid, s_loc, NEG)
            sk = jnp.concatenate([jnp.full((BLOCK, 1), sink_ref[h], F32) for h in heads], axis=0)
            m = jnp.maximum(jnp.maximum(jnp.max(s_loc, axis=1, keepdims=True),
                                        jnp.max(s_ctx, axis=1, keepdims=True)), sk)
            p_loc = jnp.exp(s_loc - m)
            p_ctx = jnp.exp(s_ctx - m)
            den = (jnp.sum(p_loc, axis=1, keepdims=True) + jnp.sum(p_ctx, axis=1, keepdims=True)
                   + jnp.exp(sk - m))
            o = (jnp.dot(p_loc.astype(BF16), vw[:, lo:hi], preferred_element_type=F32)
                 + jnp.dot(p_ctx.astype(BF16), vc[:, lo:hi], preferred_element_type=F32))
            o = o / den
            outs.extend(o[g * BLOCK:(g + 1) * BLOCK] for g in range(GQA_GROUP))
        o_ref[blk * BLOCK:(blk + 1) * BLOCK, :] = jnp.concatenate(outs, axis=1).astype(BF16)


def _attn_call(sink, q, k, v, kc, vc, tq=512):
    bsz, seq, _ = q.shape
    n_ctx = kc.shape[1]
    full_kv = pl.BlockSpec((None, seq, KV_WIDTH), lambda b, i: (b, 0, 0))
    ctx_kv = pl.BlockSpec((None, n_ctx, KV_WIDTH), lambda b, i: (b, 0, 0))
    qspec = pl.BlockSpec((None, tq, ATT_WIDTH), lambda b, i: (b, i, 0))
    return pl.pallas_call(
        functools.partial(_attn_kernel, seq=seq, tq=tq),
        grid=(bsz, seq // tq),
        in_specs=[pl.BlockSpec(memory_space=pltpu.SMEM), qspec, full_kv, full_kv, ctx_kv, ctx_kv],
        out_specs=qspec,
        out_shape=jax.ShapeDtypeStruct((bsz, seq, ATT_WIDTH), BF16),
        compiler_params=_cparams(("arbitrary", "arbitrary")),
        name="attn",
    )(sink, q, k, v, kc, vc)


def _out_kernel(att_ref, u_ref, up_ref, un_ref, bg_ref, cw_ref, wo_ref, x_ref, gt_ref,
                g_ref, sh_ref, sc_ref, wr_ref, x1_ref, h2_ref, aff_ref, *, tm):
    i = pl.program_id(1)
    last = pl.num_programs(1) - 1
    u = u_ref[...]
    row = lax.broadcasted_iota(jnp.int32, u.shape, 0)
    prev_row = jnp.where(i > 0, up_ref[SUBLANES - 1:SUBLANES, :], 0.0)
    next_row = jnp.where(i < last, un_ref[0:1, :], 0.0)
    u_prev = jnp.where(row == 0, prev_row, pltpu.roll(u, 1, 0))
    u_next = jnp.where(row == tm - 1, next_row, pltpu.roll(u, tm - 1, 0))
    conv = bg_ref[...] * (cw_ref[0:1, :] * u_prev + cw_ref[1:2, :] * u + cw_ref[2:3, :] * u_next)
    y = (jnp.dot(att_ref[...], wo_ref[0:ATT_WIDTH, :], preferred_element_type=F32)
         + jnp.dot(conv.astype(BF16), wo_ref[ATT_WIDTH:, :], preferred_element_type=F32))
    x1 = x_ref[...] + gt_ref[...] * y
    x1_ref[...] = x1
    h2 = _norm_mod(x1, g_ref[...], sh_ref[...], sc_ref[...])
    h2_ref[...] = h2
    nt = (((1,), (1,)), ((), ()))
    h_hi = h2.astype(BF16)
    h_lo = (h2 - h_hi.astype(F32)).astype(BF16)
    wr = wr_ref[...]
    w_hi = wr.astype(BF16)
    w_lo = (wr - w_hi.astype(F32)).astype(BF16)
    logits = (lax.dot_general(w_hi, h_hi, nt, preferred_element_type=F32)
              + lax.dot_general(w_lo, h_hi, nt, preferred_element_type=F32)
              + lax.dot_general(w_hi, h_lo, nt, preferred_element_type=F32))
    mx = jnp.max(logits, axis=0, keepdims=True)
    ex = jnp.exp(logits - mx)
    aff_ref[...] = ex / jnp.sum(ex, axis=0, keepdims=True)


def _out_call(att, u, bg, conv_w, w_out, x, gt1, g_ffn, sh2, sc2, w_router_t, tm=512):
    bsz, seq, d = x.shape
    cw = u.shape[-1]
    n_exp = w_router_t.shape[0]
    nh = seq // SUBLANES

    def rows(width):
        return pl.BlockSpec((None, tm, width), lambda b, i: (b, i, 0))

    halo_prev = pl.BlockSpec((None, SUBLANES, cw),
                             lambda b, i: (b, jnp.maximum(i * (tm // SUBLANES) - 1, 0), 0))
    halo_next = pl.BlockSpec((None, SUBLANES, cw),
                             lambda b, i: (b, jnp.minimum((i + 1) * (tm // SUBLANES), nh - 1), 0))
    vec = pl.BlockSpec((1, d), lambda b, i: (0, 0))
    bvec = pl.BlockSpec((None, 1, d), lambda b, i: (b, 0, 0))
    return pl.pallas_call(
        functools.partial(_out_kernel, tm=tm),
        grid=(bsz, seq // tm),
        in_specs=[rows(ATT_WIDTH), rows(cw), halo_prev, halo_next, rows(cw),
                  pl.BlockSpec((3, cw), lambda b, i: (0, 0)),
                  pl.BlockSpec((d, d), lambda b, i: (0, 0)),
                  rows(d), bvec, vec, bvec, bvec,
                  pl.BlockSpec((n_exp, d), lambda b, i: (0, 0))],
        out_specs=[rows(d), rows(d), pl.BlockSpec((None, n_exp, tm), lambda b, i: (b, 0, i))],
        out_shape=[jax.ShapeDtypeStruct((bsz, seq, d), F32),
                   jax.ShapeDtypeStruct((bsz, seq, d), F32),
                   jax.ShapeDtypeStruct((bsz, n_exp, seq), F32)],
        compiler_params=_cparams(("arbitrary", "arbitrary")),
        name="out_proj",
    )(att, u, u, u, bg, conv_w, w_out, x, gt1, g_ffn, sh2, sc2, w_router_t)


def _prefix_rows(m, tri, ones, low):
    n_exp, n_chunk, _ = m.shape
    mb = m.astype(BF16).reshape(n_exp * n_chunk, LANES)
    within = jnp.dot(mb, tri, preferred_element_type=F32).reshape(n_exp, n_chunk, LANES)
    tot = jnp.dot(mb, ones, preferred_element_type=F32).astype(BF16).reshape(n_exp, n_chunk, LANES)
    carry = jnp.stack([jnp.dot(low, tot[e], preferred_element_type=F32) for e in range(n_exp)], axis=0)
    return carry + within, carry


def _sel_kernel(aff_ref, pos_ref, carry_ref, *, cap):
    a = aff_ref[...]
    n_exp, n_chunk, _ = a.shape
    bits = lax.bitcast_convert_type(a, jnp.int32)

    def count(pred):
        c = jnp.sum(pred.astype(F32), axis=2, keepdims=True)
        return jnp.sum(c, axis=1, keepdims=True)

    def body(it, thr):
        cand = thr | jnp.left_shift(jnp.int32(1), 30 - it)
        return jnp.where(count(bits >= cand) >= cap, cand, thr)

    thr = lax.fori_loop(0, 31, body, jnp.zeros((n_exp, 1, 1), jnp.int32))
    gt = bits > thr
    eq = bits == thr
    need = cap - count(gt)

    r = lax.broadcasted_iota(jnp.int32, (LANES, LANES), 0)
    c = lax.broadcasted_iota(jnp.int32, (LANES, LANES), 1)
    tri = (r < c).astype(BF16)
    ones = jnp.ones((LANES, LANES), BF16)
    rr = lax.broadcasted_iota(jnp.int32, (n_chunk, n_chunk), 0)
    cc = lax.broadcasted_iota(jnp.int32, (n_chunk, n_chunk), 1)
    low = (cc < rr).astype(BF16)

    eq_rank, _ = _prefix_rows(eq, tri, ones, low)
    sel = gt | (eq & (eq_rank < need))
    pos, carry = _prefix_rows(sel, tri, ones, low)
    pos_ref[...] = jnp.where(sel, pos, -1.0)
    carry_ref[...] = carry


def _sel_call(aff4, cap):
    bsz, n_exp, n_chunk, _ = aff4.shape
    spec = pl.BlockSpec((None, n_exp, n_chunk, LANES), lambda b: (b, 0, 0, 0))
    return pl.pallas_call(
        functools.partial(_sel_kernel, cap=cap),
        grid=(bsz,),
        in_specs=[spec],
        out_specs=[spec, spec],
        out_shape=[jax.ShapeDtypeStruct(aff4.shape, F32)] * 2,
        compiler_params=_cparams(("arbitrary",)),
        name="select",
    )(aff4)


COMPACT_ROWS = 16
SLOT_BLOCK = LANES


def _compact_kernel(cum_ref, pos_ref, aff_ref, out_ref, *, n_exp, n_tchunk):
    b = pl.program_id(0)
    e = pl.program_id(1)
    base = (b * n_exp + e) * (n_tchunk + 1)
    out_ref[...] = jnp.zeros(out_ref.shape, F32)
    nt = (((1,), (1,)), ((), ()))
    rowi = lax.broadcasted_iota(jnp.int32, (COMPACT_ROWS, LANES), 0)
    lane = lax.broadcasted_iota(jnp.int32, (COMPACT_ROWS, LANES), 1).astype(F32)
    slot = lax.broadcasted_iota(jnp.int32, (SLOT_BLOCK, LANES), 0)

    def chunk_body(c, carry):
        lo = cum_ref[base + c]
        hi = cum_ref[base + c + 1]
        p = pos_ref[c]
        a = aff_ref[c]
        g_hi = a.astype(BF16).astype(F32)
        r1 = a - g_hi
        g_mid = r1.astype(BF16).astype(F32)
        g_lo = r1 - g_mid
        pieces = []
        for r in range(SUBLANES):
            t_hi = jnp.asarray(c * SUBLANES + r).astype(F32)

            def bc(t):
                return jnp.broadcast_to(t[r:r + 1, :], (COMPACT_ROWS, LANES))

            piece = jnp.where(rowi == 0, t_hi,
                              jnp.where(rowi == 1, lane,
                                        jnp.where(rowi == 2, bc(g_hi),
                                                  jnp.where(rowi == 3, bc(g_mid),
                                                            jnp.where(rowi == 4, bc(g_lo), 0.0)))))
            pieces.append(piece)
        vals = jnp.concatenate(pieces, axis=1).astype(BF16)

        def slot_body(j, carry2):
            sl = (slot + j * SLOT_BLOCK).astype(F32)
            oh = jnp.concatenate(
                [jnp.where(jnp.broadcast_to(p[r:r + 1, :], (SLOT_BLOCK, LANES)) == sl, 1.0, 0.0)
                 for r in range(SUBLANES)], axis=1).astype(BF16)
            out_ref[j] += lax.dot_general(vals, oh, nt, preferred_element_type=F32)
            return carry2

        j_lo = lo // SLOT_BLOCK
        j_hi = jnp.where(hi > lo, (hi - 1) // SLOT_BLOCK + 1, j_lo)
        lax.fori_loop(j_lo, j_hi, slot_body, 0)
        return carry

    lax.fori_loop(0, n_tchunk, chunk_body, 0)


def _compact_call(cum, pos5, aff5, cap):
    bsz, n_exp, n_tchunk, _, _ = pos5.shape
    n_sblk = cap // SLOT_BLOCK
    spec = pl.BlockSpec((None, None, n_tchunk, SUBLANES, LANES), lambda b, e, cum_ref: (b, e, 0, 0, 0))
    return pl.pallas_call(
        functools.partial(_compact_kernel, n_exp=n_exp, n_tchunk=n_tchunk),
        grid_spec=pltpu.PrefetchScalarGridSpec(
            num_scalar_prefetch=1,
            grid=(bsz, n_exp),
            in_specs=[spec, spec],
            out_specs=pl.BlockSpec((None, None, n_sblk, COMPACT_ROWS, LANES),
                                   lambda b, e, cum_ref: (b, e, 0, 0, 0)),
        ),
        out_shape=jax.ShapeDtypeStruct((bsz, n_exp, n_sblk, COMPACT_ROWS, LANES), F32),
        compiler_params=_cparams(("arbitrary", "arbitrary")),
        name="compact",
    )(cum, pos5, aff5)


FFN_ROWS = 256


def _ffn_kernel(idx_ref, h2_hbm, acc_in, g_ref, gt_ref, wg_ref, wu_ref, wd_ref, acc_hbm,
                xs, obuf, wgb, wub, wdb, sem, *, seq, cap, n_exp, bsz):
    del acc_in
    e = pl.program_id(0)
    b = pl.program_id(1)
    step = e * bsz + b
    base = (b * n_exp + e) * cap
    rowbase = b * seq

    def scatter_done():
        pltpu.make_async_copy(obuf, acc_hbm.at[pl.ds(0, cap)], sem.at[2]).wait()

    @pl.when(step > 0)
    def _():
        scatter_done()

    def issue(s, carry):
        row = rowbase + idx_ref[base + s]
        pltpu.make_async_copy(h2_hbm.at[pl.ds(row, 1)], xs.at[pl.ds(s, 1)], sem.at[0]).start()
        pltpu.make_async_copy(acc_hbm.at[pl.ds(row, 1)], obuf.at[pl.ds(s, 1)], sem.at[1]).start()
        return carry

    lax.fori_loop(0, cap, issue, 0)

    @pl.when(b == 0)
    def _():
        wgb[...] = wg_ref[...].astype(BF16)
        wub[...] = wu_ref[...].astype(BF16)
        wdb[...] = wd_ref[...].astype(BF16)

    pltpu.make_async_copy(h2_hbm.at[pl.ds(0, cap)], xs, sem.at[0]).wait()
    pltpu.make_async_copy(acc_hbm.at[pl.ds(0, cap)], obuf, sem.at[1]).wait()

    gt = gt_ref[...]
    for ci in range(cap // FFN_ROWS):
        rows = slice(ci * FFN_ROWS, (ci + 1) * FFN_ROWS)
        xc = xs[rows, :].astype(BF16)
        a = jnp.dot(xc, wgb[...], preferred_element_type=F32)
        u = jnp.dot(xc, wub[...], preferred_element_type=F32)
        hm = (a * jax.nn.sigmoid(a) * u).astype(BF16)
        y = jnp.dot(hm, wdb[...], preferred_element_type=F32) * g_ref[rows, :]
        obuf[rows, :] = obuf[rows, :] + gt * y

    def scatter(s, carry):
        row = rowbase + idx_ref[base + s]
        pltpu.make_async_copy(obuf.at[pl.ds(s, 1)], acc_hbm.at[pl.ds(row, 1)], sem.at[2]).start()
        return carry

    lax.fori_loop(0, cap, scatter, 0)

    @pl.when(step == n_exp * bsz - 1)
    def _():
        scatter_done()


def _ffn_call(idx, h2, x1, gate, gt2, w_gate, w_up, w_down, cap):
    bsz, seq, d = x1.shape
    n_exp, _, f = w_gate.shape
    kern = functools.partial(_ffn_kernel, seq=seq, cap=cap, n_exp=n_exp, bsz=bsz)
    any_spec = pl.BlockSpec(memory_space=pl.ANY)
    out = pl.pallas_call(
        kern,
        grid_spec=pltpu.PrefetchScalarGridSpec(
            num_scalar_prefetch=1,
            grid=(n_exp, bsz),
            in_specs=[any_spec, any_spec,
                      pl.BlockSpec((None, None, cap, 1), lambda e, b, idx_ref: (b, e, 0, 0)),
                      pl.BlockSpec((None, 1, d), lambda e, b, idx_ref: (b, 0, 0)),
                      pl.BlockSpec((None, d, f), lambda e, b, idx_ref: (e, 0, 0)),
                      pl.BlockSpec((None, d, f), lambda e, b, idx_ref: (e, 0, 0)),
                      pl.BlockSpec((None, f, d), lambda e, b, idx_ref: (e, 0, 0))],
            out_specs=any_spec,
            scratch_shapes=[pltpu.VMEM((cap, d), F32), pltpu.VMEM((cap, d), F32),
                            pltpu.VMEM((d, f), BF16), pltpu.VMEM((d, f), BF16), pltpu.VMEM((f, d), BF16),
                            pltpu.SemaphoreType.DMA((3,))],
        ),
        out_shape=jax.ShapeDtypeStruct((bsz * seq, d), F32),
        input_output_aliases={2: 0},
        compiler_params=_cparams(("arbitrary", "arbitrary")),
        name="ffn",
    )(idx, h2.reshape(bsz * seq, d), x1.reshape(bsz * seq, d), gate, gt2, w_gate, w_up, w_down)
    return out.reshape(bsz, seq, d)


def _final_kernel(x_ref, g_ref, o_ref):
    xf = x_ref[...]
    o_ref[...] = xf * lax.rsqrt(jnp.mean(xf * xf, axis=-1, keepdims=True) + EPS) * g_ref[...]


def _final_call(x, g, tm=1024):
    bsz, seq, d = x.shape
    spec = pl.BlockSpec((None, tm, d), lambda b, i: (b, i, 0))
    return pl.pallas_call(
        _final_kernel,
        grid=(bsz, seq // tm),
        in_specs=[spec, pl.BlockSpec((1, d), lambda b, i: (0, 0))],
        out_specs=spec,
        out_shape=jax.ShapeDtypeStruct(x.shape, F32),
        compiler_params=_cparams(("arbitrary", "arbitrary")),
        name="final",
    )(x, g)


def _rope_tables(seq):
    t = np.arange(seq)
    n_freq = HEAD_DIM // 4
    inv = ROPE_THETA ** (-np.arange(n_freq, dtype=np.float64) / n_freq)
    ang = np.concatenate([(t // GRID_W)[:, None] * inv, (t % GRID_W)[:, None] * inv], axis=1)
    cos_h = np.concatenate([np.cos(ang), np.cos(ang)], axis=1)
    sin_h = np.concatenate([-np.sin(ang), np.sin(ang)], axis=1)
    reps = LANES // HEAD_DIM
    return (jnp.asarray(np.tile(cos_h, (1, reps)), F32), jnp.asarray(np.tile(sin_h, (1, reps)), F32))


def kernel(x, c, ctx, c_ctx, w_ada, b_ada, g_mix, w_in, conv_w, sink, w_out, g_ffn,
           w_router, w_gate, w_up, w_down, g_final):
    bsz, seq, d = x.shape
    n_exp = w_router.shape[-1]
    cap = EC_CAPACITY_FACTOR * seq // n_exp
    assert w_ada.shape[0] == 1, "single trunk layer"
    assert bsz + 1 <= SUBLANES and seq % (SUBLANES * LANES) == 0 and cap % SLOT_BLOCK == 0

    cin = jnp.zeros((SUBLANES, d), F32).at[:bsz].set(c).at[bsz].set(c_ctx)
    mod = _mod_call(cin, w_ada[0], b_ada[0][None, :])
    sh1, sc1, gt1, sh2, sc2, gt2 = [m[:bsz, None, :] for m in jnp.split(mod, 6, axis=-1)]
    csh1 = mod[bsz:bsz + 1, 0:d]
    csc1 = mod[bsz:bsz + 1, d:2 * d]

    w_in_b = w_in[0].astype(BF16)
    g_mix_r = g_mix[0][None, :]
    kv0 = ATT_WIDTH
    k_c, v_c = _ctx_call(ctx, g_mix_r, csh1, csc1, w_in_b[:, kv0:kv0 + 2 * KV_WIDTH])

    cos_t, sin_t = _rope_tables(seq)
    q, k, v, bg, u = _in_call(x, g_mix_r, sh1, sc1, w_in_b, cos_t, sin_t)
    att = _attn_call(sink[0], q, k, v, k_c, v_c)
    x1, h2, aff = _out_call(att, u, bg, conv_w[0], w_out[0].astype(BF16), x, gt1,
                            g_ffn[0][None, :], sh2, sc2, w_router[0].T)

    n_chunk = seq // LANES
    aff4 = aff.reshape(bsz, n_exp, n_chunk, LANES)
    pos, carry = _sel_call(aff4, cap)
    n_tchunk = n_chunk // SUBLANES
    cum = jnp.concatenate([carry[:, :, ::SUBLANES, 0], jnp.full((bsz, n_exp, 1), cap, F32)], axis=-1)
    cum = cum.astype(jnp.int32).reshape(-1)
    comp = _compact_call(cum, pos.reshape(bsz, n_exp, n_tchunk, SUBLANES, LANES),
                         aff4.reshape(bsz, n_exp, n_tchunk, SUBLANES, LANES), cap)
    idx = (comp[:, :, :, 0, :] * LANES + comp[:, :, :, 1, :]).astype(jnp.int32).reshape(-1)
    gate = ((comp[:, :, :, 2, :] + comp[:, :, :, 3, :]) + comp[:, :, :, 4, :]).reshape(bsz, n_exp, cap, 1)

    x2 = _ffn_call(idx, h2, x1, gate, gt2, w_gate[0], w_up[0], w_down[0], cap)
    return _final_call(x2, g_final[None, :])
```

```python
import functools

import numpy as np
import jax
import jax.numpy as jnp
from jax import lax
from jax.experimental import pallas as pl
from jax.experimental.pallas import tpu as pltpu

GRID_W = 64
N_HEADS = 8
N_KV_HEADS = 2
HEAD_DIM = 64
GQA_GROUP = N_HEADS // N_KV_HEADS
ATT_WIDTH = N_HEADS * HEAD_DIM
KV_WIDTH = N_KV_HEADS * HEAD_DIM
WINDOW = 128
BLOCK = 128
ROPE_THETA = 10000.0
N_EXPERTS = 16
EC_CAPACITY_FACTOR = 2
EPS = 1e-6

LANES = 128
SUBLANES = 8
VMEM_LIMIT = 56 * 1024 * 1024

F32 = jnp.float32
BF16 = jnp.bfloat16
NEG = -1e30


def _cparams(sem):
    return pltpu.CompilerParams(dimension_semantics=sem, vmem_limit_bytes=VMEM_LIMIT)


def _norm_mod(xf, g, shift, scale):
    r = xf * lax.rsqrt(jnp.mean(xf * xf, axis=-1, keepdims=True) + EPS)
    return (r * g) * (1.0 + scale) + shift


def _mod_kernel(c_ref, w_ref, b_ref, o_ref):
    c = c_ref[...]
    s = c * jax.nn.sigmoid(c)
    o_ref[...] = jnp.dot(s, w_ref[...], precision=lax.Precision.HIGHEST,
                         preferred_element_type=F32) + b_ref[...]


def _mod_call(cin, w, bias, tn=1024):
    d, n = w.shape
    return pl.pallas_call(
        _mod_kernel,
        grid=(n // tn,),
        in_specs=[pl.BlockSpec((SUBLANES, d), lambda j: (0, 0)),
                  pl.BlockSpec((d, tn), lambda j: (0, j)),
                  pl.BlockSpec((1, tn), lambda j: (0, j))],
        out_specs=pl.BlockSpec((SUBLANES, tn), lambda j: (0, j)),
        out_shape=jax.ShapeDtypeStruct((SUBLANES, n), F32),
        compiler_params=_cparams(("arbitrary",)),
        name="mod",
    )(cin, w, bias)


def _ctx_kernel(x_ref, g_ref, sh_ref, sc_ref, w_ref, k_ref, v_ref):
    h = _norm_mod(x_ref[...], g_ref[...], sh_ref[...], sc_ref[...])
    kv = jnp.dot(h.astype(BF16), w_ref[...], preferred_element_type=F32)
    k_ref[...] = kv[:, :KV_WIDTH].astype(BF16)
    v_ref[...] = kv[:, KV_WIDTH:].astype(BF16)


def _ctx_call(ctx, g, sh, sc, w_kv):
    bsz, n, d = ctx.shape
    vec = pl.BlockSpec((1, d), lambda b: (0, 0))
    kv_spec = pl.BlockSpec((None, n, KV_WIDTH), lambda b: (b, 0, 0))
    return pl.pallas_call(
        _ctx_kernel,
        grid=(bsz,),
        in_specs=[pl.BlockSpec((None, n, d), lambda b: (b, 0, 0)), vec, vec, vec,
                  pl.BlockSpec((d, 2 * KV_WIDTH), lambda b: (0, 0))],
        out_specs=[kv_spec, kv_spec],
        out_shape=[jax.ShapeDtypeStruct((bsz, n, KV_WIDTH), BF16)] * 2,
        compiler_params=_cparams(("arbitrary",)),
        name="ctx_kv",
    )(ctx, g, sh, sc, w_kv)


def _rope(t, cos, sin_signed):
    lane = lax.broadcasted_iota(jnp.int32, t.shape, 1)
    first = (lane % HEAD_DIM) < (HEAD_DIM // 2)
    swapped = jnp.where(first, pltpu.roll(t, LANES - HEAD_DIM // 2, 1), pltpu.roll(t, HEAD_DIM // 2, 1))
    return t * cos + swapped * sin_signed


def _in_kernel(x_ref, g_ref, sh_ref, sc_ref, w_ref, cos_ref, sin_ref,
               q_ref, k_ref, v_ref, bg_ref, u_ref, *, conv_width):
    h = _norm_mod(x_ref[...], g_ref[...], sh_ref[...], sc_ref[...]).astype(BF16)
    cos = cos_ref[...]
    sin = sin_ref[...]
    scale = HEAD_DIM ** -0.5
    for j in range(ATT_WIDTH // LANES):
        qj = jnp.dot(h, w_ref[:, j * LANES:(j + 1) * LANES], preferred_element_type=F32)
        q_ref[:, j * LANES:(j + 1) * LANES] = (_rope(qj, cos, sin) * scale).astype(BF16)
    o = ATT_WIDTH
    kk = jnp.dot(h, w_ref[:, o:o + KV_WIDTH], preferred_element_type=F32)
    k_ref[...] = _rope(kk, cos, sin).astype(BF16)
    o += KV_WIDTH
    v_ref[...] = jnp.dot(h, w_ref[:, o:o + KV_WIDTH], preferred_element_type=F32).astype(BF16)
    o += KV_WIDTH
    bg_ref[...] = jnp.dot(h, w_ref[:, o:o + conv_width], preferred_element_type=F32)
    o += conv_width
    cg = jnp.dot(h, w_ref[:, o:o + conv_width], preferred_element_type=F32)
    o += conv_width
    hv = jnp.dot(h, w_ref[:, o:o + conv_width], preferred_element_type=F32)
    u_ref[...] = cg * hv


def _in_call(x, g, sh, sc, w_in, cos_t, sin_t, tm=512):
    bsz, seq, d = x.shape
    n_in = w_in.shape[1]
    cw = d - ATT_WIDTH
    vec = pl.BlockSpec((1, d), lambda b, i: (0, 0))
    bvec = pl.BlockSpec((None, 1, d), lambda b, i: (b, 0, 0))
    tab = pl.BlockSpec((tm, LANES), lambda b, i: (i, 0))

    def rows(width):
        return pl.BlockSpec((None, tm, width), lambda b, i: (b, i, 0))

    return pl.pallas_call(
        functools.partial(_in_kernel, conv_width=cw),
        grid=(bsz, seq // tm),
        in_specs=[rows(d), vec, bvec, bvec, pl.BlockSpec((d, n_in), lambda b, i: (0, 0)), tab, tab],
        out_specs=[rows(ATT_WIDTH), rows(KV_WIDTH), rows(KV_WIDTH), rows(cw), rows(cw)],
        out_shape=[jax.ShapeDtypeStruct((bsz, seq, ATT_WIDTH), BF16),
                   jax.ShapeDtypeStruct((bsz, seq, KV_WIDTH), BF16),
                   jax.ShapeDtypeStruct((bsz, seq, KV_WIDTH), BF16),
                   jax.ShapeDtypeStruct((bsz, seq, cw), F32),
                   jax.ShapeDtypeStruct((bsz, seq, cw), F32)],
        compiler_params=_cparams(("arbitrary", "arbitrary")),
        name="in_proj",
    )(x, g, sh, sc, w_in, cos_t, sin_t)


def _attn_kernel(sink_ref, q_ref, k_ref, v_ref, kc_ref, vc_ref, o_ref, *, seq, tq):
    i = pl.program_id(1)
    nblk = tq // BLOCK
    span = 3 * BLOCK
    nt = (((1,), (1,)), ((), ()))
    kc = kc_ref[...]
    vc = vc_ref[...]
    for blk in range(nblk):
        n = i * nblk + blk
        ws = pl.multiple_of(jnp.clip((n - 1) * BLOCK, 0, seq - span), BLOCK)
        kw = k_ref[pl.ds(ws, span), :]
        vw = v_ref[pl.ds(ws, span), :]
        qb = q_ref[blk * BLOCK:(blk + 1) * BLOCK, :]
        qpos = n * BLOCK + lax.broadcasted_iota(jnp.int32, (BLOCK, span), 0)
        kpos = ws + lax.broadcasted_iota(jnp.int32, (BLOCK, span), 1)
        valid1 = jnp.abs(kpos - qpos) <= WINDOW
        valid = jnp.concatenate([valid1] * GQA_GROUP, axis=0)
        outs = []
        for kh in range(N_KV_HEADS):
            heads = [kh * GQA_GROUP + g for g in range(GQA_GROUP)]
            qs = jnp.concatenate([qb[:, h * HEAD_DIM:(h + 1) * HEAD_DIM] for h in heads], axis=0)
            lo, hi = kh * HEAD_DIM, (kh + 1) * HEAD_DIM
            s_loc = lax.dot_general(qs, kw[:, lo:hi], nt, preferred_element_type=F32)
            s_ctx = lax.dot_general(qs, kc[:, lo:hi], nt, preferred_element_type=F32)
            s_loc = jnp.where(valid, s_loc, NEG)
            sk = jnp.concatenate([jnp.full((BLOCK, 1), sink_ref[h], F32) for h in heads], axis=0)
            m = jnp.maximum(jnp.maximum(jnp.max(s_loc, axis=1, keepdims=True),
                                        jnp.max(s_ctx, axis=1, keepdims=True)), sk)
            p_loc = jnp.exp(s_loc - m)
            p_ctx = jnp.exp(s_ctx - m)
            den = (jnp.sum(p_loc, axis=1, keepdims=True) + jnp.sum(p_ctx, axis=1, keepdims=True)
                   + jnp.exp(sk - m))
            o = (jnp.dot(p_loc.astype(BF16), vw[:, lo:hi], preferred_element_type=F32)
                 + jnp.dot(p_ctx.astype(BF16), vc[:, lo:hi], preferred_element_type=F32))
            o = o / den
            outs.extend(o[g * BLOCK:(g + 1) * BLOCK] for g in range(GQA_GROUP))
        o_ref[blk * BLOCK:(blk + 1) * BLOCK, :] = jnp.concatenate(outs, axis=1).astype(BF16)


def _attn_call(sink, q, k, v, kc, vc, tq=512):
    bsz, seq, _ = q.shape
    n_ctx = kc.shape[1]
    full_kv = pl.BlockSpec((None, seq, KV_WIDTH), lambda b, i: (b, 0, 0))
    ctx_kv = pl.BlockSpec((None, n_ctx, KV_WIDTH), lambda b, i: (b, 0, 0))
    qspec = pl.BlockSpec((None, tq, ATT_WIDTH), lambda b, i: (b, i, 0))
    return pl.pallas_call(
        functools.partial(_attn_kernel, seq=seq, tq=tq),
        grid=(bsz, seq // tq),
        in_specs=[pl.BlockSpec(memory_space=pltpu.SMEM), qspec, full_kv, full_kv, ctx_kv, ctx_kv],
        out_specs=qspec,
        out_shape=jax.ShapeDtypeStruct((bsz, seq, ATT_WIDTH), BF16),
        compiler_params=_cparams(("arbitrary", "arbitrary")),
        name="attn",
    )(sink, q, k, v, kc, vc)


def _out_kernel(att_ref, u_ref, up_ref, un_ref, bg_ref, cw_ref, wo_ref, x_ref, gt_ref,
                g_ref, sh_ref, sc_ref, wr_ref, rows_ref, aff_ref, *, tm):
    i = pl.program_id(1)
    last = pl.num_programs(1) - 1
    u = u_ref[...]
    row = lax.broadcasted_iota(jnp.int32, u.shape, 0)
    prev_row = jnp.where(i > 0, up_ref[SUBLANES - 1:SUBLANES, :], 0.0)
    next_row = jnp.where(i < last, un_ref[0:1, :], 0.0)
    u_prev = jnp.where(row == 0, prev_row, pltpu.roll(u, 1, 0))
    u_next = jnp.where(row == tm - 1, next_row, pltpu.roll(u, tm - 1, 0))
    conv = bg_ref[...] * (cw_ref[0:1, :] * u_prev + cw_ref[1:2, :] * u + cw_ref[2:3, :] * u_next)
    y = (jnp.dot(att_ref[...], wo_ref[0:ATT_WIDTH, :], preferred_element_type=F32)
         + jnp.dot(conv.astype(BF16), wo_ref[ATT_WIDTH:, :], preferred_element_type=F32))
    x1 = x_ref[...] + gt_ref[...] * y
    d = x1.shape[-1]
    h2 = _norm_mod(x1, g_ref[...], sh_ref[...], sc_ref[...])
    rows_ref[:, 0:d] = h2
    rows_ref[:, d:2 * d] = x1
    nt = (((1,), (1,)), ((), ()))
    h_hi = h2.astype(BF16)
    h_lo = (h2 - h_hi.astype(F32)).astype(BF16)
    wr = wr_ref[...]
    w_hi = wr.astype(BF16)
    w_lo = (wr - w_hi.astype(F32)).astype(BF16)
    logits = (lax.dot_general(w_hi, h_hi, nt, preferred_element_type=F32)
              + lax.dot_general(w_lo, h_hi, nt, preferred_element_type=F32)
              + lax.dot_general(w_hi, h_lo, nt, preferred_element_type=F32))
    mx = jnp.max(logits, axis=0, keepdims=True)
    ex = jnp.exp(logits - mx)
    aff_ref[...] = ex / jnp.sum(ex, axis=0, keepdims=True)


def _out_call(att, u, bg, conv_w, w_out, x, gt1, g_ffn, sh2, sc2, w_router_t, tm=512):
    bsz, seq, d = x.shape
    cw = u.shape[-1]
    n_exp = w_router_t.shape[0]
    nh = seq // SUBLANES

    def rows(width):
        return pl.BlockSpec((None, tm, width), lambda b, i: (b, i, 0))

    halo_prev = pl.BlockSpec((None, SUBLANES, cw),
                             lambda b, i: (b, jnp.maximum(i * (tm // SUBLANES) - 1, 0), 0))
    halo_next = pl.BlockSpec((None, SUBLANES, cw),
                             lambda b, i: (b, jnp.minimum((i + 1) * (tm // SUBLANES), nh - 1), 0))
    vec = pl.BlockSpec((1, d), lambda b, i: (0, 0))
    bvec = pl.BlockSpec((None, 1, d), lambda b, i: (b, 0, 0))
    return pl.pallas_call(
        functools.partial(_out_kernel, tm=tm),
        grid=(bsz, seq // tm),
        in_specs=[rows(ATT_WIDTH), rows(cw), halo_prev, halo_next, rows(cw),
                  pl.BlockSpec((3, cw), lambda b, i: (0, 0)),
                  pl.BlockSpec((d, d), lambda b, i: (0, 0)),
                  rows(d), bvec, vec, bvec, bvec,
                  pl.BlockSpec((n_exp, d), lambda b, i: (0, 0))],
        out_specs=[rows(2 * d), pl.BlockSpec((None, n_exp, tm), lambda b, i: (b, 0, i))],
        out_shape=[jax.ShapeDtypeStruct((bsz, seq, 2 * d), F32),
                   jax.ShapeDtypeStruct((bsz, n_exp, seq), F32)],
        compiler_params=_cparams(("arbitrary", "arbitrary")),
        name="out_proj",
    )(att, u, u, u, bg, conv_w, w_out, x, gt1, g_ffn, sh2, sc2, w_router_t)


def _prefix_rows(m, tri, ones, low):
    n_exp, n_chunk, _ = m.shape
    mb = m.astype(BF16).reshape(n_exp * n_chunk, LANES)
    within = jnp.dot(mb, tri, preferred_element_type=F32).reshape(n_exp, n_chunk, LANES)
    tot = jnp.dot(mb, ones, preferred_element_type=F32).astype(BF16).reshape(n_exp, n_chunk, LANES)
    carry = jnp.stack([jnp.dot(low, tot[e], preferred_element_type=F32) for e in range(n_exp)], axis=0)
    return carry + within, carry


def _sel_kernel(aff_ref, pos_ref, carry_ref, *, cap):
    a = aff_ref[...]
    n_exp, n_chunk, _ = a.shape
    bits = lax.bitcast_convert_type(a, jnp.int32)

    def count(pred):
        c = jnp.sum(pred.astype(F32), axis=2, keepdims=True)
        return jnp.sum(c, axis=1, keepdims=True)

    def body(it, thr):
        cand = thr | jnp.left_shift(jnp.int32(1), 30 - it)
        return jnp.where(count(bits >= cand) >= cap, cand, thr)

    thr = lax.fori_loop(0, 31, body, jnp.zeros((n_exp, 1, 1), jnp.int32))
    gt = bits > thr
    eq = bits == thr
    need = cap - count(gt)

    r = lax.broadcasted_iota(jnp.int32, (LANES, LANES), 0)
    c = lax.broadcasted_iota(jnp.int32, (LANES, LANES), 1)
    tri = (r < c).astype(BF16)
    ones = jnp.ones((LANES, LANES), BF16)
    rr = lax.broadcasted_iota(jnp.int32, (n_chunk, n_chunk), 0)
    cc = lax.broadcasted_iota(jnp.int32, (n_chunk, n_chunk), 1)
    low = (cc < rr).astype(BF16)

    eq_rank, _ = _prefix_rows(eq, tri, ones, low)
    sel = gt | (eq & (eq_rank < need))
    pos, carry = _prefix_rows(sel, tri, ones, low)
    pos_ref[...] = jnp.where(sel, pos, -1.0)
    carry_ref[...] = carry


def _sel_call(aff4, cap):
    bsz, n_exp, n_chunk, _ = aff4.shape
    spec = pl.BlockSpec((None, n_exp, n_chunk, LANES), lambda b: (b, 0, 0, 0))
    return pl.pallas_call(
        functools.partial(_sel_kernel, cap=cap),
        grid=(bsz,),
        in_specs=[spec],
        out_specs=[spec, spec],
        out_shape=[jax.ShapeDtypeStruct(aff4.shape, F32)] * 2,
        compiler_params=_cparams(("arbitrary",)),
        name="select",
    )(aff4)


COMPACT_ROWS = 16
SLOT_BLOCK = LANES


def _compact_kernel(cum_ref, pos_ref, aff_ref, out_ref, *, n_exp, n_tchunk):
    b = pl.program_id(0)
    e = pl.program_id(1)
    base = (b * n_exp + e) * (n_tchunk + 1)
    out_ref[...] = jnp.zeros(out_ref.shape, F32)
    nt = (((1,), (1,)), ((), ()))
    rowi = lax.broadcasted_iota(jnp.int32, (COMPACT_ROWS, LANES), 0)
    lane = lax.broadcasted_iota(jnp.int32, (COMPACT_ROWS, LANES), 1).astype(F32)
    slot = lax.broadcasted_iota(jnp.int32, (SLOT_BLOCK, LANES), 0)

    def chunk_body(c, carry):
        lo = cum_ref[base + c]
        hi = cum_ref[base + c + 1]
        p = pos_ref[c]
        a = aff_ref[c]
        g_hi = a.astype(BF16).astype(F32)
        r1 = a - g_hi
        g_mid = r1.astype(BF16).astype(F32)
        g_lo = r1 - g_mid
        pieces = []
        for r in range(SUBLANES):
            t_hi = jnp.asarray(c * SUBLANES + r).astype(F32)

            def bc(t):
                return jnp.broadcast_to(t[r:r + 1, :], (COMPACT_ROWS, LANES))

            piece = jnp.where(rowi == 0, t_hi,
                              jnp.where(rowi == 1, lane,
                                        jnp.where(rowi == 2, bc(g_hi),
                                                  jnp.where(rowi == 3, bc(g_mid),
                                                            jnp.where(rowi == 4, bc(g_lo), 0.0)))))
            pieces.append(piece)
        vals = jnp.concatenate(pieces, axis=1).astype(BF16)

        def slot_body(j, carry2):
            sl = (slot + j * SLOT_BLOCK).astype(F32)
            oh = jnp.concatenate(
                [jnp.where(jnp.broadcast_to(p[r:r + 1, :], (SLOT_BLOCK, LANES)) == sl, 1.0, 0.0)
                 for r in range(SUBLANES)], axis=1).astype(BF16)
            out_ref[j] += lax.dot_general(vals, oh, nt, preferred_element_type=F32)
            return carry2

        j_lo = lo // SLOT_BLOCK
        j_hi = jnp.where(hi > lo, (hi - 1) // SLOT_BLOCK + 1, j_lo)
        lax.fori_loop(j_lo, j_hi, slot_body, 0)
        return carry

    lax.fori_loop(0, n_tchunk, chunk_body, 0)


def _compact_call(cum, pos5, aff5, cap):
    bsz, n_exp, n_tchunk, _, _ = pos5.shape
    n_sblk = cap // SLOT_BLOCK
    spec = pl.BlockSpec((None, None, n_tchunk, SUBLANES, LANES), lambda b, e, cum_ref: (b, e, 0, 0, 0))
    return pl.pallas_call(
        functools.partial(_compact_kernel, n_exp=n_exp, n_tchunk=n_tchunk),
        grid_spec=pltpu.PrefetchScalarGridSpec(
            num_scalar_prefetch=1,
            grid=(bsz, n_exp),
            in_specs=[spec, spec],
            out_specs=pl.BlockSpec((None, None, n_sblk, COMPACT_ROWS, LANES),
                                   lambda b, e, cum_ref: (b, e, 0, 0, 0)),
        ),
        out_shape=jax.ShapeDtypeStruct((bsz, n_exp, n_sblk, COMPACT_ROWS, LANES), F32),
        compiler_params=_cparams(("arbitrary", "arbitrary")),
        name="compact",
    )(cum, pos5, aff5)


FFN_ROWS = 256


FFN_COLS = 256


def _ffn_kernel(idx_ref, rows_in, g_ref, gt_ref, wg_ref, wu_ref, wd_ref, rows_hbm,
                buf0, buf1, wgb, wub, wdb, gsem, ssem, *, seq, cap, n_exp, bsz, d):
    del rows_in
    e = pl.program_id(0)
    bufs = (buf0, buf1)
    n_chunk = cap // FFN_ROWS
    f = wgb.shape[1]
    acc_cols = pl.ds(d, d)

    def hbm_row(ee, b, s):
        return b * seq + idx_ref[(b * n_exp + ee) * cap + s]

    def gather_start(ee, b, s):
        pltpu.make_async_copy(rows_hbm.at[pl.ds(hbm_row(ee, b, s), 1)], bufs[b].at[pl.ds(s, 1)],
                              gsem.at[b]).start()

    def gather_wait(b):
        pltpu.make_async_copy(rows_hbm.at[pl.ds(0, cap)], bufs[b], gsem.at[b]).wait()

    def scatter_start(ee, b, s):
        pltpu.make_async_copy(bufs[b].at[pl.ds(s, 1), acc_cols],
                              rows_hbm.at[pl.ds(hbm_row(ee, b, s), 1), acc_cols], ssem.at[b]).start()

    def scatter_wait(b):
        pltpu.make_async_copy(bufs[b].at[:, acc_cols], rows_hbm.at[pl.ds(0, cap), acc_cols],
                              ssem.at[b]).wait()

    @pl.when(e == 0)
    def _():
        def first(s, carry):
            gather_start(0, 0, s)
            return carry
        lax.fori_loop(0, cap, first, 0)

    @pl.when(e > 0)
    def _():
        scatter_wait(bsz - 1)

    wgb[...] = wg_ref[...].astype(BF16)
    wub[...] = wu_ref[...].astype(BF16)
    wdb[...] = wd_ref[...].astype(BF16)

    for b in range(bsz):
        buf = bufs[b]
        if b > 0:
            scatter_wait(b - 1)
        gather_wait(b)
        nb = (b + 1) % bsz
        ne = e if b + 1 < bsz else jnp.minimum(e + 1, n_exp - 1)
        gt = gt_ref[b]
        for ci in range(n_chunk):
            r0 = ci * FFN_ROWS
            rows = slice(r0, r0 + FFN_ROWS)
            todo = [functools.partial(gather_start, ne, nb, s) for s in range(r0, r0 + FFN_ROWS)]
            if ci > 0:
                todo += [functools.partial(scatter_start, e, b, s) for s in range(r0 - FFN_ROWS, r0)]
            n_piece = 3 * (f // FFN_COLS)
            per_piece = -(-len(todo) // n_piece)

            def issue_some():
                for _ in range(min(per_piece, len(todo))):
                    todo.pop(0)()

            xc = buf[rows, 0:d].astype(BF16)
            g_col = g_ref[b, rows, :]
            hm = []
            for nj in range(f // FFN_COLS):
                cols = slice(nj * FFN_COLS, (nj + 1) * FFN_COLS)
                a = jnp.dot(xc, wgb[:, cols], preferred_element_type=F32)
                issue_some()
                u = jnp.dot(xc, wub[:, cols], preferred_element_type=F32)
                issue_some()
                hm.append((a * jax.nn.sigmoid(a) * u).astype(BF16))
            hm = jnp.concatenate(hm, axis=1)
            for nj in range(d // FFN_COLS):
                cols = slice(nj * FFN_COLS, (nj + 1) * FFN_COLS)
                acols = slice(d + nj * FFN_COLS, d + (nj + 1) * FFN_COLS)
                y = jnp.dot(hm, wdb[:, cols], preferred_element_type=F32) * g_col
                buf[rows, acols] = buf[rows, acols] + gt[:, cols] * y
                issue_some()
            assert not todo
        for s in range(cap - FFN_ROWS, cap):
            scatter_start(e, b, s)

    @pl.when(e == n_exp - 1)
    def _():
        scatter_wait(bsz - 1)
        gather_wait(0)


def _ffn_call(idx, rows, gate, gt2, w_gate, w_up, w_down, cap):
    bsz, seq, d2 = rows.shape
    d = d2 // 2
    n_exp, _, f = w_gate.shape
    assert bsz == 2, "ffn double-buffers exactly two samples per expert"
    kern = functools.partial(_ffn_kernel, seq=seq, cap=cap, n_exp=n_exp, bsz=bsz, d=d)
    any_spec = pl.BlockSpec(memory_space=pl.ANY)
    out = pl.pallas_call(
        kern,
        grid_spec=pltpu.PrefetchScalarGridSpec(
            num_scalar_prefetch=1,
            grid=(n_exp,),
            in_specs=[any_spec,
                      pl.BlockSpec((bsz, None, cap, 1), lambda e, idx_ref: (0, e, 0, 0)),
                      pl.BlockSpec((bsz, 1, d), lambda e, idx_ref: (0, 0, 0)),
                      pl.BlockSpec((None, d, f), lambda e, idx_ref: (e, 0, 0)),
                      pl.BlockSpec((None, d, f), lambda e, idx_ref: (e, 0, 0)),
                      pl.BlockSpec((None, f, d), lambda e, idx_ref: (e, 0, 0))],
            out_specs=any_spec,
            scratch_shapes=[pltpu.VMEM((cap, d2), F32), pltpu.VMEM((cap, d2), F32),
                            pltpu.VMEM((d, f), BF16), pltpu.VMEM((d, f), BF16), pltpu.VMEM((f, d), BF16),
                            pltpu.SemaphoreType.DMA((bsz,)), pltpu.SemaphoreType.DMA((bsz,))],
        ),
        out_shape=jax.ShapeDtypeStruct((bsz * seq, d2), F32),
        input_output_aliases={1: 0},
        compiler_params=_cparams(("arbitrary",)),
        name="ffn",
    )(idx, rows.reshape(bsz * seq, d2), gate, gt2, w_gate, w_up, w_down)
    return out.reshape(bsz, seq, d2)


def _final_kernel(x_ref, g_ref, o_ref):
    xf = x_ref[...]
    o_ref[...] = xf * lax.rsqrt(jnp.mean(xf * xf, axis=-1, keepdims=True) + EPS) * g_ref[...]


def _final_call(rows, g, tm=1024):
    bsz, seq, d2 = rows.shape
    d = d2 // 2
    return pl.pallas_call(
        _final_kernel,
        grid=(bsz, seq // tm),
        in_specs=[pl.BlockSpec((None, tm, d), lambda b, i: (b, i, 1)),
                  pl.BlockSpec((1, d), lambda b, i: (0, 0))],
        out_specs=pl.BlockSpec((None, tm, d), lambda b, i: (b, i, 0)),
        out_shape=jax.ShapeDtypeStruct((bsz, seq, d), F32),
        compiler_params=_cparams(("arbitrary", "arbitrary")),
        name="final",
    )(rows, g)


def _rope_tables(seq):
    t = np.arange(seq)
    n_freq = HEAD_DIM // 4
    inv = ROPE_THETA ** (-np.arange(n_freq, dtype=np.float64) / n_freq)
    ang = np.concatenate([(t // GRID_W)[:, None] * inv, (t % GRID_W)[:, None] * inv], axis=1)
    cos_h = np.concatenate([np.cos(ang), np.cos(ang)], axis=1)
    sin_h = np.concatenate([-np.sin(ang), np.sin(ang)], axis=1)
    reps = LANES // HEAD_DIM
    return (jnp.asarray(np.tile(cos_h, (1, reps)), F32), jnp.asarray(np.tile(sin_h, (1, reps)), F32))


def kernel(x, c, ctx, c_ctx, w_ada, b_ada, g_mix, w_in, conv_w, sink, w_out, g_ffn,
           w_router, w_gate, w_up, w_down, g_final):
    bsz, seq, d = x.shape
    n_exp = w_router.shape[-1]
    cap = EC_CAPACITY_FACTOR * seq // n_exp
    assert w_ada.shape[0] == 1, "single trunk layer"
    assert bsz + 1 <= SUBLANES and seq % (SUBLANES * LANES) == 0 and cap % SLOT_BLOCK == 0

    cin = jnp.zeros((SUBLANES, d), F32).at[:bsz].set(c).at[bsz].set(c_ctx)
    mod = _mod_call(cin, w_ada[0], b_ada[0][None, :])
    sh1, sc1, gt1, sh2, sc2, gt2 = [m[:bsz, None, :] for m in jnp.split(mod, 6, axis=-1)]
    csh1 = mod[bsz:bsz + 1, 0:d]
    csc1 = mod[bsz:bsz + 1, d:2 * d]

    w_in_b = w_in[0].astype(BF16)
    g_mix_r = g_mix[0][None, :]
    kv0 = ATT_WIDTH
    k_c, v_c = _ctx_call(ctx, g_mix_r, csh1, csc1, w_in_b[:, kv0:kv0 + 2 * KV_WIDTH])

    cos_t, sin_t = _rope_tables(seq)
    q, k, v, bg, u = _in_call(x, g_mix_r, sh1, sc1, w_in_b, cos_t, sin_t)
    att = _attn_call(sink[0], q, k, v, k_c, v_c)
    rows, aff = _out_call(att, u, bg, conv_w[0], w_out[0].astype(BF16), x, gt1,
                          g_ffn[0][None, :], sh2, sc2, w_router[0].T)

    n_chunk = seq // LANES
    aff4 = aff.reshape(bsz, n_exp, n_chunk, LANES)
    pos, carry = _sel_call(aff4, cap)
    n_tchunk = n_chunk // SUBLANES
    cum = jnp.concatenate([carry[:, :, ::SUBLANES, 0], jnp.full((bsz, n_exp, 1), cap, F32)], axis=-1)
    cum = cum.astype(jnp.int32).reshape(-1)
    comp = _compact_call(cum, pos.reshape(bsz, n_exp, n_tchunk, SUBLANES, LANES),
                         aff4.reshape(bsz, n_exp, n_tchunk, SUBLANES, LANES), cap)
    idx = (comp[:, :, :, 0, :] * LANES + comp[:, :, :, 1, :]).astype(jnp.int32).reshape(-1)
    gate = ((comp[:, :, :, 2, :] + comp[:, :, :, 3, :]) + comp[:, :, :, 4, :]).reshape(bsz, n_exp, cap, 1)

    rows = _ffn_call(idx, rows, gate, gt2, w_gate[0], w_up[0], w_down[0], cap)
    return _final_call(rows, g_final[None, :])
```

```python
import functools

import numpy as np
import jax
import jax.numpy as jnp
from jax import lax
from jax.experimental import pallas as pl
from jax.experimental.pallas import tpu as pltpu

GRID_W = 64
N_HEADS = 8
N_KV_HEADS = 2
HEAD_DIM = 64
GQA_GROUP = N_HEADS // N_KV_HEADS
ATT_WIDTH = N_HEADS * HEAD_DIM
KV_WIDTH = N_KV_HEADS * HEAD_DIM
WINDOW = 128
BLOCK = 128
ROPE_THETA = 10000.0
N_EXPERTS = 16
EC_CAPACITY_FACTOR = 2
EPS = 1e-6

LANES = 128
SUBLANES = 8
MXU_COLS = 256
VMEM_LIMIT = 56 * 1024 * 1024

F32 = jnp.float32
BF16 = jnp.bfloat16
NEG = -1e30
LOG2E = 1.4426950408889634


def _cparams(sem):
    return pltpu.CompilerParams(dimension_semantics=sem, vmem_limit_bytes=VMEM_LIMIT)


def _dot(a, b):
    return jnp.dot(a, b, precision=lax.Precision.DEFAULT, preferred_element_type=F32)


def _dot_nt(a, b):
    return lax.dot_general(a, b, (((1,), (1,)), ((), ())), precision=lax.Precision.DEFAULT,
                           preferred_element_type=F32)


def _norm_mod(xf, g, shift, scale):
    r = xf * lax.rsqrt(jnp.mean(xf * xf, axis=-1, keepdims=True) + EPS)
    return (r * g) * (1.0 + scale) + shift


def _mod_kernel(c_ref, w_ref, b_ref, o_ref):
    c = c_ref[...]
    s = c * jax.nn.sigmoid(c)
    o_ref[...] = jnp.dot(s, w_ref[...], precision=lax.Precision.HIGHEST,
                         preferred_element_type=F32) + b_ref[...]


def _mod_call(cin, w, bias, tn=1024):
    d, n = w.shape
    return pl.pallas_call(
        _mod_kernel,
        grid=(n // tn,),
        in_specs=[pl.BlockSpec((SUBLANES, d), lambda j: (0, 0)),
                  pl.BlockSpec((d, tn), lambda j: (0, j)),
                  pl.BlockSpec((1, tn), lambda j: (0, j))],
        out_specs=pl.BlockSpec((SUBLANES, tn), lambda j: (0, j)),
        out_shape=jax.ShapeDtypeStruct((SUBLANES, n), F32),
        compiler_params=_cparams(("arbitrary",)),
        name="mod",
    )(cin, w, bias)


def _ctx_kernel(x_ref, g_ref, sh_ref, sc_ref, w_ref, k_ref, v_ref):
    h = _norm_mod(x_ref[...], g_ref[...], sh_ref[...], sc_ref[...])
    kv = _dot(h, w_ref[...])
    k_ref[...] = kv[:, :KV_WIDTH]
    v_ref[...] = kv[:, KV_WIDTH:]


def _ctx_call(ctx, g, sh, sc, w_kv):
    bsz, n, d = ctx.shape
    vec = pl.BlockSpec((1, d), lambda b: (0, 0))
    kv_spec = pl.BlockSpec((None, n, KV_WIDTH), lambda b: (b, 0, 0))
    return pl.pallas_call(
        _ctx_kernel,
        grid=(bsz,),
        in_specs=[pl.BlockSpec((None, n, d), lambda b: (b, 0, 0)), vec, vec, vec,
                  pl.BlockSpec((d, 2 * KV_WIDTH), lambda b: (0, 0))],
        out_specs=[kv_spec, kv_spec],
        out_shape=[jax.ShapeDtypeStruct((bsz, n, KV_WIDTH), F32)] * 2,
        compiler_params=_cparams(("arbitrary",)),
        name="ctx_kv",
    )(ctx, g, sh, sc, w_kv)


def _rope(t, cos, sin_signed):
    lane = lax.broadcasted_iota(jnp.int32, t.shape, 1)
    first = (lane % HEAD_DIM) < (HEAD_DIM // 2)
    swapped = jnp.where(first, pltpu.roll(t, LANES - HEAD_DIM // 2, 1), pltpu.roll(t, HEAD_DIM // 2, 1))
    return t * cos + swapped * sin_signed


def _in_kernel(x_ref, g_ref, sh_ref, sc_ref, w_ref, cos_ref, sin_ref,
               q_ref, k_ref, v_ref, bg_ref, u_ref, *, conv_width):
    h = _norm_mod(x_ref[...], g_ref[...], sh_ref[...], sc_ref[...])
    cos = cos_ref[...]
    sin = sin_ref[...]
    scale = HEAD_DIM ** -0.5 * LOG2E
    for j in range(ATT_WIDTH // MXU_COLS):
        qj = _dot(h, w_ref[:, j * MXU_COLS:(j + 1) * MXU_COLS])
        for jj in range(MXU_COLS // LANES):
            c0 = j * MXU_COLS + jj * LANES
            q_ref[:, c0:c0 + LANES] = _rope(qj[:, jj * LANES:(jj + 1) * LANES], cos, sin) * scale
    o = ATT_WIDTH
    kv = _dot(h, w_ref[:, o:o + 2 * KV_WIDTH])
    k_ref[...] = _rope(kv[:, :KV_WIDTH], cos, sin)
    v_ref[...] = kv[:, KV_WIDTH:]
    o += 2 * KV_WIDTH
    bg_ref[...] = _dot(h, w_ref[:, o:o + conv_width])
    o += conv_width
    cg = _dot(h, w_ref[:, o:o + conv_width])
    o += conv_width
    hv = _dot(h, w_ref[:, o:o + conv_width])
    u_ref[...] = cg * hv


def _in_call(x, g, sh, sc, w_in, cos_t, sin_t, tm=1024):
    bsz, seq, d = x.shape
    n_in = w_in.shape[1]
    cw = d - ATT_WIDTH
    vec = pl.BlockSpec((1, d), lambda b, i: (0, 0))
    bvec = pl.BlockSpec((None, 1, d), lambda b, i: (b, 0, 0))
    tab = pl.BlockSpec((tm, LANES), lambda b, i: (i, 0))

    def rows(width):
        return pl.BlockSpec((None, tm, width), lambda b, i: (b, i, 0))

    return pl.pallas_call(
        functools.partial(_in_kernel, conv_width=cw),
        grid=(bsz, seq // tm),
        in_specs=[rows(d), vec, bvec, bvec, pl.BlockSpec((d, n_in), lambda b, i: (0, 0)), tab, tab],
        out_specs=[rows(ATT_WIDTH), rows(KV_WIDTH), rows(KV_WIDTH), rows(cw), rows(cw)],
        out_shape=[jax.ShapeDtypeStruct((bsz, seq, ATT_WIDTH), F32),
                   jax.ShapeDtypeStruct((bsz, seq, KV_WIDTH), F32),
                   jax.ShapeDtypeStruct((bsz, seq, KV_WIDTH), F32),
                   jax.ShapeDtypeStruct((bsz, seq, cw), F32),
                   jax.ShapeDtypeStruct((bsz, seq, cw), F32)],
        compiler_params=_cparams(("arbitrary", "arbitrary")),
        name="in_proj",
    )(x, g, sh, sc, w_in, cos_t, sin_t)


ATTN_TILE = 32


def _attn_kernel(sink_ref, q_ref, k_ref, v_ref, kc_ref, vc_ref, o_ref,
                 s_scr, p_scr, bias_scr, m_scr, *, seq, tq):
    i = pl.program_id(1)
    nblk = tq // BLOCK
    span = 3 * BLOCK
    width = s_scr.shape[1]
    rows_all = GQA_GROUP * BLOCK
    tiles_per_head = BLOCK // ATTN_TILE
    kc = kc_ref[...]
    vc = vc_ref[...]
    for blk in range(nblk):
        n = i * nblk + blk
        ws = pl.multiple_of(jnp.clip((n - 1) * BLOCK, 0, seq - span), BLOCK)
        kw = k_ref[pl.ds(ws, span), :]
        vw = v_ref[pl.ds(ws, span), :]
        qb = q_ref[blk * BLOCK:(blk + 1) * BLOCK, :]
        qpos = n * BLOCK + lax.broadcasted_iota(jnp.int32, (BLOCK, span), 0)
        kpos = ws + lax.broadcasted_iota(jnp.int32, (BLOCK, span), 1)
        bias_scr[...] = jnp.where(jnp.abs(kpos - qpos) <= WINDOW, 0.0, NEG)
        outs = []
        for kh in range(N_KV_HEADS):
            heads = [kh * GQA_GROUP + g for g in range(GQA_GROUP)]
            qs = jnp.concatenate([qb[:, h * HEAD_DIM:(h + 1) * HEAD_DIM] for h in heads], axis=0)
            lo, hi = kh * HEAD_DIM, (kh + 1) * HEAD_DIM
            s_scr[:, 0:span] = _dot_nt(qs, kw[:, lo:hi])
            s_scr[:, span:width] = _dot_nt(qs, kc[:, lo:hi])

            for t in range(rows_all // ATTN_TILE):
                r0 = t * ATTN_TILE
                b0 = (t % tiles_per_head) * ATTN_TILE
                sk = sink_ref[kh * GQA_GROUP + t // tiles_per_head] * LOG2E
                s_loc = s_scr[pl.ds(r0, ATTN_TILE), 0:span] + bias_scr[pl.ds(b0, ATTN_TILE), :]
                s_ctx = s_scr[pl.ds(r0, ATTN_TILE), span:width]
                m = jnp.maximum(jnp.maximum(jnp.max(s_loc, axis=1, keepdims=True),
                                            jnp.max(s_ctx, axis=1, keepdims=True)), sk)
                m_scr[pl.ds(r0, ATTN_TILE), :] = m
                p_scr[pl.ds(r0, ATTN_TILE), 0:span] = jnp.exp2(s_loc - m)
                p_scr[pl.ds(r0, ATTN_TILE), span:width] = jnp.exp2(s_ctx - m)
            own = (lax.broadcasted_iota(jnp.int32, (1, KV_WIDTH), 1) // HEAD_DIM) == kh
            o_ext = (_dot(p_scr[:, 0:span], jnp.where(own, vw, 1.0))
                     + _dot(p_scr[:, span:width], jnp.where(own, vc, 1.0)))
            other = (1 - kh) * HEAD_DIM
            sk_col = jnp.concatenate([jnp.full((BLOCK, 1), sink_ref[h] * LOG2E, F32) for h in heads], axis=0)
            den = o_ext[:, other:other + 1] + jnp.exp2(sk_col - m_scr[...])
            o = o_ext[:, lo:hi] / den
            outs.extend(o[g * BLOCK:(g + 1) * BLOCK] for g in range(GQA_GROUP))
        o_ref[blk * BLOCK:(blk + 1) * BLOCK, :] = jnp.concatenate(outs, axis=1)


def _attn_call(sink, q, k, v, kc, vc, tq=512):
    bsz, seq, _ = q.shape
    n_ctx = kc.shape[1]
    span = 3 * BLOCK
    rows_all = GQA_GROUP * BLOCK
    full_kv = pl.BlockSpec((None, seq, KV_WIDTH), lambda b, i: (b, 0, 0))
    ctx_kv = pl.BlockSpec((None, n_ctx, KV_WIDTH), lambda b, i: (b, 0, 0))
    qspec = pl.BlockSpec((None, tq, ATT_WIDTH), lambda b, i: (b, i, 0))
    return pl.pallas_call(
        functools.partial(_attn_kernel, seq=seq, tq=tq),
        grid=(bsz, seq // tq),
        in_specs=[pl.BlockSpec(memory_space=pltpu.SMEM), qspec, full_kv, full_kv, ctx_kv, ctx_kv],
        out_specs=qspec,
        out_shape=jax.ShapeDtypeStruct((bsz, seq, ATT_WIDTH), F32),
        scratch_shapes=[pltpu.VMEM((rows_all, span + n_ctx), F32),
                        pltpu.VMEM((rows_all, span + n_ctx), F32),
                        pltpu.VMEM((BLOCK, span), F32),
                        pltpu.VMEM((rows_all, 1), F32)],
        compiler_params=_cparams(("arbitrary", "arbitrary")),
        name="attn",
    )(sink, q, k, v, kc, vc)


def _out_kernel(att_ref, u_ref, up_ref, un_ref, bg_ref, cw_ref, wo_ref, x_ref, gt_ref,
                g_ref, sh_ref, sc_ref, wr_ref, rows_ref, aff_ref, *, tm):
    i = pl.program_id(1)
    last = pl.num_programs(1) - 1
    u = u_ref[...]
    row = lax.broadcasted_iota(jnp.int32, u.shape, 0)
    prev_row = jnp.where(i > 0, up_ref[SUBLANES - 1:SUBLANES, :], 0.0)
    next_row = jnp.where(i < last, un_ref[0:1, :], 0.0)
    u_prev = jnp.where(row == 0, prev_row, pltpu.roll(u, 1, 0))
    u_next = jnp.where(row == tm - 1, next_row, pltpu.roll(u, tm - 1, 0))
    conv = bg_ref[...] * (cw_ref[0:1, :] * u_prev + cw_ref[1:2, :] * u + cw_ref[2:3, :] * u_next)
    y = _dot(att_ref[...], wo_ref[0:ATT_WIDTH, :]) + _dot(conv, wo_ref[ATT_WIDTH:, :])
    x1 = x_ref[...] + gt_ref[...] * y
    d = x1.shape[-1]
    h2 = _norm_mod(x1, g_ref[...], sh_ref[...], sc_ref[...])
    rows_ref[:, 0:d] = x1
    rows_ref[:, d:2 * d] = h2
    logits = lax.dot_general(wr_ref[...], h2, (((1,), (1,)), ((), ())),
                             precision=lax.Precision.HIGHEST, preferred_element_type=F32)
    mx = jnp.max(logits, axis=0, keepdims=True)
    ex = jnp.exp(logits - mx)
    aff_ref[...] = ex / jnp.sum(ex, axis=0, keepdims=True)


def _out_call(att, u, bg, conv_w, w_out, x, gt1, g_ffn, sh2, sc2, w_router_t, tm=1024):
    bsz, seq, d = x.shape
    cw = u.shape[-1]
    n_exp = w_router_t.shape[0]
    nh = seq // SUBLANES

    def rows(width):
        return pl.BlockSpec((None, tm, width), lambda b, i: (b, i, 0))

    halo_prev = pl.BlockSpec((None, SUBLANES, cw),
                             lambda b, i: (b, jnp.maximum(i * (tm // SUBLANES) - 1, 0), 0))
    halo_next = pl.BlockSpec((None, SUBLANES, cw),
                             lambda b, i: (b, jnp.minimum((i + 1) * (tm // SUBLANES), nh - 1), 0))
    vec = pl.BlockSpec((1, d), lambda b, i: (0, 0))
    bvec = pl.BlockSpec((None, 1, d), lambda b, i: (b, 0, 0))
    return pl.pallas_call(
        functools.partial(_out_kernel, tm=tm),
        grid=(bsz, seq // tm),
        in_specs=[rows(ATT_WIDTH), rows(cw), halo_prev, halo_next, rows(cw),
                  pl.BlockSpec((3, cw), lambda b, i: (0, 0)),
                  pl.BlockSpec((d, d), lambda b, i: (0, 0)),
                  rows(d), bvec, vec, bvec, bvec,
                  pl.BlockSpec((n_exp, d), lambda b, i: (0, 0))],
        out_specs=[rows(2 * d), pl.BlockSpec((None, n_exp, tm), lambda b, i: (b, 0, i))],
        out_shape=[jax.ShapeDtypeStruct((bsz, seq, 2 * d), F32),
                   jax.ShapeDtypeStruct((bsz, n_exp, seq), F32)],
        compiler_params=_cparams(("arbitrary", "arbitrary")),
        name="out_proj",
    )(att, u, u, u, bg, conv_w, w_out, x, gt1, g_ffn, sh2, sc2, w_router_t)


def _prefix_rows(m, tri, ones, low):
    n_exp, n_chunk, _ = m.shape
    mb = m.astype(BF16).reshape(n_exp * n_chunk, LANES)
    within = jnp.dot(mb, tri, preferred_element_type=F32).reshape(n_exp, n_chunk, LANES)
    tot = jnp.dot(mb, ones, preferred_element_type=F32).astype(BF16).reshape(n_exp, n_chunk, LANES)
    carry = jnp.stack([jnp.dot(low, tot[e], preferred_element_type=F32) for e in range(n_exp)], axis=0)
    return carry + within, carry


def _sel_kernel(aff_ref, pos_ref, carry_ref, *, cap):
    a = aff_ref[...]
    n_exp, n_chunk, _ = a.shape
    bits = lax.bitcast_convert_type(a, jnp.int32)

    def count(pred):
        c = jnp.sum(pred.astype(F32), axis=2, keepdims=True)
        return jnp.sum(c, axis=1, keepdims=True)

    def body(it, thr):
        cand = thr | jnp.left_shift(jnp.int32(1), 30 - it)
        return jnp.where(count(bits >= cand) >= cap, cand, thr)

    thr = lax.fori_loop(0, 31, body, jnp.zeros((n_exp, 1, 1), jnp.int32))
    gt = bits > thr
    eq = bits == thr
    need = cap - count(gt)

    r = lax.broadcasted_iota(jnp.int32, (LANES, LANES), 0)
    c = lax.broadcasted_iota(jnp.int32, (LANES, LANES), 1)
    tri = (r < c).astype(BF16)
    ones = jnp.ones((LANES, LANES), BF16)
    rr = lax.broadcasted_iota(jnp.int32, (n_chunk, n_chunk), 0)
    cc = lax.broadcasted_iota(jnp.int32, (n_chunk, n_chunk), 1)
    low = (cc < rr).astype(BF16)

    eq_rank, _ = _prefix_rows(eq, tri, ones, low)
    sel = gt | (eq & (eq_rank < need))
    pos, carry = _prefix_rows(sel, tri, ones, low)
    pos_ref[...] = jnp.where(sel, pos, -1.0)
    carry_ref[...] = carry


def _sel_call(aff4, cap):
    bsz, n_exp, n_chunk, _ = aff4.shape
    spec = pl.BlockSpec((None, n_exp, n_chunk, LANES), lambda b: (b, 0, 0, 0))
    return pl.pallas_call(
        functools.partial(_sel_kernel, cap=cap),
        grid=(bsz,),
        in_specs=[spec],
        out_specs=[spec, spec],
        out_shape=[jax.ShapeDtypeStruct(aff4.shape, F32)] * 2,
        compiler_params=_cparams(("arbitrary",)),
        name="select",
    )(aff4)


COMPACT_ROWS = 16
SLOT_BLOCK = LANES


def _compact_kernel(cum_ref, pos_ref, aff_ref, out_ref, *, n_exp, n_tchunk):
    b = pl.program_id(0)
    e = pl.program_id(1)
    base = (b * n_exp + e) * (n_tchunk + 1)
    out_ref[...] = jnp.zeros(out_ref.shape, F32)
    nt = (((1,), (1,)), ((), ()))
    rowi = lax.broadcasted_iota(jnp.int32, (COMPACT_ROWS, LANES), 0)
    lane = lax.broadcasted_iota(jnp.int32, (COMPACT_ROWS, LANES), 1).astype(F32)
    slot_b = lax.broadcasted_iota(jnp.int32, (SLOT_BLOCK, LANES), 0).astype(F32).astype(BF16)
    one_b = jnp.ones((), BF16)
    zero_b = jnp.zeros((), BF16)

    n_sblk = out_ref.shape[0] - 1

    def chunk_vals(c):
        a = aff_ref[c]
        g_hi = a.astype(BF16).astype(F32)
        r1 = a - g_hi
        g_mid = r1.astype(BF16).astype(F32)
        g_lo = r1 - g_mid
        pieces = []
        for r in range(SUBLANES):
            def bc(t):
                return jnp.broadcast_to(t[r:r + 1, :], (COMPACT_ROWS, LANES))

            piece = jnp.where(rowi == 0, float(c * SUBLANES + r),
                              jnp.where(rowi == 1, lane,
                                        jnp.where(rowi == 2, bc(g_hi),
                                                  jnp.where(rowi == 3, bc(g_mid),
                                                            jnp.where(rowi == 4, bc(g_lo), 0.0)))))
            pieces.append(piece)
        return jnp.concatenate(pieces, axis=1).astype(BF16)

    def visit(j, p, vals):
        rel = p - jnp.asarray(j * SLOT_BLOCK).astype(F32)
        rel = jnp.where((rel >= 0.0) & (rel < SLOT_BLOCK), rel, -1.0)
        pieces_oh = []
        for r in range(SUBLANES):
            row16 = jnp.broadcast_to(rel[r:r + 1, :], (2 * SUBLANES, LANES)).astype(BF16)
            blk = jnp.concatenate([row16] * (SLOT_BLOCK // (2 * SUBLANES)), axis=0)
            pieces_oh.append(jnp.where(blk == slot_b, one_b, zero_b))
        oh = jnp.concatenate(pieces_oh, axis=1)
        out_ref[j] += lax.dot_general(vals, oh, nt, preferred_element_type=F32)

    def first_block(c):
        return jnp.minimum(cum_ref[base + c] // SLOT_BLOCK, n_sblk - 1)

    for c in range(n_tchunk):
        p = pos_ref[c]
        vals = chunk_vals(c)
        j0 = first_block(c)
        visit(j0, p, vals)
        visit(j0 + 1, p, vals)

    for c in range(n_tchunk):
        lo = cum_ref[base + c]
        hi = cum_ref[base + c + 1]
        j_end = jnp.where(hi > lo, (hi - 1) // SLOT_BLOCK + 1, 0)

        def more(j, carry, c=c):
            visit(j, pos_ref[c], chunk_vals(c))
            return carry

        lax.fori_loop(first_block(c) + 2, j_end, more, 0)


def _compact_call(cum, pos5, aff5, cap):
    bsz, n_exp, n_tchunk, _, _ = pos5.shape
    n_sblk = cap // SLOT_BLOCK
    spec = pl.BlockSpec((None, None, n_tchunk, SUBLANES, LANES), lambda b, e, cum_ref: (b, e, 0, 0, 0))
    return pl.pallas_call(
        functools.partial(_compact_kernel, n_exp=n_exp, n_tchunk=n_tchunk),
        grid_spec=pltpu.PrefetchScalarGridSpec(
            num_scalar_prefetch=1,
            grid=(bsz, n_exp),
            in_specs=[spec, spec],
            out_specs=pl.BlockSpec((None, None, n_sblk + 1, COMPACT_ROWS, LANES),
                                   lambda b, e, cum_ref: (b, e, 0, 0, 0)),
        ),
        out_shape=jax.ShapeDtypeStruct((bsz, n_exp, n_sblk + 1, COMPACT_ROWS, LANES), F32),
        compiler_params=_cparams(("arbitrary", "arbitrary")),
        name="compact",
    )(cum, pos5, aff5)[:, :, :n_sblk]


FFN_ROWS = 256


FFN_COLS = 256


def _ffn_kernel(idx_ref, rows_in, g_ref, gt_ref, wg_ref, wu_ref, wd_ref, rows_hbm,
                buf0, buf1, wgb, wub, wdb, gsem, ssem, *, seq, cap, n_exp, bsz, d):
    del rows_in
    e = pl.program_id(0)
    bufs = (buf0, buf1)
    n_chunk = cap // FFN_ROWS
    f = wgb.shape[1]
    acc_cols = pl.ds(0, d)

    def hbm_row(ee, b, s):
        return idx_ref[(b * n_exp + ee) * cap + s]

    def dma_thread(s):
        return s % 2 if isinstance(s, int) else 0

    def gather_start(ee, b, s):
        pltpu.make_async_copy(rows_hbm.at[pl.ds(hbm_row(ee, b, s), 1)], bufs[b].at[pl.ds(s, 1)],
                              gsem.at[b]).start(priority=dma_thread(s))

    def gather_wait(b):
        pltpu.make_async_copy(rows_hbm.at[pl.ds(0, cap)], bufs[b], gsem.at[b]).wait()

    def scatter_start(ee, b, s):
        pltpu.make_async_copy(bufs[b].at[pl.ds(s, 1), acc_cols],
                              rows_hbm.at[pl.ds(hbm_row(ee, b, s), 1), acc_cols],
                              ssem.at[b]).start(priority=dma_thread(s))

    def scatter_wait(b):
        pltpu.make_async_copy(bufs[b].at[:, acc_cols], rows_hbm.at[pl.ds(0, cap), acc_cols],
                              ssem.at[b]).wait()

    def compute_chunk(b, ci, todo):
        buf = bufs[b]
        rows = slice(ci * FFN_ROWS, (ci + 1) * FFN_ROWS)
        n_piece = 3 * (f // FFN_COLS)
        per_piece = -(-len(todo) // n_piece)

        def issue_some():
            for _ in range(min(per_piece, len(todo))):
                todo.pop(0)()

        xc = buf[rows, d:2 * d].astype(BF16)
        g_col = g_ref[b, rows, :]
        gt = gt_ref[b]
        hm = []
        for nj in range(f // FFN_COLS):
            cols = slice(nj * FFN_COLS, (nj + 1) * FFN_COLS)
            a = jnp.dot(xc, wgb[:, cols], preferred_element_type=F32)
            issue_some()
            u = jnp.dot(xc, wub[:, cols], preferred_element_type=F32)
            issue_some()
            hm.append((a * jax.nn.sigmoid(a) * u).astype(BF16))
        hm = jnp.concatenate(hm, axis=1)
        for nj in range(d // FFN_COLS):
            cols = slice(nj * FFN_COLS, (nj + 1) * FFN_COLS)
            y = jnp.dot(hm, wdb[:, cols], preferred_element_type=F32) * g_col
            buf[rows, cols] = buf[rows, cols] + gt[:, cols] * y
            issue_some()
        assert not todo

    def chunk_rows(ci):
        return range(ci * FFN_ROWS, (ci + 1) * FFN_ROWS)

    def split(n, parts):
        bounds = [n * k // parts for k in range(parts + 1)]
        return [range(bounds[k], bounds[k + 1]) for k in range(parts)]

    @pl.when(e == 0)
    def _():
        def first(s, carry):
            gather_start(0, 0, s)
            return carry
        lax.fori_loop(0, cap, first, 0)

    @pl.when(e > 0)
    def _():
        scatter_wait(1)

    wgb[...] = wg_ref[...].astype(BF16)
    wub[...] = wu_ref[...].astype(BF16)
    wdb[...] = wd_ref[...].astype(BF16)

    gather_wait(0)
    for ci in range(n_chunk):
        todo = [functools.partial(gather_start, e, 1, s) for s in chunk_rows(ci)]
        if ci > 0:
            todo += [functools.partial(scatter_start, e, 0, s) for s in chunk_rows(ci - 1)]
        compute_chunk(0, ci, todo)

    ne = jnp.minimum(e + 1, n_exp - 1)
    gather_wait(1)
    compute_chunk(1, 0, [functools.partial(scatter_start, e, 0, s) for s in chunk_rows(n_chunk - 1)])
    scatter_wait(0)
    for ci, part in zip(range(1, n_chunk), split(cap, n_chunk - 1)):
        todo = [functools.partial(gather_start, ne, 0, s) for s in part]
        todo += [functools.partial(scatter_start, e, 1, s) for s in chunk_rows(ci - 1)]
        compute_chunk(1, ci, todo)
    for s in chunk_rows(n_chunk - 1):
        scatter_start(e, 1, s)

    @pl.when(e == n_exp - 1)
    def _():
        scatter_wait(1)
        gather_wait(0)


def _ffn_call(idx, rows, gate, gt2, w_gate, w_up, w_down, cap):
    bsz, seq, d2 = rows.shape
    d = d2 // 2
    n_exp, _, f = w_gate.shape
    assert bsz == 2, "ffn double-buffers exactly two samples per expert"
    kern = functools.partial(_ffn_kernel, seq=seq, cap=cap, n_exp=n_exp, bsz=bsz, d=d)
    any_spec = pl.BlockSpec(memory_space=pl.ANY)
    out = pl.pallas_call(
        kern,
        grid_spec=pltpu.PrefetchScalarGridSpec(
            num_scalar_prefetch=1,
            grid=(n_exp,),
            in_specs=[any_spec,
                      pl.BlockSpec((bsz, None, cap, 1), lambda e, idx_ref: (0, e, 0, 0)),
                      pl.BlockSpec((bsz, 1, d), lambda e, idx_ref: (0, 0, 0)),
                      pl.BlockSpec((None, d, f), lambda e, idx_ref: (e, 0, 0)),
                      pl.BlockSpec((None, d, f), lambda e, idx_ref: (e, 0, 0)),
                      pl.BlockSpec((None, f, d), lambda e, idx_ref: (e, 0, 0))],
            out_specs=any_spec,
            scratch_shapes=[pltpu.VMEM((cap, d2), F32), pltpu.VMEM((cap, d2), F32),
                            pltpu.VMEM((d, f), BF16), pltpu.VMEM((d, f), BF16), pltpu.VMEM((f, d), BF16),
                            pltpu.SemaphoreType.DMA((bsz,)), pltpu.SemaphoreType.DMA((bsz,))],
        ),
        out_shape=jax.ShapeDtypeStruct((bsz * seq, d2), F32),
        input_output_aliases={1: 0},
        compiler_params=_cparams(("arbitrary",)),
        name="ffn",
    )(idx, rows.reshape(bsz * seq, d2), gate, gt2, w_gate, w_up, w_down)
    return out.reshape(bsz, seq, d2)


def _final_kernel(x_ref, g_ref, o_ref):
    xf = x_ref[...]
    o_ref[...] = xf * lax.rsqrt(jnp.mean(xf * xf, axis=-1, keepdims=True) + EPS) * g_ref[...]


def _final_call(rows, g, tm=1024):
    bsz, seq, d2 = rows.shape
    d = d2 // 2
    return pl.pallas_call(
        _final_kernel,
        grid=(bsz, seq // tm),
        in_specs=[pl.BlockSpec((None, tm, d), lambda b, i: (b, i, 0)),
                  pl.BlockSpec((1, d), lambda b, i: (0, 0))],
        out_specs=pl.BlockSpec((None, tm, d), lambda b, i: (b, i, 0)),
        out_shape=jax.ShapeDtypeStruct((bsz, seq, d), F32),
        compiler_params=_cparams(("arbitrary", "arbitrary")),
        name="final",
    )(rows, g)


def _rope_tables(seq):
    t = np.arange(seq)
    n_freq = HEAD_DIM // 4
    inv = ROPE_THETA ** (-np.arange(n_freq, dtype=np.float64) / n_freq)
    ang = np.concatenate([(t // GRID_W)[:, None] * inv, (t % GRID_W)[:, None] * inv], axis=1)
    cos_h = np.concatenate([np.cos(ang), np.cos(ang)], axis=1)
    sin_h = np.concatenate([-np.sin(ang), np.sin(ang)], axis=1)
    reps = LANES // HEAD_DIM
    return (jnp.asarray(np.tile(cos_h, (1, reps)), F32), jnp.asarray(np.tile(sin_h, (1, reps)), F32))


def kernel(x, c, ctx, c_ctx, w_ada, b_ada, g_mix, w_in, conv_w, sink, w_out, g_ffn,
           w_router, w_gate, w_up, w_down, g_final):
    bsz, seq, d = x.shape
    n_exp = w_router.shape[-1]
    cap = EC_CAPACITY_FACTOR * seq // n_exp
    assert w_ada.shape[0] == 1, "single trunk layer"
    assert bsz + 1 <= SUBLANES and seq % (SUBLANES * LANES) == 0 and cap % SLOT_BLOCK == 0

    cin = jnp.zeros((SUBLANES, d), F32).at[:bsz].set(c).at[bsz].set(c_ctx)
    mod = _mod_call(cin, w_ada[0], b_ada[0][None, :])
    sh1, sc1, gt1, sh2, sc2, gt2 = [m[:bsz, None, :] for m in jnp.split(mod, 6, axis=-1)]
    csh1 = mod[bsz:bsz + 1, 0:d]
    csc1 = mod[bsz:bsz + 1, d:2 * d]

    w_in0 = w_in[0]
    g_mix_r = g_mix[0][None, :]
    kv0 = ATT_WIDTH
    k_c, v_c = _ctx_call(ctx, g_mix_r, csh1, csc1, w_in0[:, kv0:kv0 + 2 * KV_WIDTH])

    cos_t, sin_t = _rope_tables(seq)
    q, k, v, bg, u = _in_call(x, g_mix_r, sh1, sc1, w_in0, cos_t, sin_t)
    att = _attn_call(sink[0], q, k, v, k_c, v_c)
    rows, aff = _out_call(att, u, bg, conv_w[0], w_out[0], x, gt1,
                          g_ffn[0][None, :], sh2, sc2, w_router[0].T)

    n_chunk = seq // LANES
    aff4 = aff.reshape(bsz, n_exp, n_chunk, LANES)
    pos, carry = _sel_call(aff4, cap)
    n_tchunk = n_chunk // SUBLANES
    cum = jnp.concatenate([carry[:, :, ::SUBLANES, 0], jnp.full((bsz, n_exp, 1), cap, F32)], axis=-1)
    cum = cum.astype(jnp.int32).reshape(-1)
    comp = _compact_call(cum, pos.reshape(bsz, n_exp, n_tchunk, SUBLANES, LANES),
                         aff4.reshape(bsz, n_exp, n_tchunk, SUBLANES, LANES), cap)
    idx = (comp[:, :, :, 0, :] * LANES + comp[:, :, :, 1, :]).astype(jnp.int32)
    idx = (idx + (jnp.arange(bsz, dtype=jnp.int32) * seq)[:, None, None, None]).reshape(-1)
    gate = ((comp[:, :, :, 2, :] + comp[:, :, :, 3, :]) + comp[:, :, :, 4, :]).reshape(bsz, n_exp, cap, 1)

    rows = _ffn_call(idx, rows, gate, gt2, w_gate[0], w_up[0], w_down[0], cap)
    return _final_call(rows, g_final[None, :])
```

```python
import functools

import numpy as np
import jax
import jax.numpy as jnp
from jax import lax
from jax.experimental import pallas as pl
from jax.experimental.pallas import tpu as pltpu

GRID_W = 64
N_HEADS = 8
N_KV_HEADS = 2
HEAD_DIM = 64
GQA_GROUP = N_HEADS // N_KV_HEADS
ATT_WIDTH = N_HEADS * HEAD_DIM
KV_WIDTH = N_KV_HEADS * HEAD_DIM
WINDOW = 128
BLOCK = 128
ROPE_THETA = 10000.0
N_EXPERTS = 16
EC_CAPACITY_FACTOR = 2
EPS = 1e-6

LANES = 128
SUBLANES = 8
MXU_COLS = 256
VMEM_LIMIT = 56 * 1024 * 1024

F32 = jnp.float32
BF16 = jnp.bfloat16
NEG = -1e30
LOG2E = 1.4426950408889634


def _cparams(sem):
    return pltpu.CompilerParams(dimension_semantics=sem, vmem_limit_bytes=VMEM_LIMIT)


def _dot(a, b):
    return jnp.dot(a, b, precision=lax.Precision.DEFAULT, preferred_element_type=F32)


def _dot_nt(a, b):
    return lax.dot_general(a, b, (((1,), (1,)), ((), ())), precision=lax.Precision.DEFAULT,
                           preferred_element_type=F32)


def _norm_mod(xf, g, shift, scale):
    r = xf * lax.rsqrt(jnp.mean(xf * xf, axis=-1, keepdims=True) + EPS)
    return (r * g) * (1.0 + scale) + shift


def _mod_kernel(c_ref, w_ref, b_ref, o_ref):
    c = c_ref[...]
    s = c * jax.nn.sigmoid(c)
    o_ref[...] = jnp.dot(s, w_ref[...], precision=lax.Precision.HIGHEST,
                         preferred_element_type=F32) + b_ref[...]


def _mod_call(cin, w, bias, tn=1024):
    d, n = w.shape
    return pl.pallas_call(
        _mod_kernel,
        grid=(n // tn,),
        in_specs=[pl.BlockSpec((SUBLANES, d), lambda j: (0, 0)),
                  pl.BlockSpec((d, tn), lambda j: (0, j)),
                  pl.BlockSpec((1, tn), lambda j: (0, j))],
        out_specs=pl.BlockSpec((SUBLANES, tn), lambda j: (0, j)),
        out_shape=jax.ShapeDtypeStruct((SUBLANES, n), F32),
        compiler_params=_cparams(("arbitrary",)),
        name="mod",
    )(cin, w, bias)


def _ctx_kernel(x_ref, g_ref, sh_ref, sc_ref, w_ref, k_ref, v_ref):
    h = _norm_mod(x_ref[...], g_ref[...], sh_ref[...], sc_ref[...])
    kv = _dot(h, w_ref[...])
    k_ref[...] = kv[:, :KV_WIDTH]
    v_ref[...] = kv[:, KV_WIDTH:]


def _ctx_call(ctx, g, sh, sc, w_kv):
    bsz, n, d = ctx.shape
    vec = pl.BlockSpec((1, d), lambda b: (0, 0))
    kv_spec = pl.BlockSpec((None, n, KV_WIDTH), lambda b: (b, 0, 0))
    return pl.pallas_call(
        _ctx_kernel,
        grid=(bsz,),
        in_specs=[pl.BlockSpec((None, n, d), lambda b: (b, 0, 0)), vec, vec, vec,
                  pl.BlockSpec((d, 2 * KV_WIDTH), lambda b: (0, 0))],
        out_specs=[kv_spec, kv_spec],
        out_shape=[jax.ShapeDtypeStruct((bsz, n, KV_WIDTH), F32)] * 2,
        compiler_params=_cparams(("arbitrary",)),
        name="ctx_kv",
    )(ctx, g, sh, sc, w_kv)


def _rope(t, cos, sin_signed):
    lane = lax.broadcasted_iota(jnp.int32, t.shape, 1)
    first = (lane % HEAD_DIM) < (HEAD_DIM // 2)
    swapped = jnp.where(first, pltpu.roll(t, LANES - HEAD_DIM // 2, 1), pltpu.roll(t, HEAD_DIM // 2, 1))
    return t * cos + swapped * sin_signed


def _in_kernel(x_ref, g_ref, sh_ref, sc_ref, w_ref, cos_ref, sin_ref,
               q_ref, k_ref, v_ref, bg_ref, u_ref, *, conv_width):
    h = _norm_mod(x_ref[...], g_ref[...], sh_ref[...], sc_ref[...])
    cos = cos_ref[...]
    sin = sin_ref[...]
    scale = HEAD_DIM ** -0.5 * LOG2E
    for j in range(ATT_WIDTH // MXU_COLS):
        qj = _dot(h, w_ref[:, j * MXU_COLS:(j + 1) * MXU_COLS])
        for jj in range(MXU_COLS // LANES):
            c0 = j * MXU_COLS + jj * LANES
            q_ref[:, c0:c0 + LANES] = _rope(qj[:, jj * LANES:(jj + 1) * LANES], cos, sin) * scale
    o = ATT_WIDTH
    kv = _dot(h, w_ref[:, o:o + 2 * KV_WIDTH])
    k_ref[...] = _rope(kv[:, :KV_WIDTH], cos, sin)
    v_ref[...] = kv[:, KV_WIDTH:]
    o += 2 * KV_WIDTH
    bg_ref[...] = _dot(h, w_ref[:, o:o + conv_width])
    o += conv_width
    cg = _dot(h, w_ref[:, o:o + conv_width])
    o += conv_width
    hv = _dot(h, w_ref[:, o:o + conv_width])
    u_ref[...] = cg * hv


def _in_call(x, g, sh, sc, w_in, cos_t, sin_t, tm=1024):
    bsz, seq, d = x.shape
    n_in = w_in.shape[1]
    cw = d - ATT_WIDTH
    vec = pl.BlockSpec((1, d), lambda b, i: (0, 0))
    bvec = pl.BlockSpec((None, 1, d), lambda b, i: (b, 0, 0))
    tab = pl.BlockSpec((tm, LANES), lambda b, i: (i, 0))

    def rows(width):
        return pl.BlockSpec((None, tm, width), lambda b, i: (b, i, 0))

    return pl.pallas_call(
        functools.partial(_in_kernel, conv_width=cw),
        grid=(bsz, seq // tm),
        in_specs=[rows(d), vec, bvec, bvec, pl.BlockSpec((d, n_in), lambda b, i: (0, 0)), tab, tab],
        out_specs=[rows(ATT_WIDTH), rows(KV_WIDTH), rows(KV_WIDTH), rows(cw), rows(cw)],
        out_shape=[jax.ShapeDtypeStruct((bsz, seq, ATT_WIDTH), F32),
                   jax.ShapeDtypeStruct((bsz, seq, KV_WIDTH), F32),
                   jax.ShapeDtypeStruct((bsz, seq, KV_WIDTH), F32),
                   jax.ShapeDtypeStruct((bsz, seq, cw), F32),
                   jax.ShapeDtypeStruct((bsz, seq, cw), F32)],
        compiler_params=_cparams(("arbitrary", "arbitrary")),
        name="in_proj",
    )(x, g, sh, sc, w_in, cos_t, sin_t)


ATTN_TILE = 32


def _attn_kernel(sink_ref, q_ref, k_ref, v_ref, kc_ref, vc_ref, o_ref,
                 s_scr, p_scr, bias_scr, m_scr, *, seq, tq):
    i = pl.program_id(1)
    nblk = tq // BLOCK
    span = 3 * BLOCK
    width = s_scr.shape[1]
    rows_all = GQA_GROUP * BLOCK
    tiles_per_head = BLOCK // ATTN_TILE
    kc = kc_ref[...]
    vc = vc_ref[...]
    for blk in range(nblk):
        n = i * nblk + blk
        ws = pl.multiple_of(jnp.clip((n - 1) * BLOCK, 0, seq - span), BLOCK)
        kw = k_ref[pl.ds(ws, span), :]
        vw = v_ref[pl.ds(ws, span), :]
        qb = q_ref[blk * BLOCK:(blk + 1) * BLOCK, :]
        qpos = n * BLOCK + lax.broadcasted_iota(jnp.int32, (BLOCK, span), 0)
        kpos = ws + lax.broadcasted_iota(jnp.int32, (BLOCK, span), 1)
        bias_scr[...] = jnp.where(jnp.abs(kpos - qpos) <= WINDOW, 0.0, NEG)
        outs = []
        for kh in range(N_KV_HEADS):
            heads = [kh * GQA_GROUP + g for g in range(GQA_GROUP)]
            qs = jnp.concatenate([qb[:, h * HEAD_DIM:(h + 1) * HEAD_DIM] for h in heads], axis=0)
            lo, hi = kh * HEAD_DIM, (kh + 1) * HEAD_DIM
            s_scr[:, 0:span] = _dot_nt(qs, kw[:, lo:hi])
            s_scr[:, span:width] = _dot_nt(qs, kc[:, lo:hi])

            for t in range(rows_all // ATTN_TILE):
                r0 = t * ATTN_TILE
                b0 = (t % tiles_per_head) * ATTN_TILE
                sk = sink_ref[kh * GQA_GROUP + t // tiles_per_head] * LOG2E
                s_loc = s_scr[pl.ds(r0, ATTN_TILE), 0:span] + bias_scr[pl.ds(b0, ATTN_TILE), :]
                s_ctx = s_scr[pl.ds(r0, ATTN_TILE), span:width]
                m = jnp.maximum(jnp.maximum(jnp.max(s_loc, axis=1, keepdims=True),
                                            jnp.max(s_ctx, axis=1, keepdims=True)), sk)
                m_scr[pl.ds(r0, ATTN_TILE), :] = m
                p_scr[pl.ds(r0, ATTN_TILE), 0:span] = jnp.exp2(s_loc - m)
                p_scr[pl.ds(r0, ATTN_TILE), span:width] = jnp.exp2(s_ctx - m)
            own = (lax.broadcasted_iota(jnp.int32, (1, KV_WIDTH), 1) // HEAD_DIM) == kh
            o_ext = (_dot(p_scr[:, 0:span], jnp.where(own, vw, 1.0))
                     + _dot(p_scr[:, span:width], jnp.where(own, vc, 1.0)))
            other = (1 - kh) * HEAD_DIM
            sk_col = jnp.concatenate([jnp.full((BLOCK, 1), sink_ref[h] * LOG2E, F32) for h in heads], axis=0)
            den = o_ext[:, other:other + 1] + jnp.exp2(sk_col - m_scr[...])
            o = o_ext[:, lo:hi] / den
            outs.extend(o[g * BLOCK:(g + 1) * BLOCK] for g in range(GQA_GROUP))
        o_ref[blk * BLOCK:(blk + 1) * BLOCK, :] = jnp.concatenate(outs, axis=1)


def _attn_call(sink, q, k, v, kc, vc, tq=512):
    bsz, seq, _ = q.shape
    n_ctx = kc.shape[1]
    span = 3 * BLOCK
    rows_all = GQA_GROUP * BLOCK
    full_kv = pl.BlockSpec((None, seq, KV_WIDTH), lambda b, i: (b, 0, 0))
    ctx_kv = pl.BlockSpec((None, n_ctx, KV_WIDTH), lambda b, i: (b, 0, 0))
    qspec = pl.BlockSpec((None, tq, ATT_WIDTH), lambda b, i: (b, i, 0))
    return pl.pallas_call(
        functools.partial(_attn_kernel, seq=seq, tq=tq),
        grid=(bsz, seq // tq),
        in_specs=[pl.BlockSpec(memory_space=pltpu.SMEM), qspec, full_kv, full_kv, ctx_kv, ctx_kv],
        out_specs=qspec,
        out_shape=jax.ShapeDtypeStruct((bsz, seq, ATT_WIDTH), F32),
        scratch_shapes=[pltpu.VMEM((rows_all, span + n_ctx), F32),
                        pltpu.VMEM((rows_all, span + n_ctx), F32),
                        pltpu.VMEM((BLOCK, span), F32),
                        pltpu.VMEM((rows_all, 1), F32)],
        compiler_params=_cparams(("arbitrary", "arbitrary")),
        name="attn",
    )(sink, q, k, v, kc, vc)


def _out_kernel(att_ref, u_ref, up_ref, un_ref, bg_ref, cw_ref, wo_ref, x_ref, gt_ref,
                g_ref, sh_ref, sc_ref, wr_ref, rows_ref, aff_ref, *, tm):
    i = pl.program_id(1)
    last = pl.num_programs(1) - 1
    u = u_ref[...]
    row = lax.broadcasted_iota(jnp.int32, u.shape, 0)
    prev_row = jnp.where(i > 0, up_ref[SUBLANES - 1:SUBLANES, :], 0.0)
    next_row = jnp.where(i < last, un_ref[0:1, :], 0.0)
    u_prev = jnp.where(row == 0, prev_row, pltpu.roll(u, 1, 0))
    u_next = jnp.where(row == tm - 1, next_row, pltpu.roll(u, tm - 1, 0))
    conv = bg_ref[...] * (cw_ref[0:1, :] * u_prev + cw_ref[1:2, :] * u + cw_ref[2:3, :] * u_next)
    y = _dot(att_ref[...], wo_ref[0:ATT_WIDTH, :]) + _dot(conv, wo_ref[ATT_WIDTH:, :])
    x1 = x_ref[...] + gt_ref[...] * y
    d = x1.shape[-1]
    h2 = _norm_mod(x1, g_ref[...], sh_ref[...], sc_ref[...])
    rows_ref[:, 0:d] = x1
    rows_ref[:, d:2 * d] = h2
    logits = lax.dot_general(wr_ref[...], h2, (((1,), (1,)), ((), ())),
                             precision=lax.Precision.HIGHEST, preferred_element_type=F32)
    mx = jnp.max(logits, axis=0, keepdims=True)
    ex = jnp.exp(logits - mx)
    aff_ref[...] = ex / jnp.sum(ex, axis=0, keepdims=True)


def _out_call(att, u, bg, conv_w, w_out, x, gt1, g_ffn, sh2, sc2, w_router_t, tm=1024):
    bsz, seq, d = x.shape
    cw = u.shape[-1]
    n_exp = w_router_t.shape[0]
    nh = seq // SUBLANES

    def rows(width):
        return pl.BlockSpec((None, tm, width), lambda b, i: (b, i, 0))

    halo_prev = pl.BlockSpec((None, SUBLANES, cw),
                             lambda b, i: (b, jnp.maximum(i * (tm // SUBLANES) - 1, 0), 0))
    halo_next = pl.BlockSpec((None, SUBLANES, cw),
                             lambda b, i: (b, jnp.minimum((i + 1) * (tm // SUBLANES), nh - 1), 0))
    vec = pl.BlockSpec((1, d), lambda b, i: (0, 0))
    bvec = pl.BlockSpec((None, 1, d), lambda b, i: (b, 0, 0))
    return pl.pallas_call(
        functools.partial(_out_kernel, tm=tm),
        grid=(bsz, seq // tm),
        in_specs=[rows(ATT_WIDTH), rows(cw), halo_prev, halo_next, rows(cw),
                  pl.BlockSpec((3, cw), lambda b, i: (0, 0)),
                  pl.BlockSpec((d, d), lambda b, i: (0, 0)),
                  rows(d), bvec, vec, bvec, bvec,
                  pl.BlockSpec((n_exp, d), lambda b, i: (0, 0))],
        out_specs=[rows(2 * d), pl.BlockSpec((None, n_exp, tm), lambda b, i: (b, 0, i))],
        out_shape=[jax.ShapeDtypeStruct((bsz, seq, 2 * d), F32),
                   jax.ShapeDtypeStruct((bsz, n_exp, seq), F32)],
        compiler_params=_cparams(("arbitrary", "arbitrary")),
        name="out_proj",
    )(att, u, u, u, bg, conv_w, w_out, x, gt1, g_ffn, sh2, sc2, w_router_t)


def _prefix_rows(m, tri, ones, low):
    n_exp, n_chunk, _ = m.shape
    mb = m.astype(BF16).reshape(n_exp * n_chunk, LANES)
    within = jnp.dot(mb, tri, preferred_element_type=F32).reshape(n_exp, n_chunk, LANES)
    tot = jnp.dot(mb, ones, preferred_element_type=F32).astype(BF16).reshape(n_exp, n_chunk, LANES)
    carry = jnp.stack([jnp.dot(low, tot[e], preferred_element_type=F32) for e in range(n_exp)], axis=0)
    return carry + within, carry


def _sel_kernel(aff_ref, pos_ref, carry_ref, *, cap):
    a = aff_ref[...]
    n_exp, n_chunk, _ = a.shape
    bits = lax.bitcast_convert_type(a, jnp.int32)

    def count(pred):
        c = jnp.sum(pred.astype(F32), axis=2, keepdims=True)
        return jnp.sum(c, axis=1, keepdims=True)

    def body(it, thr):
        cand = thr | jnp.left_shift(jnp.int32(1), 30 - it)
        return jnp.where(count(bits >= cand) >= cap, cand, thr)

    thr = lax.fori_loop(0, 31, body, jnp.zeros((n_exp, 1, 1), jnp.int32))
    gt = bits > thr
    eq = bits == thr
    need = cap - count(gt)

    r = lax.broadcasted_iota(jnp.int32, (LANES, LANES), 0)
    c = lax.broadcasted_iota(jnp.int32, (LANES, LANES), 1)
    tri = (r < c).astype(BF16)
    ones = jnp.ones((LANES, LANES), BF16)
    rr = lax.broadcasted_iota(jnp.int32, (n_chunk, n_chunk), 0)
    cc = lax.broadcasted_iota(jnp.int32, (n_chunk, n_chunk), 1)
    low = (cc < rr).astype(BF16)

    eq_rank, _ = _prefix_rows(eq, tri, ones, low)
    sel = gt | (eq & (eq_rank < need))
    pos, carry = _prefix_rows(sel, tri, ones, low)
    pos_ref[...] = jnp.where(sel, pos, -1.0)
    carry_ref[...] = carry


def _sel_call(aff4, cap):
    bsz, n_exp, n_chunk, _ = aff4.shape
    spec = pl.BlockSpec((None, n_exp, n_chunk, LANES), lambda b: (b, 0, 0, 0))
    return pl.pallas_call(
        functools.partial(_sel_kernel, cap=cap),
        grid=(bsz,),
        in_specs=[spec],
        out_specs=[spec, spec],
        out_shape=[jax.ShapeDtypeStruct(aff4.shape, F32)] * 2,
        compiler_params=_cparams(("arbitrary",)),
        name="select",
    )(aff4)


COMPACT_ROWS = 16
SLOT_BLOCK = LANES


def _compact_kernel(cum_ref, pos_ref, aff_ref, out_ref, *, n_exp, n_tchunk):
    b = pl.program_id(0)
    e = pl.program_id(1)
    base = (b * n_exp + e) * (n_tchunk + 1)
    out_ref[...] = jnp.zeros(out_ref.shape, F32)
    nt = (((1,), (1,)), ((), ()))
    rowi = lax.broadcasted_iota(jnp.int32, (COMPACT_ROWS, LANES), 0)
    lane = lax.broadcasted_iota(jnp.int32, (COMPACT_ROWS, LANES), 1).astype(F32)
    slot_b = lax.broadcasted_iota(jnp.int32, (SLOT_BLOCK, LANES), 0).astype(F32).astype(BF16)
    one_b = jnp.ones((), BF16)
    zero_b = jnp.zeros((), BF16)

    n_sblk = out_ref.shape[0] - 1

    def chunk_vals(c):
        a = aff_ref[c]
        g_hi = a.astype(BF16).astype(F32)
        r1 = a - g_hi
        g_mid = r1.astype(BF16).astype(F32)
        g_lo = r1 - g_mid
        pieces = []
        for r in range(SUBLANES):
            def bc(t):
                return jnp.broadcast_to(t[r:r + 1, :], (COMPACT_ROWS, LANES))

            piece = jnp.where(rowi == 0, float(c * SUBLANES + r),
                              jnp.where(rowi == 1, lane,
                                        jnp.where(rowi == 2, bc(g_hi),
                                                  jnp.where(rowi == 3, bc(g_mid),
                                                            jnp.where(rowi == 4, bc(g_lo), 0.0)))))
            pieces.append(piece)
        return jnp.concatenate(pieces, axis=1).astype(BF16)

    def visit(j, p, vals):
        rel = p - jnp.asarray(j * SLOT_BLOCK).astype(F32)
        rel = jnp.where((rel >= 0.0) & (rel < SLOT_BLOCK), rel, -1.0)
        pieces_oh = []
        for r in range(SUBLANES):
            row16 = jnp.broadcast_to(rel[r:r + 1, :], (2 * SUBLANES, LANES)).astype(BF16)
            blk = jnp.concatenate([row16] * (SLOT_BLOCK // (2 * SUBLANES)), axis=0)
            pieces_oh.append(jnp.where(blk == slot_b, one_b, zero_b))
        oh = jnp.concatenate(pieces_oh, axis=1)
        out_ref[j] += lax.dot_general(vals, oh, nt, preferred_element_type=F32)

    def first_block(c):
        return jnp.minimum(cum_ref[base + c] // SLOT_BLOCK, n_sblk - 1)

    for c in range(n_tchunk):
        p = pos_ref[c]
        vals = chunk_vals(c)
        j0 = first_block(c)
        visit(j0, p, vals)
        visit(j0 + 1, p, vals)

    for c in range(n_tchunk):
        lo = cum_ref[base + c]
        hi = cum_ref[base + c + 1]
        j_end = jnp.where(hi > lo, (hi - 1) // SLOT_BLOCK + 1, 0)

        def more(j, carry, c=c):
            visit(j, pos_ref[c], chunk_vals(c))
            return carry

        lax.fori_loop(first_block(c) + 2, j_end, more, 0)


def _compact_call(cum, pos5, aff5, cap):
    bsz, n_exp, n_tchunk, _, _ = pos5.shape
    n_sblk = cap // SLOT_BLOCK
    spec = pl.BlockSpec((None, None, n_tchunk, SUBLANES, LANES), lambda b, e, cum_ref: (b, e, 0, 0, 0))
    return pl.pallas_call(
        functools.partial(_compact_kernel, n_exp=n_exp, n_tchunk=n_tchunk),
        grid_spec=pltpu.PrefetchScalarGridSpec(
            num_scalar_prefetch=1,
            grid=(bsz, n_exp),
            in_specs=[spec, spec],
            out_specs=pl.BlockSpec((None, None, n_sblk + 1, COMPACT_ROWS, LANES),
                                   lambda b, e, cum_ref: (b, e, 0, 0, 0)),
        ),
        out_shape=jax.ShapeDtypeStruct((bsz, n_exp, n_sblk + 1, COMPACT_ROWS, LANES), F32),
        compiler_params=_cparams(("arbitrary", "arbitrary")),
        name="compact",
    )(cum, pos5, aff5)[:, :, :n_sblk]


FFN_ROWS = 256


FFN_COLS = 256


def _ffn_kernel(idx_ref, rows_in, g_ref, gt_ref, wg_ref, wu_ref, wd_ref, rows_hbm,
                buf0, buf1, gsem, ssem, *, seq, cap, n_exp, bsz, d):
    del rows_in
    e = pl.program_id(0)
    bufs = (buf0, buf1)
    n_chunk = cap // FFN_ROWS
    f = wg_ref.shape[1]
    acc_cols = pl.ds(0, d)

    def hbm_row(ee, b, s):
        return idx_ref[(b * n_exp + ee) * cap + s]

    def dma_thread(s):
        return s % 2 if isinstance(s, int) else 0

    def gather_start(ee, b, s):
        pltpu.make_async_copy(rows_hbm.at[pl.ds(hbm_row(ee, b, s), 1)], bufs[b].at[pl.ds(s, 1)],
                              gsem.at[b]).start(priority=dma_thread(s))

    def gather_wait(b):
        pltpu.make_async_copy(rows_hbm.at[pl.ds(0, cap)], bufs[b], gsem.at[b]).wait()

    def scatter_start(ee, b, s):
        pltpu.make_async_copy(bufs[b].at[pl.ds(s, 1), acc_cols],
                              rows_hbm.at[pl.ds(hbm_row(ee, b, s), 1), acc_cols],
                              ssem.at[b]).start(priority=dma_thread(s))

    def scatter_wait(b):
        pltpu.make_async_copy(bufs[b].at[:, acc_cols], rows_hbm.at[pl.ds(0, cap), acc_cols],
                              ssem.at[b]).wait()

    def compute_chunk(b, ci, todo):
        buf = bufs[b]
        rows = slice(ci * FFN_ROWS, (ci + 1) * FFN_ROWS)
        n_piece = 3 * (f // FFN_COLS)
        per_piece = -(-len(todo) // n_piece)

        def issue_some():
            for _ in range(min(per_piece, len(todo))):
                todo.pop(0)()

        xc = buf[rows, d:2 * d]
        g_col = g_ref[b, rows, :]
        gt = gt_ref[b]
        hm = []
        for nj in range(f // FFN_COLS):
            cols = slice(nj * FFN_COLS, (nj + 1) * FFN_COLS)
            a = _dot(xc, wg_ref[:, cols])
            issue_some()
            u = _dot(xc, wu_ref[:, cols])
            issue_some()
            hm.append(a * jax.nn.sigmoid(a) * u)
        hm = jnp.concatenate(hm, axis=1)
        for nj in range(d // FFN_COLS):
            cols = slice(nj * FFN_COLS, (nj + 1) * FFN_COLS)
            y = _dot(hm, wd_ref[:, cols]) * g_col
            buf[rows, cols] = buf[rows, cols] + gt[:, cols] * y
            issue_some()
        assert not todo

    def chunk_rows(ci):
        return range(ci * FFN_ROWS, (ci + 1) * FFN_ROWS)

    def split(n, parts):
        bounds = [n * k // parts for k in range(parts + 1)]
        return [range(bounds[k], bounds[k + 1]) for k in range(parts)]

    @pl.when(e == 0)
    def _():
        def first(s, carry):
            gather_start(0, 0, s)
            return carry
        lax.fori_loop(0, cap, first, 0)

    @pl.when(e > 0)
    def _():
        scatter_wait(1)

    gather_wait(0)
    for ci in range(n_chunk):
        todo = [functools.partial(gather_start, e, 1, s) for s in chunk_rows(ci)]
        if ci > 0:
            todo += [functools.partial(scatter_start, e, 0, s) for s in chunk_rows(ci - 1)]
        compute_chunk(0, ci, todo)

    ne = jnp.minimum(e + 1, n_exp - 1)
    gather_wait(1)
    compute_chunk(1, 0, [functools.partial(scatter_start, e, 0, s) for s in chunk_rows(n_chunk - 1)])
    scatter_wait(0)
    for ci, part in zip(range(1, n_chunk), split(cap, n_chunk - 1)):
        todo = [functools.partial(gather_start, ne, 0, s) for s in part]
        todo += [functools.partial(scatter_start, e, 1, s) for s in chunk_rows(ci - 1)]
        compute_chunk(1, ci, todo)
    for s in chunk_rows(n_chunk - 1):
        scatter_start(e, 1, s)

    @pl.when(e == n_exp - 1)
    def _():
        scatter_wait(1)
        gather_wait(0)


def _ffn_call(idx, rows, gate, gt2, w_gate, w_up, w_down, cap):
    bsz, seq, d2 = rows.shape
    d = d2 // 2
    n_exp, _, f = w_gate.shape
    assert bsz == 2, "ffn double-buffers exactly two samples per expert"
    kern = functools.partial(_ffn_kernel, seq=seq, cap=cap, n_exp=n_exp, bsz=bsz, d=d)
    any_spec = pl.BlockSpec(memory_space=pl.ANY)
    out = pl.pallas_call(
        kern,
        grid_spec=pltpu.PrefetchScalarGridSpec(
            num_scalar_prefetch=1,
            grid=(n_exp,),
            in_specs=[any_spec,
                      pl.BlockSpec((bsz, None, cap, 1), lambda e, idx_ref: (0, e, 0, 0)),
                      pl.BlockSpec((bsz, 1, d), lambda e, idx_ref: (0, 0, 0)),
                      pl.BlockSpec((None, d, f), lambda e, idx_ref: (e, 0, 0)),
                      pl.BlockSpec((None, d, f), lambda e, idx_ref: (e, 0, 0)),
                      pl.BlockSpec((None, f, d), lambda e, idx_ref: (e, 0, 0))],
            out_specs=any_spec,
            scratch_shapes=[pltpu.VMEM((cap, d2), F32), pltpu.VMEM((cap, d2), F32),
                            pltpu.SemaphoreType.DMA((bsz,)), pltpu.SemaphoreType.DMA((bsz,))],
        ),
        out_shape=jax.ShapeDtypeStruct((bsz * seq, d2), F32),
        input_output_aliases={1: 0},
        compiler_params=_cparams(("arbitrary",)),
        name="ffn",
    )(idx, rows.reshape(bsz * seq, d2), gate, gt2, w_gate, w_up, w_down)
    return out.reshape(bsz, seq, d2)


def _final_kernel(x_ref, g_ref, o_ref):
    xf = x_ref[...]
    o_ref[...] = xf * lax.rsqrt(jnp.mean(xf * xf, axis=-1, keepdims=True) + EPS) * g_ref[...]


def _final_call(rows, g, tm=1024):
    bsz, seq, d2 = rows.shape
    d = d2 // 2
    return pl.pallas_call(
        _final_kernel,
        grid=(bsz, seq // tm),
        in_specs=[pl.BlockSpec((None, tm, d), lambda b, i: (b, i, 0)),
                  pl.BlockSpec((1, d), lambda b, i: (0, 0))],
        out_specs=pl.BlockSpec((None, tm, d), lambda b, i: (b, i, 0)),
        out_shape=jax.ShapeDtypeStruct((bsz, seq, d), F32),
        compiler_params=_cparams(("arbitrary", "arbitrary")),
        name="final",
    )(rows, g)


def _rope_tables(seq):
    t = np.arange(seq)
    n_freq = HEAD_DIM // 4
    inv = ROPE_THETA ** (-np.arange(n_freq, dtype=np.float64) / n_freq)
    ang = np.concatenate([(t // GRID_W)[:, None] * inv, (t % GRID_W)[:, None] * inv], axis=1)
    cos_h = np.concatenate([np.cos(ang), np.cos(ang)], axis=1)
    sin_h = np.concatenate([-np.sin(ang), np.sin(ang)], axis=1)
    reps = LANES // HEAD_DIM
    return (jnp.asarray(np.tile(cos_h, (1, reps)), F32), jnp.asarray(np.tile(sin_h, (1, reps)), F32))


def kernel(x, c, ctx, c_ctx, w_ada, b_ada, g_mix, w_in, conv_w, sink, w_out, g_ffn,
           w_router, w_gate, w_up, w_down, g_final):
    bsz, seq, d = x.shape
    n_exp = w_router.shape[-1]
    cap = EC_CAPACITY_FACTOR * seq // n_exp
    assert w_ada.shape[0] == 1, "single trunk layer"
    assert bsz + 1 <= SUBLANES and seq % (SUBLANES * LANES) == 0 and cap % SLOT_BLOCK == 0

    cin = jnp.zeros((SUBLANES, d), F32).at[:bsz].set(c).at[bsz].set(c_ctx)
    mod = _mod_call(cin, w_ada[0], b_ada[0][None, :])
    sh1, sc1, gt1, sh2, sc2, gt2 = [m[:bsz, None, :] for m in jnp.split(mod, 6, axis=-1)]
    csh1 = mod[bsz:bsz + 1, 0:d]
    csc1 = mod[bsz:bsz + 1, d:2 * d]

    w_in0 = w_in[0]
    g_mix_r = g_mix[0][None, :]
    kv0 = ATT_WIDTH
    k_c, v_c = _ctx_call(ctx, g_mix_r, csh1, csc1, w_in0[:, kv0:kv0 + 2 * KV_WIDTH])

    cos_t, sin_t = _rope_tables(seq)
    q, k, v, bg, u = _in_call(x, g_mix_r, sh1, sc1, w_in0, cos_t, sin_t)
    att = _attn_call(sink[0], q, k, v, k_c, v_c)
    rows, aff = _out_call(att, u, bg, conv_w[0], w_out[0], x, gt1,
                          g_ffn[0][None, :], sh2, sc2, w_router[0].T)

    n_chunk = seq // LANES
    aff4 = aff.reshape(bsz, n_exp, n_chunk, LANES)
    pos, carry = _sel_call(aff4, cap)
    n_tchunk = n_chunk // SUBLANES
    cum = jnp.concatenate([carry[:, :, ::SUBLANES, 0], jnp.full((bsz, n_exp, 1), cap, F32)], axis=-1)
    cum = cum.astype(jnp.int32).reshape(-1)
    comp = _compact_call(cum, pos.reshape(bsz, n_exp, n_tchunk, SUBLANES, LANES),
                         aff4.reshape(bsz, n_exp, n_tchunk, SUBLANES, LANES), cap)
    idx = (comp[:, :, :, 0, :] * LANES + comp[:, :, :, 1, :]).astype(jnp.int32)
    idx = (idx + (jnp.arange(bsz, dtype=jnp.int32) * seq)[:, None, None, None]).reshape(-1)
    gate = ((comp[:, :, :, 2, :] + comp[:, :, :, 3, :]) + comp[:, :, :, 4, :]).reshape(bsz, n_exp, cap, 1)

    rows = _ffn_call(idx, rows, gate, gt2, w_gate[0], w_up[0], w_down[0], cap)
    return _final_call(rows, g_final[None, :])
```

```python
import functools

import numpy as np
import jax
import jax.numpy as jnp
from jax import lax
from jax.experimental import pallas as pl
from jax.experimental.pallas import tpu as pltpu

GRID_W = 64
N_HEADS = 8
N_KV_HEADS = 2
HEAD_DIM = 64
GQA_GROUP = N_HEADS // N_KV_HEADS
ATT_WIDTH = N_HEADS * HEAD_DIM
KV_WIDTH = N_KV_HEADS * HEAD_DIM
WINDOW = 128
BLOCK = 128
ROPE_THETA = 10000.0
N_EXPERTS = 16
EC_CAPACITY_FACTOR = 2
EPS = 1e-6

LANES = 128
SUBLANES = 8
MXU_COLS = 256
VMEM_LIMIT = 56 * 1024 * 1024

F32 = jnp.float32
BF16 = jnp.bfloat16
NEG = -1e30
LOG2E = 1.4426950408889634


def _cparams(sem):
    return pltpu.CompilerParams(dimension_semantics=sem, vmem_limit_bytes=VMEM_LIMIT)


def _dot(a, b):
    return jnp.dot(a, b, precision=lax.Precision.DEFAULT, preferred_element_type=F32)


def _dot_nt(a, b):
    return lax.dot_general(a, b, (((1,), (1,)), ((), ())), precision=lax.Precision.DEFAULT,
                           preferred_element_type=F32)


def _norm_mod(xf, g, shift, scale):
    r = xf * lax.rsqrt(jnp.mean(xf * xf, axis=-1, keepdims=True) + EPS)
    return (r * g) * (1.0 + scale) + shift


def _mod_kernel(c_ref, w_ref, b_ref, o_ref):
    c = c_ref[...]
    s = c * jax.nn.sigmoid(c)
    o_ref[...] = jnp.dot(s, w_ref[...], precision=lax.Precision.HIGHEST,
                         preferred_element_type=F32) + b_ref[...]


def _mod_call(cin, w, bias, tn=1024):
    d, n = w.shape
    return pl.pallas_call(
        _mod_kernel,
        grid=(n // tn,),
        in_specs=[pl.BlockSpec((SUBLANES, d), lambda j: (0, 0)),
                  pl.BlockSpec((d, tn), lambda j: (0, j)),
                  pl.BlockSpec((1, tn), lambda j: (0, j))],
        out_specs=pl.BlockSpec((SUBLANES, tn), lambda j: (0, j)),
        out_shape=jax.ShapeDtypeStruct((SUBLANES, n), F32),
        compiler_params=_cparams(("arbitrary",)),
        name="mod",
    )(cin, w, bias)


def _ctx_kernel(x_ref, g_ref, sh_ref, sc_ref, w_ref, k_ref, v_ref):
    h = _norm_mod(x_ref[...], g_ref[...], sh_ref[...], sc_ref[...])
    kv = _dot(h, w_ref[...])
    k_ref[...] = kv[:, :KV_WIDTH]
    v_ref[...] = kv[:, KV_WIDTH:]


def _ctx_call(ctx, g, sh, sc, w_kv):
    bsz, n, d = ctx.shape
    vec = pl.BlockSpec((1, d), lambda b: (0, 0))
    kv_spec = pl.BlockSpec((None, n, KV_WIDTH), lambda b: (b, 0, 0))
    return pl.pallas_call(
        _ctx_kernel,
        grid=(bsz,),
        in_specs=[pl.BlockSpec((None, n, d), lambda b: (b, 0, 0)), vec, vec, vec,
                  pl.BlockSpec((d, 2 * KV_WIDTH), lambda b: (0, 0))],
        out_specs=[kv_spec, kv_spec],
        out_shape=[jax.ShapeDtypeStruct((bsz, n, KV_WIDTH), F32)] * 2,
        compiler_params=_cparams(("arbitrary",)),
        name="ctx_kv",
    )(ctx, g, sh, sc, w_kv)


def _rope(t, cos, sin_signed):
    lane = lax.broadcasted_iota(jnp.int32, t.shape, 1)
    first = (lane % HEAD_DIM) < (HEAD_DIM // 2)
    swapped = jnp.where(first, pltpu.roll(t, LANES - HEAD_DIM // 2, 1), pltpu.roll(t, HEAD_DIM // 2, 1))
    return t * cos + swapped * sin_signed


def _in_kernel(x_ref, g_ref, sh_ref, sc_ref, w_ref, cos_ref, sin_ref,
               q_ref, k_ref, v_ref, bg_ref, u_ref, *, conv_width):
    h = _norm_mod(x_ref[...], g_ref[...], sh_ref[...], sc_ref[...])
    cos = cos_ref[...]
    sin = sin_ref[...]
    scale = HEAD_DIM ** -0.5 * LOG2E
    for j in range(ATT_WIDTH // MXU_COLS):
        qj = _dot(h, w_ref[:, j * MXU_COLS:(j + 1) * MXU_COLS])
        for jj in range(MXU_COLS // LANES):
            c0 = j * MXU_COLS + jj * LANES
            q_ref[:, c0:c0 + LANES] = _rope(qj[:, jj * LANES:(jj + 1) * LANES], cos, sin) * scale
    o = ATT_WIDTH
    kv = _dot(h, w_ref[:, o:o + 2 * KV_WIDTH])
    k_ref[...] = _rope(kv[:, :KV_WIDTH], cos, sin)
    v_ref[...] = kv[:, KV_WIDTH:]
    o += 2 * KV_WIDTH
    bg_ref[...] = _dot(h, w_ref[:, o:o + conv_width])
    o += conv_width
    cg = _dot(h, w_ref[:, o:o + conv_width])
    o += conv_width
    hv = _dot(h, w_ref[:, o:o + conv_width])
    u_ref[...] = cg * hv


def _in_call(x, g, sh, sc, w_in, cos_t, sin_t, tm=1024):
    bsz, seq, d = x.shape
    n_in = w_in.shape[1]
    cw = d - ATT_WIDTH
    vec = pl.BlockSpec((1, d), lambda b, i: (0, 0))
    bvec = pl.BlockSpec((None, 1, d), lambda b, i: (b, 0, 0))
    tab = pl.BlockSpec((tm, LANES), lambda b, i: (i, 0))

    def rows(width):
        return pl.BlockSpec((None, tm, width), lambda b, i: (b, i, 0))

    return pl.pallas_call(
        functools.partial(_in_kernel, conv_width=cw),
        grid=(bsz, seq // tm),
        in_specs=[rows(d), vec, bvec, bvec, pl.BlockSpec((d, n_in), lambda b, i: (0, 0)), tab, tab],
        out_specs=[rows(ATT_WIDTH), rows(KV_WIDTH), rows(KV_WIDTH), rows(cw), rows(cw)],
        out_shape=[jax.ShapeDtypeStruct((bsz, seq, ATT_WIDTH), F32),
                   jax.ShapeDtypeStruct((bsz, seq, KV_WIDTH), F32),
                   jax.ShapeDtypeStruct((bsz, seq, KV_WIDTH), F32),
                   jax.ShapeDtypeStruct((bsz, seq, cw), F32),
                   jax.ShapeDtypeStruct((bsz, seq, cw), F32)],
        compiler_params=_cparams(("arbitrary", "arbitrary")),
        name="in_proj",
    )(x, g, sh, sc, w_in, cos_t, sin_t)


ATTN_TILE = 32


def _attn_kernel(sink_ref, q_ref, k_ref, v_ref, kc_ref, vc_ref, o_ref,
                 s_scr, p_scr, bias_scr, m_scr, *, seq, tq):
    i = pl.program_id(1)
    nblk = tq // BLOCK
    span = 3 * BLOCK
    width = s_scr.shape[1]
    rows_all = GQA_GROUP * BLOCK
    tiles_per_head = BLOCK // ATTN_TILE
    kc = kc_ref[...]
    vc = vc_ref[...]
    for blk in range(nblk):
        n = i * nblk + blk
        ws = pl.multiple_of(jnp.clip((n - 1) * BLOCK, 0, seq - span), BLOCK)
        kw = k_ref[pl.ds(ws, span), :]
        vw = v_ref[pl.ds(ws, span), :]
        qb = q_ref[blk * BLOCK:(blk + 1) * BLOCK, :]
        qpos = n * BLOCK + lax.broadcasted_iota(jnp.int32, (BLOCK, span), 0)
        kpos = ws + lax.broadcasted_iota(jnp.int32, (BLOCK, span), 1)
        bias_scr[...] = jnp.where(jnp.abs(kpos - qpos) <= WINDOW, 0.0, NEG)
        outs = []
        for kh in range(N_KV_HEADS):
            heads = [kh * GQA_GROUP + g for g in range(GQA_GROUP)]
            qs = jnp.concatenate([qb[:, h * HEAD_DIM:(h + 1) * HEAD_DIM] for h in heads], axis=0)
            lo, hi = kh * HEAD_DIM, (kh + 1) * HEAD_DIM
            s_scr[:, 0:span] = _dot_nt(qs, kw[:, lo:hi])
            s_scr[:, span:width] = _dot_nt(qs, kc[:, lo:hi])

            for t in range(rows_all // ATTN_TILE):
                r0 = t * ATTN_TILE
                b0 = (t % tiles_per_head) * ATTN_TILE
                sk = sink_ref[kh * GQA_GROUP + t // tiles_per_head] * LOG2E
                s_loc = s_scr[pl.ds(r0, ATTN_TILE), 0:span] + bias_scr[pl.ds(b0, ATTN_TILE), :]
                s_ctx = s_scr[pl.ds(r0, ATTN_TILE), span:width]
                m = jnp.maximum(jnp.maximum(jnp.max(s_loc, axis=1, keepdims=True),
                                            jnp.max(s_ctx, axis=1, keepdims=True)), sk)
                m_scr[pl.ds(r0, ATTN_TILE), :] = m
                p_scr[pl.ds(r0, ATTN_TILE), 0:span] = jnp.exp2(s_loc - m)
                p_scr[pl.ds(r0, ATTN_TILE), span:width] = jnp.exp2(s_ctx - m)
            own = (lax.broadcasted_iota(jnp.int32, (1, KV_WIDTH), 1) // HEAD_DIM) == kh
            o_ext = (_dot(p_scr[:, 0:span], jnp.where(own, vw, 1.0))
                     + _dot(p_scr[:, span:width], jnp.where(own, vc, 1.0)))
            other = (1 - kh) * HEAD_DIM
            sk_col = jnp.concatenate([jnp.full((BLOCK, 1), sink_ref[h] * LOG2E, F32) for h in heads], axis=0)
            den = o_ext[:, other:other + 1] + jnp.exp2(sk_col - m_scr[...])
            o = o_ext[:, lo:hi] / den
            outs.extend(o[g * BLOCK:(g + 1) * BLOCK] for g in range(GQA_GROUP))
        o_ref[blk * BLOCK:(blk + 1) * BLOCK, :] = jnp.concatenate(outs, axis=1)


def _attn_call(sink, q, k, v, kc, vc, tq=512):
    bsz, seq, _ = q.shape
    n_ctx = kc.shape[1]
    span = 3 * BLOCK
    rows_all = GQA_GROUP * BLOCK
    full_kv = pl.BlockSpec((None, seq, KV_WIDTH), lambda b, i: (b, 0, 0))
    ctx_kv = pl.BlockSpec((None, n_ctx, KV_WIDTH), lambda b, i: (b, 0, 0))
    qspec = pl.BlockSpec((None, tq, ATT_WIDTH), lambda b, i: (b, i, 0))
    return pl.pallas_call(
        functools.partial(_attn_kernel, seq=seq, tq=tq),
        grid=(bsz, seq // tq),
        in_specs=[pl.BlockSpec(memory_space=pltpu.SMEM), qspec, full_kv, full_kv, ctx_kv, ctx_kv],
        out_specs=qspec,
        out_shape=jax.ShapeDtypeStruct((bsz, seq, ATT_WIDTH), F32),
        scratch_shapes=[pltpu.VMEM((rows_all, span + n_ctx), F32),
                        pltpu.VMEM((rows_all, span + n_ctx), F32),
                        pltpu.VMEM((BLOCK, span), F32),
                        pltpu.VMEM((rows_all, 1), F32)],
        compiler_params=_cparams(("arbitrary", "arbitrary")),
        name="attn",
    )(sink, q, k, v, kc, vc)


def _out_kernel(att_ref, u_ref, up_ref, un_ref, bg_ref, cw_ref, wo_ref, x_ref, gt_ref,
                g_ref, sh_ref, sc_ref, wr_ref, rows_ref, aff_ref, *, tm):
    i = pl.program_id(1)
    last = pl.num_programs(1) - 1
    u = u_ref[...]
    row = lax.broadcasted_iota(jnp.int32, u.shape, 0)
    prev_row = jnp.where(i > 0, up_ref[SUBLANES - 1:SUBLANES, :], 0.0)
    next_row = jnp.where(i < last, un_ref[0:1, :], 0.0)
    u_prev = jnp.where(row == 0, prev_row, pltpu.roll(u, 1, 0))
    u_next = jnp.where(row == tm - 1, next_row, pltpu.roll(u, tm - 1, 0))
    conv = bg_ref[...] * (cw_ref[0:1, :] * u_prev + cw_ref[1:2, :] * u + cw_ref[2:3, :] * u_next)
    y = _dot(att_ref[...], wo_ref[0:ATT_WIDTH, :]) + _dot(conv, wo_ref[ATT_WIDTH:, :])
    x1 = x_ref[...] + gt_ref[...] * y
    d = x1.shape[-1]
    h2 = _norm_mod(x1, g_ref[...], sh_ref[...], sc_ref[...])
    rows_ref[:, 0:d] = x1
    rows_ref[:, d:2 * d] = h2
    def split3(t):
        hi = t.astype(BF16)
        r1 = t - hi.astype(F32)
        mid = r1.astype(BF16)
        return hi, mid, (r1 - mid.astype(F32)).astype(BF16)

    def nt_bf16(a, b):
        return lax.dot_general(a, b, (((1,), (1,)), ((), ())), preferred_element_type=F32)

    h_hi, h_mid, h_lo = split3(h2)
    w_hi, w_mid, w_lo = split3(wr_ref[...])
    n_exp = w_hi.shape[0]
    by_hi = nt_bf16(jnp.concatenate([w_hi, w_mid, w_lo], axis=0), h_hi)
    by_mid = nt_bf16(jnp.concatenate([w_hi, w_mid], axis=0), h_mid)
    logits = (((by_hi[2 * n_exp:] + by_mid[n_exp:]) + nt_bf16(w_hi, h_lo))
              + (by_hi[n_exp:2 * n_exp] + by_mid[:n_exp])) + by_hi[:n_exp]
    mx = jnp.max(logits, axis=0, keepdims=True)
    ex = jnp.exp(logits - mx)
    aff_ref[...] = ex / jnp.sum(ex, axis=0, keepdims=True)


def _out_call(att, u, bg, conv_w, w_out, x, gt1, g_ffn, sh2, sc2, w_router_t, tm=1024):
    bsz, seq, d = x.shape
    cw = u.shape[-1]
    n_exp = w_router_t.shape[0]
    nh = seq // SUBLANES

    def rows(width):
        return pl.BlockSpec((None, tm, width), lambda b, i: (b, i, 0))

    halo_prev = pl.BlockSpec((None, SUBLANES, cw),
                             lambda b, i: (b, jnp.maximum(i * (tm // SUBLANES) - 1, 0), 0))
    halo_next = pl.BlockSpec((None, SUBLANES, cw),
                             lambda b, i: (b, jnp.minimum((i + 1) * (tm // SUBLANES), nh - 1), 0))
    vec = pl.BlockSpec((1, d), lambda b, i: (0, 0))
    bvec = pl.BlockSpec((None, 1, d), lambda b, i: (b, 0, 0))
    return pl.pallas_call(
        functools.partial(_out_kernel, tm=tm),
        grid=(bsz, seq // tm),
        in_specs=[rows(ATT_WIDTH), rows(cw), halo_prev, halo_next, rows(cw),
                  pl.BlockSpec((3, cw), lambda b, i: (0, 0)),
                  pl.BlockSpec((d, d), lambda b, i: (0, 0)),
                  rows(d), bvec, vec, bvec, bvec,
                  pl.BlockSpec((n_exp, d), lambda b, i: (0, 0))],
        out_specs=[rows(2 * d), pl.BlockSpec((None, n_exp, tm), lambda b, i: (b, 0, i))],
        out_shape=[jax.ShapeDtypeStruct((bsz, seq, 2 * d), F32),
                   jax.ShapeDtypeStruct((bsz, n_exp, seq), F32)],
        compiler_params=_cparams(("arbitrary", "arbitrary")),
        name="out_proj",
    )(att, u, u, u, bg, conv_w, w_out, x, gt1, g_ffn, sh2, sc2, w_router_t)


def _prefix_rows(m, tri, ones, low):
    n_exp, n_chunk, _ = m.shape
    mb = m.astype(BF16).reshape(n_exp * n_chunk, LANES)
    within = jnp.dot(mb, tri, preferred_element_type=F32).reshape(n_exp, n_chunk, LANES)
    tot = jnp.dot(mb, ones, preferred_element_type=F32).astype(BF16).reshape(n_exp, n_chunk, LANES)
    carry = jnp.stack([jnp.dot(low, tot[e], preferred_element_type=F32) for e in range(n_exp)], axis=0)
    return carry + within, carry


def _sel_kernel(aff_ref, pos_ref, carry_ref, *, cap):
    a = aff_ref[...]
    n_exp, n_chunk, _ = a.shape
    bits = lax.bitcast_convert_type(a, jnp.int32)

    def count(pred):
        c = jnp.sum(pred.astype(F32), axis=2, keepdims=True)
        return jnp.sum(c, axis=1, keepdims=True)

    def body(it, thr):
        cand = thr | jnp.left_shift(jnp.int32(1), 30 - it)
        return jnp.where(count(bits >= cand) >= cap, cand, thr)

    thr = lax.fori_loop(0, 31, body, jnp.zeros((n_exp, 1, 1), jnp.int32))
    gt = bits > thr
    eq = bits == thr
    need = cap - count(gt)

    r = lax.broadcasted_iota(jnp.int32, (LANES, LANES), 0)
    c = lax.broadcasted_iota(jnp.int32, (LANES, LANES), 1)
    tri = (r < c).astype(BF16)
    ones = jnp.ones((LANES, LANES), BF16)
    rr = lax.broadcasted_iota(jnp.int32, (n_chunk, n_chunk), 0)
    cc = lax.broadcasted_iota(jnp.int32, (n_chunk, n_chunk), 1)
    low = (cc < rr).astype(BF16)

    eq_rank, _ = _prefix_rows(eq, tri, ones, low)
    sel = gt | (eq & (eq_rank < need))
    pos, carry = _prefix_rows(sel, tri, ones, low)
    pos_ref[...] = jnp.where(sel, pos, -1.0)
    carry_ref[...] = carry


def _sel_call(aff4, cap):
    bsz, n_exp, n_chunk, _ = aff4.shape
    spec = pl.BlockSpec((None, n_exp, n_chunk, LANES), lambda b: (b, 0, 0, 0))
    return pl.pallas_call(
        functools.partial(_sel_kernel, cap=cap),
        grid=(bsz,),
        in_specs=[spec],
        out_specs=[spec, spec],
        out_shape=[jax.ShapeDtypeStruct(aff4.shape, F32)] * 2,
        compiler_params=_cparams(("arbitrary",)),
        name="select",
    )(aff4)


COMPACT_ROWS = 16
SLOT_BLOCK = LANES


def _compact_kernel(cum_ref, pos_ref, aff_ref, out_ref, *, n_exp, n_tchunk):
    b = pl.program_id(0)
    e = pl.program_id(1)
    base = (b * n_exp + e) * (n_tchunk + 1)
    out_ref[...] = jnp.zeros(out_ref.shape, F32)
    nt = (((1,), (1,)), ((), ()))
    rowi = lax.broadcasted_iota(jnp.int32, (COMPACT_ROWS, LANES), 0)
    lane = lax.broadcasted_iota(jnp.int32, (COMPACT_ROWS, LANES), 1).astype(F32)
    slot_b = lax.broadcasted_iota(jnp.int32, (SLOT_BLOCK, LANES), 0).astype(F32).astype(BF16)
    one_b = jnp.ones((), BF16)
    zero_b = jnp.zeros((), BF16)

    n_sblk = out_ref.shape[0] - 1

    def chunk_vals(c):
        a = aff_ref[c]
        g_hi = a.astype(BF16).astype(F32)
        r1 = a - g_hi
        g_mid = r1.astype(BF16).astype(F32)
        g_lo = r1 - g_mid
        pieces = []
        for r in range(SUBLANES):
            def bc(t):
                return jnp.broadcast_to(t[r:r + 1, :], (COMPACT_ROWS, LANES))

            piece = jnp.where(rowi == 0, float(c * SUBLANES + r),
                              jnp.where(rowi == 1, lane,
                                        jnp.where(rowi == 2, bc(g_hi),
                                                  jnp.where(rowi == 3, bc(g_mid),
                                                            jnp.where(rowi == 4, bc(g_lo), 0.0)))))
            pieces.append(piece)
        return jnp.concatenate(pieces, axis=1).astype(BF16)

    def visit(j, p, vals):
        rel = p - jnp.asarray(j * SLOT_BLOCK).astype(F32)
        rel = jnp.where((rel >= 0.0) & (rel < SLOT_BLOCK), rel, -1.0)
        pieces_oh = []
        for r in range(SUBLANES):
            row16 = jnp.broadcast_to(rel[r:r + 1, :], (2 * SUBLANES, LANES)).astype(BF16)
            blk = jnp.concatenate([row16] * (SLOT_BLOCK // (2 * SUBLANES)), axis=0)
            pieces_oh.append(jnp.where(blk == slot_b, one_b, zero_b))
        oh = jnp.concatenate(pieces_oh, axis=1)
        out_ref[j] += lax.dot_general(vals, oh, nt, preferred_element_type=F32)

    def first_block(c):
        return jnp.minimum(cum_ref[base + c] // SLOT_BLOCK, n_sblk - 1)

    for c in range(n_tchunk):
        p = pos_ref[c]
        vals = chunk_vals(c)
        j0 = first_block(c)
        visit(j0, p, vals)
        visit(j0 + 1, p, vals)

    for c in range(n_tchunk):
        lo = cum_ref[base + c]
        hi = cum_ref[base + c + 1]
        j_end = jnp.where(hi > lo, (hi - 1) // SLOT_BLOCK + 1, 0)

        def more(j, carry, c=c):
            visit(j, pos_ref[c], chunk_vals(c))
            return carry

        lax.fori_loop(first_block(c) + 2, j_end, more, 0)


def _compact_call(cum, pos5, aff5, cap):
    bsz, n_exp, n_tchunk, _, _ = pos5.shape
    n_sblk = cap // SLOT_BLOCK
    spec = pl.BlockSpec((None, None, n_tchunk, SUBLANES, LANES), lambda b, e, cum_ref: (b, e, 0, 0, 0))
    return pl.pallas_call(
        functools.partial(_compact_kernel, n_exp=n_exp, n_tchunk=n_tchunk),
        grid_spec=pltpu.PrefetchScalarGridSpec(
            num_scalar_prefetch=1,
            grid=(bsz, n_exp),
            in_specs=[spec, spec],
            out_specs=pl.BlockSpec((None, None, n_sblk + 1, COMPACT_ROWS, LANES),
                                   lambda b, e, cum_ref: (b, e, 0, 0, 0)),
        ),
        out_shape=jax.ShapeDtypeStruct((bsz, n_exp, n_sblk + 1, COMPACT_ROWS, LANES), F32),
        compiler_params=_cparams(("arbitrary", "arbitrary")),
        name="compact",
    )(cum, pos5, aff5)[:, :, :n_sblk]


FFN_ROWS = 256


FFN_COLS = 256


def _ffn_kernel(idx_ref, rows_in, g_ref, gt_ref, wg_ref, wu_ref, wd_ref, rows_hbm,
                buf0, buf1, wgb, wub, wdb, gsem, ssem, *, seq, cap, n_exp, bsz, d):
    del rows_in
    e = pl.program_id(0)
    bufs = (buf0, buf1)
    n_chunk = cap // FFN_ROWS
    f = wgb.shape[1]
    acc_cols = pl.ds(0, d)

    def hbm_row(ee, b, s):
        return idx_ref[(b * n_exp + ee) * cap + s]

    def dma_thread(s):
        return s % 2 if isinstance(s, int) else 0

    def gather_start(ee, b, s):
        pltpu.make_async_copy(rows_hbm.at[pl.ds(hbm_row(ee, b, s), 1)], bufs[b].at[pl.ds(s, 1)],
                              gsem.at[b]).start(priority=dma_thread(s))

    def gather_wait(b):
        pltpu.make_async_copy(rows_hbm.at[pl.ds(0, cap)], bufs[b], gsem.at[b]).wait()

    def scatter_start(ee, b, s):
        pltpu.make_async_copy(bufs[b].at[pl.ds(s, 1), acc_cols],
                              rows_hbm.at[pl.ds(hbm_row(ee, b, s), 1), acc_cols],
                              ssem.at[b]).start(priority=dma_thread(s))

    def scatter_wait(b):
        pltpu.make_async_copy(bufs[b].at[:, acc_cols], rows_hbm.at[pl.ds(0, cap), acc_cols],
                              ssem.at[b]).wait()

    def compute_chunk(b, ci, todo):
        buf = bufs[b]
        rows = slice(ci * FFN_ROWS, (ci + 1) * FFN_ROWS)
        n_piece = 3 * (f // FFN_COLS)
        per_piece = -(-len(todo) // n_piece)

        def issue_some():
            for _ in range(min(per_piece, len(todo))):
                todo.pop(0)()

        xc = buf[rows, d:2 * d].astype(BF16)
        g_col = g_ref[b, rows, :]
        gt = gt_ref[b]
        hm = []
        for nj in range(f // FFN_COLS):
            cols = slice(nj * FFN_COLS, (nj + 1) * FFN_COLS)
            a = jnp.dot(xc, wgb[:, cols], preferred_element_type=F32)
            issue_some()
            u = jnp.dot(xc, wub[:, cols], preferred_element_type=F32)
            issue_some()
            hm.append((a * jax.nn.sigmoid(a) * u).astype(BF16))
        hm = jnp.concatenate(hm, axis=1)
        for nj in range(d // FFN_COLS):
            cols = slice(nj * FFN_COLS, (nj + 1) * FFN_COLS)
            y = jnp.dot(hm, wdb[:, cols], preferred_element_type=F32) * g_col
            buf[rows, cols] = buf[rows, cols] + gt[:, cols] * y
            issue_some()
        assert not todo

    def chunk_rows(ci):
        return range(ci * FFN_ROWS, (ci + 1) * FFN_ROWS)

    def split(n, parts):
        bounds = [n * k // parts for k in range(parts + 1)]
        return [range(bounds[k], bounds[k + 1]) for k in range(parts)]

    @pl.when(e == 0)
    def _():
        def first(s, carry):
            gather_start(0, 0, s)
            return carry
        lax.fori_loop(0, cap, first, 0)

    @pl.when(e > 0)
    def _():
        scatter_wait(1)

    wgb[...] = wg_ref[...].astype(BF16)
    wub[...] = wu_ref[...].astype(BF16)
    wdb[...] = wd_ref[...].astype(BF16)

    gather_wait(0)
    for ci in range(n_chunk):
        todo = [functools.partial(gather_start, e, 1, s) for s in chunk_rows(ci)]
        if ci > 0:
            todo += [functools.partial(scatter_start, e, 0, s) for s in chunk_rows(ci - 1)]
        compute_chunk(0, ci, todo)

    ne = jnp.minimum(e + 1, n_exp - 1)
    gather_wait(1)
    compute_chunk(1, 0, [functools.partial(scatter_start, e, 0, s) for s in chunk_rows(n_chunk - 1)])
    scatter_wait(0)
    for ci, part in zip(range(1, n_chunk), split(cap, n_chunk - 1)):
        todo = [functools.partial(gather_start, ne, 0, s) for s in part]
        todo += [functools.partial(scatter_start, e, 1, s) for s in chunk_rows(ci - 1)]
        compute_chunk(1, ci, todo)
    for s in chunk_rows(n_chunk - 1):
        scatter_start(e, 1, s)

    @pl.when(e == n_exp - 1)
    def _():
        scatter_wait(1)
        gather_wait(0)


def _ffn_call(idx, rows, gate, gt2, w_gate, w_up, w_down, cap):
    bsz, seq, d2 = rows.shape
    d = d2 // 2
    n_exp, _, f = w_gate.shape
    assert bsz == 2, "ffn double-buffers exactly two samples per expert"
    kern = functools.partial(_ffn_kernel, seq=seq, cap=cap, n_exp=n_exp, bsz=bsz, d=d)
    any_spec = pl.BlockSpec(memory_space=pl.ANY)
    out = pl.pallas_call(
        kern,
        grid_spec=pltpu.PrefetchScalarGridSpec(
            num_scalar_prefetch=1,
            grid=(n_exp,),
            in_specs=[any_spec,
                      pl.BlockSpec((bsz, None, cap, 1), lambda e, idx_ref: (0, e, 0, 0)),
                      pl.BlockSpec((bsz, 1, d), lambda e, idx_ref: (0, 0, 0)),
                      pl.BlockSpec((None, d, f), lambda e, idx_ref: (e, 0, 0)),
                      pl.BlockSpec((None, d, f), lambda e, idx_ref: (e, 0, 0)),
                      pl.BlockSpec((None, f, d), lambda e, idx_ref: (e, 0, 0))],
            out_specs=any_spec,
            scratch_shapes=[pltpu.VMEM((cap, d2), F32), pltpu.VMEM((cap, d2), F32),
                            pltpu.VMEM((d, f), BF16), pltpu.VMEM((d, f), BF16), pltpu.VMEM((f, d), BF16),
                            pltpu.SemaphoreType.DMA((bsz,)), pltpu.SemaphoreType.DMA((bsz,))],
        ),
        out_shape=jax.ShapeDtypeStruct((bsz * seq, d2), F32),
        input_output_aliases={1: 0},
        compiler_params=_cparams(("arbitrary",)),
        name="ffn",
    )(idx, rows.reshape(bsz * seq, d2), gate, gt2, w_gate, w_up, w_down)
    return out.reshape(bsz, seq, d2)


def _final_kernel(x_ref, g_ref, o_ref):
    xf = x_ref[...]
    o_ref[...] = xf * lax.rsqrt(jnp.mean(xf * xf, axis=-1, keepdims=True) + EPS) * g_ref[...]


def _final_call(rows, g, tm=1024):
    bsz, seq, d2 = rows.shape
    d = d2 // 2
    return pl.pallas_call(
        _final_kernel,
        grid=(bsz, seq // tm),
        in_specs=[pl.BlockSpec((None, tm, d), lambda b, i: (b, i, 0)),
                  pl.BlockSpec((1, d), lambda b, i: (0, 0))],
        out_specs=pl.BlockSpec((None, tm, d), lambda b, i: (b, i, 0)),
        out_shape=jax.ShapeDtypeStruct((bsz, seq, d), F32),
        compiler_params=_cparams(("arbitrary", "arbitrary")),
        name="final",
    )(rows, g)


def _rope_tables(seq):
    t = np.arange(seq)
    n_freq = HEAD_DIM // 4
    inv = ROPE_THETA ** (-np.arange(n_freq, dtype=np.float64) / n_freq)
    ang = np.concatenate([(t // GRID_W)[:, None] * inv, (t % GRID_W)[:, None] * inv], axis=1)
    cos_h = np.concatenate([np.cos(ang), np.cos(ang)], axis=1)
    sin_h = np.concatenate([-np.sin(ang), np.sin(ang)], axis=1)
    reps = LANES // HEAD_DIM
    return (jnp.asarray(np.tile(cos_h, (1, reps)), F32), jnp.asarray(np.tile(sin_h, (1, reps)), F32))


def kernel(x, c, ctx, c_ctx, w_ada, b_ada, g_mix, w_in, conv_w, sink, w_out, g_ffn,
           w_router, w_gate, w_up, w_down, g_final):
    bsz, seq, d = x.shape
    n_exp = w_router.shape[-1]
    cap = EC_CAPACITY_FACTOR * seq // n_exp
    assert w_ada.shape[0] == 1, "single trunk layer"
    assert bsz + 1 <= SUBLANES and seq % (SUBLANES * LANES) == 0 and cap % SLOT_BLOCK == 0

    cin = jnp.zeros((SUBLANES, d), F32).at[:bsz].set(c).at[bsz].set(c_ctx)
    mod = _mod_call(cin, w_ada[0], b_ada[0][None, :])
    sh1, sc1, gt1, sh2, sc2, gt2 = [m[:bsz, None, :] for m in jnp.split(mod, 6, axis=-1)]
    csh1 = mod[bsz:bsz + 1, 0:d]
    csc1 = mod[bsz:bsz + 1, d:2 * d]

    w_in0 = w_in[0]
    g_mix_r = g_mix[0][None, :]
    kv0 = ATT_WIDTH
    k_c, v_c = _ctx_call(ctx, g_mix_r, csh1, csc1, w_in0[:, kv0:kv0 + 2 * KV_WIDTH])

    cos_t, sin_t = _rope_tables(seq)
    q, k, v, bg, u = _in_call(x, g_mix_r, sh1, sc1, w_in0, cos_t, sin_t)
    att = _attn_call(sink[0], q, k, v, k_c, v_c)
    rows, aff = _out_call(att, u, bg, conv_w[0], w_out[0], x, gt1,
                          g_ffn[0][None, :], sh2, sc2, w_router[0].T)

    n_chunk = seq // LANES
    aff4 = aff.reshape(bsz, n_exp, n_chunk, LANES)
    pos, carry = _sel_call(aff4, cap)
    n_tchunk = n_chunk // SUBLANES
    cum = jnp.concatenate([carry[:, :, ::SUBLANES, 0], jnp.full((bsz, n_exp, 1), cap, F32)], axis=-1)
    cum = cum.astype(jnp.int32).reshape(-1)
    comp = _compact_call(cum, pos.reshape(bsz, n_exp, n_tchunk, SUBLANES, LANES),
                         aff4.reshape(bsz, n_exp, n_tchunk, SUBLANES, LANES), cap)
    idx = (comp[:, :, :, 0, :] * LANES + comp[:, :, :, 1, :]).astype(jnp.int32)
    idx = (idx + (jnp.arange(bsz, dtype=jnp.int32) * seq)[:, None, None, None]).reshape(-1)
    gate = ((comp[:, :, :, 2, :] + comp[:, :, :, 3, :]) + comp[:, :, :, 4, :]).reshape(bsz, n_exp, cap, 1)

    rows = _ffn_call(idx, rows, gate, gt2, w_gate[0], w_up[0], w_down[0], cap)
    return _final_call(rows, g_final[None, :])
```

```python
import functools

import numpy as np
import jax
import jax.numpy as jnp
from jax import lax
from jax.experimental import pallas as pl
from jax.experimental.pallas import tpu as pltpu

GRID_W = 64
N_HEADS = 8
N_KV_HEADS = 2
HEAD_DIM = 64
GQA_GROUP = N_HEADS // N_KV_HEADS
ATT_WIDTH = N_HEADS * HEAD_DIM
KV_WIDTH = N_KV_HEADS * HEAD_DIM
WINDOW = 128
BLOCK = 128
ROPE_THETA = 10000.0
N_EXPERTS = 16
EC_CAPACITY_FACTOR = 2
EPS = 1e-6

LANES = 128
SUBLANES = 8
MXU_COLS = 256
VMEM_LIMIT = 56 * 1024 * 1024

F32 = jnp.float32
BF16 = jnp.bfloat16
NEG = -1e30
LOG2E = 1.4426950408889634


def _cparams(sem):
    return pltpu.CompilerParams(dimension_semantics=sem, vmem_limit_bytes=VMEM_LIMIT)


def _dot(a, b):
    return jnp.dot(a, b, precision=lax.Precision.DEFAULT, preferred_element_type=F32)


def _dot_nt(a, b):
    return lax.dot_general(a, b, (((1,), (1,)), ((), ())), precision=lax.Precision.DEFAULT,
                           preferred_element_type=F32)


def _norm_mod(xf, g, shift, scale):
    r = xf * lax.rsqrt(jnp.mean(xf * xf, axis=-1, keepdims=True) + EPS)
    return (r * g) * (1.0 + scale) + shift


def _mod_kernel(c_ref, w_ref, b_ref, o_ref):
    c = c_ref[...]
    s = c * jax.nn.sigmoid(c)
    o_ref[...] = _dot(s, w_ref[...]) + b_ref[...]


def _mod_call(cin, w, bias, tn=1024):
    d, n = w.shape
    return pl.pallas_call(
        _mod_kernel,
        grid=(n // tn,),
        in_specs=[pl.BlockSpec((SUBLANES, d), lambda j: (0, 0)),
                  pl.BlockSpec((d, tn), lambda j: (0, j)),
                  pl.BlockSpec((1, tn), lambda j: (0, j))],
        out_specs=pl.BlockSpec((SUBLANES, tn), lambda j: (0, j)),
        out_shape=jax.ShapeDtypeStruct((SUBLANES, n), F32),
        compiler_params=_cparams(("arbitrary",)),
        name="mod",
    )(cin, w, bias)


def _ctx_kernel(x_ref, g_ref, sh_ref, sc_ref, w_ref, k_ref, v_ref):
    h = _norm_mod(x_ref[...], g_ref[...], sh_ref[...], sc_ref[...])
    kv = _dot(h, w_ref[...])
    k_ref[...] = kv[:, :KV_WIDTH]
    v_ref[...] = kv[:, KV_WIDTH:]


def _ctx_call(ctx, g, sh, sc, w_kv):
    bsz, n, d = ctx.shape
    vec = pl.BlockSpec((1, d), lambda b: (0, 0))
    kv_spec = pl.BlockSpec((None, n, KV_WIDTH), lambda b: (b, 0, 0))
    return pl.pallas_call(
        _ctx_kernel,
        grid=(bsz,),
        in_specs=[pl.BlockSpec((None, n, d), lambda b: (b, 0, 0)), vec, vec, vec,
                  pl.BlockSpec((d, 2 * KV_WIDTH), lambda b: (0, 0))],
        out_specs=[kv_spec, kv_spec],
        out_shape=[jax.ShapeDtypeStruct((bsz, n, KV_WIDTH), F32)] * 2,
        compiler_params=_cparams(("arbitrary",)),
        name="ctx_kv",
    )(ctx, g, sh, sc, w_kv)


def _rope(t, cos, sin_signed):
    lane = lax.broadcasted_iota(jnp.int32, t.shape, 1)
    first = (lane % HEAD_DIM) < (HEAD_DIM // 2)
    swapped = jnp.where(first, pltpu.roll(t, LANES - HEAD_DIM // 2, 1), pltpu.roll(t, HEAD_DIM // 2, 1))
    return t * cos + swapped * sin_signed


def _in_kernel(x_ref, g_ref, sh_ref, sc_ref, w_ref, cos_ref, sin_ref,
               q_ref, k_ref, v_ref, bg_ref, u_ref, *, conv_width):
    h = _norm_mod(x_ref[...], g_ref[...], sh_ref[...], sc_ref[...])
    cos = cos_ref[...]
    sin = sin_ref[...]
    scale = HEAD_DIM ** -0.5 * LOG2E
    for j in range(ATT_WIDTH // MXU_COLS):
        qj = _dot(h, w_ref[:, j * MXU_COLS:(j + 1) * MXU_COLS])
        for jj in range(MXU_COLS // LANES):
            c0 = j * MXU_COLS + jj * LANES
            q_ref[:, c0:c0 + LANES] = _rope(qj[:, jj * LANES:(jj + 1) * LANES], cos, sin) * scale
    o = ATT_WIDTH
    kv = _dot(h, w_ref[:, o:o + 2 * KV_WIDTH])
    k_ref[...] = _rope(kv[:, :KV_WIDTH], cos, sin)
    v_ref[...] = kv[:, KV_WIDTH:]
    o += 2 * KV_WIDTH
    bg_ref[...] = _dot(h, w_ref[:, o:o + conv_width])
    o += conv_width
    cg = _dot(h, w_ref[:, o:o + conv_width])
    o += conv_width
    hv = _dot(h, w_ref[:, o:o + conv_width])
    u_ref[...] = cg * hv


def _in_call(x, g, sh, sc, w_in, cos_t, sin_t, tm=1024):
    bsz, seq, d = x.shape
    n_in = w_in.shape[1]
    cw = d - ATT_WIDTH
    vec = pl.BlockSpec((1, d), lambda b, i: (0, 0))
    bvec = pl.BlockSpec((None, 1, d), lambda b, i: (b, 0, 0))
    tab = pl.BlockSpec((tm, LANES), lambda b, i: (i, 0))

    def rows(width):
        return pl.BlockSpec((None, tm, width), lambda b, i: (b, i, 0))

    return pl.pallas_call(
        functools.partial(_in_kernel, conv_width=cw),
        grid=(bsz, seq // tm),
        in_specs=[rows(d), vec, bvec, bvec, pl.BlockSpec((d, n_in), lambda b, i: (0, 0)), tab, tab],
        out_specs=[rows(ATT_WIDTH), rows(KV_WIDTH), rows(KV_WIDTH), rows(cw), rows(cw)],
        out_shape=[jax.ShapeDtypeStruct((bsz, seq, ATT_WIDTH), F32),
                   jax.ShapeDtypeStruct((bsz, seq, KV_WIDTH), F32),
                   jax.ShapeDtypeStruct((bsz, seq, KV_WIDTH), F32),
                   jax.ShapeDtypeStruct((bsz, seq, cw), F32),
                   jax.ShapeDtypeStruct((bsz, seq, cw), F32)],
        compiler_params=_cparams(("arbitrary", "arbitrary")),
        name="in_proj",
    )(x, g, sh, sc, w_in, cos_t, sin_t)


ATTN_TILE = 32


def _attn_kernel(sink_ref, q_ref, k_ref, v_ref, kc_ref, vc_ref, o_ref,
                 s_scr, p_scr, bias_scr, m_scr, *, seq, tq):
    i = pl.program_id(1)
    nblk = tq // BLOCK
    span = 3 * BLOCK
    width = s_scr.shape[1]
    rows_all = GQA_GROUP * BLOCK
    tiles_per_head = BLOCK // ATTN_TILE
    kc = kc_ref[...]
    vc = vc_ref[...]
    for blk in range(nblk):
        n = i * nblk + blk
        ws = pl.multiple_of(jnp.clip((n - 1) * BLOCK, 0, seq - span), BLOCK)
        kw = k_ref[pl.ds(ws, span), :]
        vw = v_ref[pl.ds(ws, span), :]
        qb = q_ref[blk * BLOCK:(blk + 1) * BLOCK, :]
        qpos = n * BLOCK + lax.broadcasted_iota(jnp.int32, (BLOCK, span), 0)
        kpos = ws + lax.broadcasted_iota(jnp.int32, (BLOCK, span), 1)
        bias_scr[...] = jnp.where(jnp.abs(kpos - qpos) <= WINDOW, 0.0, NEG)
        outs = []
        for kh in range(N_KV_HEADS):
            heads = [kh * GQA_GROUP + g for g in range(GQA_GROUP)]
            qs = jnp.concatenate([qb[:, h * HEAD_DIM:(h + 1) * HEAD_DIM] for h in heads], axis=0)
            lo, hi = kh * HEAD_DIM, (kh + 1) * HEAD_DIM
            s_scr[:, 0:span] = _dot_nt(qs, kw[:, lo:hi])
            s_scr[:, span:width] = _dot_nt(qs, kc[:, lo:hi])

            for t in range(rows_all // ATTN_TILE):
                r0 = t * ATTN_TILE
                b0 = (t % tiles_per_head) * ATTN_TILE
                sk = sink_ref[kh * GQA_GROUP + t // tiles_per_head] * LOG2E
                s_loc = s_scr[pl.ds(r0, ATTN_TILE), 0:span] + bias_scr[pl.ds(b0, ATTN_TILE), :]
                s_ctx = s_scr[pl.ds(r0, ATTN_TILE), span:width]
                m = jnp.maximum(jnp.maximum(jnp.max(s_loc, axis=1, keepdims=True),
                                            jnp.max(s_ctx, axis=1, keepdims=True)), sk)
                m_scr[pl.ds(r0, ATTN_TILE), :] = m
                p_scr[pl.ds(r0, ATTN_TILE), 0:span] = jnp.exp2(s_loc - m)
                p_scr[pl.ds(r0, ATTN_TILE), span:width] = jnp.exp2(s_ctx - m)
            own = (lax.broadcasted_iota(jnp.int32, (1, KV_WIDTH), 1) // HEAD_DIM) == kh
            o_ext = (_dot(p_scr[:, 0:span], jnp.where(own, vw, 1.0))
                     + _dot(p_scr[:, span:width], jnp.where(own, vc, 1.0)))
            other = (1 - kh) * HEAD_DIM
            sk_col = jnp.concatenate([jnp.full((BLOCK, 1), sink_ref[h] * LOG2E, F32) for h in heads], axis=0)
            den = o_ext[:, other:other + 1] + jnp.exp2(sk_col - m_scr[...])
            o = o_ext[:, lo:hi] / den
            outs.extend(o[g * BLOCK:(g + 1) * BLOCK] for g in range(GQA_GROUP))
        o_ref[blk * BLOCK:(blk + 1) * BLOCK, :] = jnp.concatenate(outs, axis=1)


def _attn_call(sink, q, k, v, kc, vc, tq=512):
    bsz, seq, _ = q.shape
    n_ctx = kc.shape[1]
    span = 3 * BLOCK
    rows_all = GQA_GROUP * BLOCK
    full_kv = pl.BlockSpec((None, seq, KV_WIDTH), lambda b, i: (b, 0, 0))
    ctx_kv = pl.BlockSpec((None, n_ctx, KV_WIDTH), lambda b, i: (b, 0, 0))
    qspec = pl.BlockSpec((None, tq, ATT_WIDTH), lambda b, i: (b, i, 0))
    return pl.pallas_call(
        functools.partial(_attn_kernel, seq=seq, tq=tq),
        grid=(bsz, seq // tq),
        in_specs=[pl.BlockSpec(memory_space=pltpu.SMEM), qspec, full_kv, full_kv, ctx_kv, ctx_kv],
        out_specs=qspec,
        out_shape=jax.ShapeDtypeStruct((bsz, seq, ATT_WIDTH), F32),
        scratch_shapes=[pltpu.VMEM((rows_all, span + n_ctx), F32),
                        pltpu.VMEM((rows_all, span + n_ctx), F32),
                        pltpu.VMEM((BLOCK, span), F32),
                        pltpu.VMEM((rows_all, 1), F32)],
        compiler_params=_cparams(("arbitrary", "arbitrary")),
        name="attn",
    )(sink, q, k, v, kc, vc)


def _out_kernel(att_ref, u_ref, up_ref, un_ref, bg_ref, cw_ref, wo_ref, x_ref, gt_ref,
                g_ref, sh_ref, sc_ref, wr_ref, rows_ref, aff_ref, *, tm):
    i = pl.program_id(1)
    last = pl.num_programs(1) - 1
    u = u_ref[...]
    row = lax.broadcasted_iota(jnp.int32, u.shape, 0)
    prev_row = jnp.where(i > 0, up_ref[SUBLANES - 1:SUBLANES, :], 0.0)
    next_row = jnp.where(i < last, un_ref[0:1, :], 0.0)
    u_prev = jnp.where(row == 0, prev_row, pltpu.roll(u, 1, 0))
    u_next = jnp.where(row == tm - 1, next_row, pltpu.roll(u, tm - 1, 0))
    conv = bg_ref[...] * (cw_ref[0:1, :] * u_prev + cw_ref[1:2, :] * u + cw_ref[2:3, :] * u_next)
    y = _dot(att_ref[...], wo_ref[0:ATT_WIDTH, :]) + _dot(conv, wo_ref[ATT_WIDTH:, :])
    x1 = x_ref[...] + gt_ref[...] * y
    d = x1.shape[-1]
    h2 = _norm_mod(x1, g_ref[...], sh_ref[...], sc_ref[...])
    rows_ref[:, 0:d] = x1
    rows_ref[:, d:2 * d] = h2
    def split3(t):
        hi = t.astype(BF16)
        r1 = t - hi.astype(F32)
        mid = r1.astype(BF16)
        return hi, mid, (r1 - mid.astype(F32)).astype(BF16)

    def nt_bf16(a, b):
        return lax.dot_general(a, b, (((1,), (1,)), ((), ())), preferred_element_type=F32)

    h_hi, h_mid, h_lo = split3(h2)
    w_hi, w_mid, w_lo = split3(wr_ref[...])
    n_exp = w_hi.shape[0]
    by_hi = nt_bf16(jnp.concatenate([w_hi, w_mid, w_lo], axis=0), h_hi)
    by_mid = nt_bf16(jnp.concatenate([w_hi, w_mid], axis=0), h_mid)
    logits = (((by_hi[2 * n_exp:] + by_mid[n_exp:]) + nt_bf16(w_hi, h_lo))
              + (by_hi[n_exp:2 * n_exp] + by_mid[:n_exp])) + by_hi[:n_exp]
    mx = jnp.max(logits, axis=0, keepdims=True)
    ex = jnp.exp(logits - mx)
    aff_ref[...] = ex / jnp.sum(ex, axis=0, keepdims=True)


def _out_call(att, u, bg, conv_w, w_out, x, gt1, g_ffn, sh2, sc2, w_router_t, tm=1024):
    bsz, seq, d = x.shape
    cw = u.shape[-1]
    n_exp = w_router_t.shape[0]
    nh = seq // SUBLANES

    def rows(width):
        return pl.BlockSpec((None, tm, width), lambda b, i: (b, i, 0))

    halo_prev = pl.BlockSpec((None, SUBLANES, cw),
                             lambda b, i: (b, jnp.maximum(i * (tm // SUBLANES) - 1, 0), 0))
    halo_next = pl.BlockSpec((None, SUBLANES, cw),
                             lambda b, i: (b, jnp.minimum((i + 1) * (tm // SUBLANES), nh - 1), 0))
    vec = pl.BlockSpec((1, d), lambda b, i: (0, 0))
    bvec = pl.BlockSpec((None, 1, d), lambda b, i: (b, 0, 0))
    return pl.pallas_call(
        functools.partial(_out_kernel, tm=tm),
        grid=(bsz, seq // tm),
        in_specs=[rows(ATT_WIDTH), rows(cw), halo_prev, halo_next, rows(cw),
                  pl.BlockSpec((3, cw), lambda b, i: (0, 0)),
                  pl.BlockSpec((d, d), lambda b, i: (0, 0)),
                  rows(d), bvec, vec, bvec, bvec,
                  pl.BlockSpec((n_exp, d), lambda b, i: (0, 0))],
        out_specs=[rows(2 * d), pl.BlockSpec((None, n_exp, tm), lambda b, i: (b, 0, i))],
        out_shape=[jax.ShapeDtypeStruct((bsz, seq, 2 * d), F32),
                   jax.ShapeDtypeStruct((bsz, n_exp, seq), F32)],
        compiler_params=_cparams(("arbitrary", "arbitrary")),
        name="out_proj",
    )(att, u, u, u, bg, conv_w, w_out, x, gt1, g_ffn, sh2, sc2, w_router_t)


def _prefix_rows(m, tri, ones, low):
    n_exp, n_chunk, _ = m.shape
    mb = m.astype(BF16).reshape(n_exp * n_chunk, LANES)
    within = jnp.dot(mb, tri, preferred_element_type=F32).reshape(n_exp, n_chunk, LANES)
    tot = jnp.dot(mb, ones, preferred_element_type=F32).astype(BF16).reshape(n_exp, n_chunk, LANES)
    carry = jnp.stack([jnp.dot(low, tot[e], preferred_element_type=F32) for e in range(n_exp)], axis=0)
    return carry + within, carry


def _sel_kernel(aff_ref, pos_ref, carry_ref, *, cap):
    a = aff_ref[...]
    n_exp, n_chunk, _ = a.shape
    bits = lax.bitcast_convert_type(a, jnp.int32)

    def count(pred):
        c = jnp.sum(pred.astype(F32), axis=2, keepdims=True)
        return jnp.sum(c, axis=1, keepdims=True)

    def body(it, thr):
        cand = thr | jnp.left_shift(jnp.int32(1), 30 - it)
        return jnp.where(count(bits >= cand) >= cap, cand, thr)

    thr = lax.fori_loop(0, 31, body, jnp.zeros((n_exp, 1, 1), jnp.int32))
    gt = bits > thr
    eq = bits == thr
    need = cap - count(gt)

    r = lax.broadcasted_iota(jnp.int32, (LANES, LANES), 0)
    c = lax.broadcasted_iota(jnp.int32, (LANES, LANES), 1)
    tri = (r < c).astype(BF16)
    ones = jnp.ones((LANES, LANES), BF16)
    rr = lax.broadcasted_iota(jnp.int32, (n_chunk, n_chunk), 0)
    cc = lax.broadcasted_iota(jnp.int32, (n_chunk, n_chunk), 1)
    low = (cc < rr).astype(BF16)

    eq_rank, _ = _prefix_rows(eq, tri, ones, low)
    sel = gt | (eq & (eq_rank < need))
    pos, carry = _prefix_rows(sel, tri, ones, low)
    pos_ref[...] = jnp.where(sel, pos, -1.0)
    carry_ref[...] = carry


def _sel_call(aff4, cap):
    bsz, n_exp, n_chunk, _ = aff4.shape
    spec = pl.BlockSpec((None, n_exp, n_chunk, LANES), lambda b: (b, 0, 0, 0))
    return pl.pallas_call(
        functools.partial(_sel_kernel, cap=cap),
        grid=(bsz,),
        in_specs=[spec],
        out_specs=[spec, spec],
        out_shape=[jax.ShapeDtypeStruct(aff4.shape, F32)] * 2,
        compiler_params=_cparams(("arbitrary",)),
        name="select",
    )(aff4)


COMPACT_ROWS = 16
SLOT_BLOCK = LANES


def _compact_kernel(cum_ref, pos_ref, aff_ref, out_ref, *, n_exp, n_tchunk):
    b = pl.program_id(0)
    e = pl.program_id(1)
    base = (b * n_exp + e) * (n_tchunk + 1)
    out_ref[...] = jnp.zeros(out_ref.shape, F32)
    nt = (((1,), (1,)), ((), ()))
    rowi = lax.broadcasted_iota(jnp.int32, (COMPACT_ROWS, LANES), 0)
    lane = lax.broadcasted_iota(jnp.int32, (COMPACT_ROWS, LANES), 1).astype(F32)
    slot_b = lax.broadcasted_iota(jnp.int32, (SLOT_BLOCK, LANES), 0).astype(F32).astype(BF16)
    one_b = jnp.ones((), BF16)
    zero_b = jnp.zeros((), BF16)

    n_sblk = out_ref.shape[0] - 1

    def chunk_vals(c):
        a = aff_ref[c]
        g_hi = a.astype(BF16).astype(F32)
        r1 = a - g_hi
        g_mid = r1.astype(BF16).astype(F32)
        g_lo = r1 - g_mid
        pieces = []
        for r in range(SUBLANES):
            def bc(t):
                return jnp.broadcast_to(t[r:r + 1, :], (COMPACT_ROWS, LANES))

            piece = jnp.where(rowi == 0, float(c * SUBLANES + r),
                              jnp.where(rowi == 1, lane,
                                        jnp.where(rowi == 2, bc(g_hi),
                                                  jnp.where(rowi == 3, bc(g_mid),
                                                            jnp.where(rowi == 4, bc(g_lo), 0.0)))))
            pieces.append(piece)
        return jnp.concatenate(pieces, axis=1).astype(BF16)

    def visit(j, p, vals):
        rel = p - jnp.asarray(j * SLOT_BLOCK).astype(F32)
        rel = jnp.where((rel >= 0.0) & (rel < SLOT_BLOCK), rel, -1.0)
        pieces_oh = []
        for r in range(SUBLANES):
            row16 = jnp.broadcast_to(rel[r:r + 1, :], (2 * SUBLANES, LANES)).astype(BF16)
            blk = jnp.concatenate([row16] * (SLOT_BLOCK // (2 * SUBLANES)), axis=0)
            pieces_oh.append(jnp.where(blk == slot_b, one_b, zero_b))
        oh = jnp.concatenate(pieces_oh, axis=1)
        out_ref[j] += lax.dot_general(vals, oh, nt, preferred_element_type=F32)

    def first_block(c):
        return jnp.minimum(cum_ref[base + c] // SLOT_BLOCK, n_sblk - 1)

    for c in range(n_tchunk):
        p = pos_ref[c]
        vals = chunk_vals(c)
        j0 = first_block(c)
        visit(j0, p, vals)
        visit(j0 + 1, p, vals)

    for c in range(n_tchunk):
        lo = cum_ref[base + c]
        hi = cum_ref[base + c + 1]
        j_end = jnp.where(hi > lo, (hi - 1) // SLOT_BLOCK + 1, 0)

        def more(j, carry, c=c):
            visit(j, pos_ref[c], chunk_vals(c))
            return carry

        lax.fori_loop(first_block(c) + 2, j_end, more, 0)


def _compact_call(cum, pos5, aff5, cap):
    bsz, n_exp, n_tchunk, _, _ = pos5.shape
    n_sblk = cap // SLOT_BLOCK
    spec = pl.BlockSpec((None, None, n_tchunk, SUBLANES, LANES), lambda b, e, cum_ref: (b, e, 0, 0, 0))
    return pl.pallas_call(
        functools.partial(_compact_kernel, n_exp=n_exp, n_tchunk=n_tchunk),
        grid_spec=pltpu.PrefetchScalarGridSpec(
            num_scalar_prefetch=1,
            grid=(bsz, n_exp),
            in_specs=[spec, spec],
            out_specs=pl.BlockSpec((None, None, n_sblk + 1, COMPACT_ROWS, LANES),
                                   lambda b, e, cum_ref: (b, e, 0, 0, 0)),
        ),
        out_shape=jax.ShapeDtypeStruct((bsz, n_exp, n_sblk + 1, COMPACT_ROWS, LANES), F32),
        compiler_params=_cparams(("arbitrary", "arbitrary")),
        name="compact",
    )(cum, pos5, aff5)[:, :, :n_sblk]


FFN_ROWS = 256


FFN_COLS = 256


def _ffn_kernel(idx_ref, rows_in, g_ref, gt_ref, wg_ref, wu_ref, wd_ref, rows_hbm,
                buf0, buf1, wgb, wub, wdb, gsem, ssem, *, cap, n_exp, d):
    del rows_in
    e = pl.program_id(0)
    bufs = (buf0, buf1)
    n_chunk = cap // FFN_ROWS
    f = wgb.shape[1]
    acc_cols = pl.ds(0, d)

    def hbm_row(ee, b, s):
        return idx_ref[(b * n_exp + ee) * cap + s]

    def dma_thread(s):
        return s % 2 if isinstance(s, int) else 0

    def gather_start(ee, b, s):
        pltpu.make_async_copy(rows_hbm.at[pl.ds(hbm_row(ee, b, s), 1)], bufs[b].at[pl.ds(s, 1)],
                              gsem.at[b]).start(priority=dma_thread(s))

    def gather_wait(b):
        pltpu.make_async_copy(rows_hbm.at[pl.ds(0, cap)], bufs[b], gsem.at[b]).wait()

    def scatter_start(ee, b, s):
        pltpu.make_async_copy(bufs[b].at[pl.ds(s, 1), acc_cols],
                              rows_hbm.at[pl.ds(hbm_row(ee, b, s), 1), acc_cols],
                              ssem.at[b]).start(priority=dma_thread(s))

    def scatter_wait(b):
        pltpu.make_async_copy(bufs[b].at[:, acc_cols], rows_hbm.at[pl.ds(0, cap), acc_cols],
                              ssem.at[b]).wait()

    def compute_chunk(b, ci, todo):
        buf = bufs[b]
        rows = slice(ci * FFN_ROWS, (ci + 1) * FFN_ROWS)
        n_piece = 3 * (f // FFN_COLS)
        per_piece = -(-len(todo) // n_piece)

        def issue_some():
            for _ in range(min(per_piece, len(todo))):
                todo.pop(0)()

        xc = buf[rows, d:2 * d].astype(BF16)
        g_col = jnp.concatenate(
            [jnp.broadcast_to(g_ref[b, j:j + 1, :], (LANES, LANES)).T
             for j in range(ci * FFN_ROWS // LANES, (ci + 1) * FFN_ROWS // LANES)], axis=0)
        g_col = jnp.tile(g_col, (1, FFN_COLS // LANES))
        gt = gt_ref[b]
        hm = []
        for nj in range(f // FFN_COLS):
            cols = slice(nj * FFN_COLS, (nj + 1) * FFN_COLS)
            a = jnp.dot(xc, wgb[:, cols], preferred_element_type=F32)
            issue_some()
            u = jnp.dot(xc, wub[:, cols], preferred_element_type=F32)
            issue_some()
            hm.append((a * jax.nn.sigmoid(a) * u).astype(BF16))
        hm = jnp.concatenate(hm, axis=1)
        for nj in range(d // FFN_COLS):
            cols = slice(nj * FFN_COLS, (nj + 1) * FFN_COLS)
            y = jnp.dot(hm, wdb[:, cols], preferred_element_type=F32) * g_col
            buf[rows, cols] = buf[rows, cols] + gt[:, cols] * y
            issue_some()
        assert not todo

    def chunk_rows(ci):
        return range(ci * FFN_ROWS, (ci + 1) * FFN_ROWS)

    def split(n, parts):
        bounds = [n * k // parts for k in range(parts + 1)]
        return [range(bounds[k], bounds[k + 1]) for k in range(parts)]

    @pl.when(e == 0)
    def _():
        def first(s, carry):
            gather_start(0, 0, s)
            return carry
        lax.fori_loop(0, cap, first, 0)

    @pl.when(e > 0)
    def _():
        scatter_wait(1)

    wgb[...] = wg_ref[...].astype(BF16)
    wub[...] = wu_ref[...].astype(BF16)
    wdb[...] = wd_ref[...].astype(BF16)

    gather_wait(0)
    for ci in range(n_chunk):
        todo = [functools.partial(gather_start, e, 1, s) for s in chunk_rows(ci)]
        if ci > 0:
            todo += [functools.partial(scatter_start, e, 0, s) for s in chunk_rows(ci - 1)]
        compute_chunk(0, ci, todo)

    ne = jnp.minimum(e + 1, n_exp - 1)
    gather_wait(1)
    compute_chunk(1, 0, [functools.partial(scatter_start, e, 0, s) for s in chunk_rows(n_chunk - 1)])
    scatter_wait(0)
    for ci, part in zip(range(1, n_chunk), split(cap, n_chunk - 1)):
        todo = [functools.partial(gather_start, ne, 0, s) for s in part]
        todo += [functools.partial(scatter_start, e, 1, s) for s in chunk_rows(ci - 1)]
        compute_chunk(1, ci, todo)
    for s in chunk_rows(n_chunk - 1):
        scatter_start(e, 1, s)

    @pl.when(e == n_exp - 1)
    def _():
        scatter_wait(1)
        gather_wait(0)


def _ffn_call(idx, rows, gate, gt2, w_gate, w_up, w_down, cap):
    bsz, seq, d2 = rows.shape
    d = d2 // 2
    n_exp, _, f = w_gate.shape
    assert bsz == 2, "ffn double-buffers exactly two samples per expert"
    kern = functools.partial(_ffn_kernel, cap=cap, n_exp=n_exp, d=d)
    any_spec = pl.BlockSpec(memory_space=pl.ANY)
    out = pl.pallas_call(
        kern,
        grid_spec=pltpu.PrefetchScalarGridSpec(
            num_scalar_prefetch=1,
            grid=(n_exp,),
            in_specs=[any_spec,
                      pl.BlockSpec((bsz, None, cap // LANES, LANES), lambda e, idx_ref: (0, e, 0, 0)),
                      pl.BlockSpec((bsz, 1, d), lambda e, idx_ref: (0, 0, 0)),
                      pl.BlockSpec((None, d, f), lambda e, idx_ref: (e, 0, 0)),
                      pl.BlockSpec((None, d, f), lambda e, idx_ref: (e, 0, 0)),
                      pl.BlockSpec((None, f, d), lambda e, idx_ref: (e, 0, 0))],
            out_specs=any_spec,
            scratch_shapes=[pltpu.VMEM((cap, d2), F32), pltpu.VMEM((cap, d2), F32),
                            pltpu.VMEM((d, f), BF16), pltpu.VMEM((d, f), BF16), pltpu.VMEM((f, d), BF16),
                            pltpu.SemaphoreType.DMA((bsz,)), pltpu.SemaphoreType.DMA((bsz,))],
        ),
        out_shape=jax.ShapeDtypeStruct((bsz * seq, d2), F32),
        input_output_aliases={1: 0},
        compiler_params=_cparams(("arbitrary",)),
        name="ffn",
    )(idx, rows.reshape(bsz * seq, d2), gate, gt2, w_gate, w_up, w_down)
    return out.reshape(bsz, seq, d2)


def _final_kernel(x_ref, g_ref, o_ref):
    xf = x_ref[...]
    o_ref[...] = xf * lax.rsqrt(jnp.mean(xf * xf, axis=-1, keepdims=True) + EPS) * g_ref[...]


def _final_call(rows, g, tm=1024):
    bsz, seq, d2 = rows.shape
    d = d2 // 2
    return pl.pallas_call(
        _final_kernel,
        grid=(bsz, seq // tm),
        in_specs=[pl.BlockSpec((None, tm, d), lambda b, i: (b, i, 0)),
                  pl.BlockSpec((1, d), lambda b, i: (0, 0))],
        out_specs=pl.BlockSpec((None, tm, d), lambda b, i: (b, i, 0)),
        out_shape=jax.ShapeDtypeStruct((bsz, seq, d), F32),
        compiler_params=_cparams(("arbitrary", "arbitrary")),
        name="final",
    )(rows, g)


def _rope_tables(seq):
    t = np.arange(seq)
    n_freq = HEAD_DIM // 4
    inv = ROPE_THETA ** (-np.arange(n_freq, dtype=np.float64) / n_freq)
    ang = np.concatenate([(t // GRID_W)[:, None] * inv, (t % GRID_W)[:, None] * inv], axis=1)
    cos_h = np.concatenate([np.cos(ang), np.cos(ang)], axis=1)
    sin_h = np.concatenate([-np.sin(ang), np.sin(ang)], axis=1)
    reps = LANES // HEAD_DIM
    return (jnp.asarray(np.tile(cos_h, (1, reps)), F32), jnp.asarray(np.tile(sin_h, (1, reps)), F32))


def kernel(x, c, ctx, c_ctx, w_ada, b_ada, g_mix, w_in, conv_w, sink, w_out, g_ffn,
           w_router, w_gate, w_up, w_down, g_final):
    bsz, seq, d = x.shape
    n_exp = w_router.shape[-1]
    cap = EC_CAPACITY_FACTOR * seq // n_exp
    assert w_ada.shape[0] == 1, "single trunk layer"
    assert bsz + 1 <= SUBLANES and seq % (SUBLANES * LANES) == 0 and cap % SLOT_BLOCK == 0

    cin = jnp.zeros((SUBLANES, d), F32).at[:bsz].set(c).at[bsz].set(c_ctx)
    mod = _mod_call(cin, w_ada[0], b_ada[0][None, :])
    sh1, sc1, gt1, sh2, sc2, gt2 = [m[:bsz, None, :] for m in jnp.split(mod, 6, axis=-1)]
    csh1 = mod[bsz:bsz + 1, 0:d]
    csc1 = mod[bsz:bsz + 1, d:2 * d]

    w_in0 = w_in[0]
    g_mix_r = g_mix[0][None, :]
    kv0 = ATT_WIDTH
    k_c, v_c = _ctx_call(ctx, g_mix_r, csh1, csc1, w_in0[:, kv0:kv0 + 2 * KV_WIDTH])

    cos_t, sin_t = _rope_tables(seq)
    q, k, v, bg, u = _in_call(x, g_mix_r, sh1, sc1, w_in0, cos_t, sin_t)
    att = _attn_call(sink[0], q, k, v, k_c, v_c)
    rows, aff = _out_call(att, u, bg, conv_w[0], w_out[0], x, gt1,
                          g_ffn[0][None, :], sh2, sc2, w_router[0].T)

    n_chunk = seq // LANES
    aff4 = aff.reshape(bsz, n_exp, n_chunk, LANES)
    pos, carry = _sel_call(aff4, cap)
    n_tchunk = n_chunk // SUBLANES
    cum = jnp.concatenate([carry[:, :, ::SUBLANES, 0], jnp.full((bsz, n_exp, 1), cap, F32)], axis=-1)
    cum = cum.astype(jnp.int32).reshape(-1)
    comp = _compact_call(cum, pos.reshape(bsz, n_exp, n_tchunk, SUBLANES, LANES),
                         aff4.reshape(bsz, n_exp, n_tchunk, SUBLANES, LANES), cap)
    idx = (comp[:, :, :, 0, :] * LANES + comp[:, :, :, 1, :]).astype(jnp.int32)
    idx = (idx + (jnp.arange(bsz, dtype=jnp.int32) * seq)[:, None, None, None]).reshape(-1)
    gate = (comp[:, :, :, 2, :] + comp[:, :, :, 3, :]) + comp[:, :, :, 4, :]

    rows = _ffn_call(idx, rows, gate, gt2, w_gate[0], w_up[0], w_down[0], cap)
    return _final_call(rows, g_final[None, :])
```

```python
import functools

import numpy as np
import jax
import jax.numpy as jnp
from jax import lax
from jax.experimental import pallas as pl
from jax.experimental.pallas import tpu as pltpu

GRID_W = 64
N_HEADS = 8
N_KV_HEADS = 2
HEAD_DIM = 64
GQA_GROUP = N_HEADS // N_KV_HEADS
ATT_WIDTH = N_HEADS * HEAD_DIM
KV_WIDTH = N_KV_HEADS * HEAD_DIM
WINDOW = 128
BLOCK = 128
ROPE_THETA = 10000.0
N_EXPERTS = 16
EC_CAPACITY_FACTOR = 2
EPS = 1e-6

LANES = 128
SUBLANES = 8
MXU_COLS = 256
VMEM_LIMIT = 56 * 1024 * 1024

F32 = jnp.float32
BF16 = jnp.bfloat16
NEG = -1e30
LOG2E = 1.4426950408889634


def _cparams(sem):
    return pltpu.CompilerParams(dimension_semantics=sem, vmem_limit_bytes=VMEM_LIMIT)


def _dot(a, b):
    return jnp.dot(a, b, precision=lax.Precision.DEFAULT, preferred_element_type=F32)


def _dot_nt(a, b):
    return lax.dot_general(a, b, (((1,), (1,)), ((), ())), precision=lax.Precision.DEFAULT,
                           preferred_element_type=F32)


def _norm_mod(xf, g, shift, scale):
    r = xf * lax.rsqrt(jnp.mean(xf * xf, axis=-1, keepdims=True) + EPS)
    return (r * g) * (1.0 + scale) + shift


def _mod_kernel(c_ref, w_ref, b_ref, o_ref):
    c = c_ref[...]
    s = c * jax.nn.sigmoid(c)
    o_ref[...] = _dot(s, w_ref[...]) + b_ref[...]


def _mod_call(cin, w, bias, tn=1024):
    d, n = w.shape
    return pl.pallas_call(
        _mod_kernel,
        grid=(n // tn,),
        in_specs=[pl.BlockSpec((SUBLANES, d), lambda j: (0, 0)),
                  pl.BlockSpec((d, tn), lambda j: (0, j)),
                  pl.BlockSpec((1, tn), lambda j: (0, j))],
        out_specs=pl.BlockSpec((SUBLANES, tn), lambda j: (0, j)),
        out_shape=jax.ShapeDtypeStruct((SUBLANES, n), F32),
        compiler_params=_cparams(("arbitrary",)),
        name="mod",
    )(cin, w, bias)


def _ctx_kernel(x_ref, g_ref, sh_ref, sc_ref, w_ref, k_ref, v_ref):
    h = _norm_mod(x_ref[...], g_ref[...], sh_ref[...], sc_ref[...])
    kv = _dot(h, w_ref[...])
    k_ref[...] = kv[:, :KV_WIDTH]
    v_ref[...] = kv[:, KV_WIDTH:]


def _ctx_call(ctx, g, sh, sc, w_kv):
    bsz, n, d = ctx.shape
    vec = pl.BlockSpec((1, d), lambda b: (0, 0))
    kv_spec = pl.BlockSpec((None, n, KV_WIDTH), lambda b: (b, 0, 0))
    return pl.pallas_call(
        _ctx_kernel,
        grid=(bsz,),
        in_specs=[pl.BlockSpec((None, n, d), lambda b: (b, 0, 0)), vec, vec, vec,
                  pl.BlockSpec((d, 2 * KV_WIDTH), lambda b: (0, 0))],
        out_specs=[kv_spec, kv_spec],
        out_shape=[jax.ShapeDtypeStruct((bsz, n, KV_WIDTH), F32)] * 2,
        compiler_params=_cparams(("arbitrary",)),
        name="ctx_kv",
    )(ctx, g, sh, sc, w_kv)


def _rope(t, cos, sin_signed):
    lane = lax.broadcasted_iota(jnp.int32, t.shape, 1)
    first = (lane % HEAD_DIM) < (HEAD_DIM // 2)
    swapped = jnp.where(first, pltpu.roll(t, LANES - HEAD_DIM // 2, 1), pltpu.roll(t, HEAD_DIM // 2, 1))
    return t * cos + swapped * sin_signed


def _in_kernel(x_ref, g_ref, sh_ref, sc_ref, w_ref, cos_ref, sin_ref,
               q_ref, k_ref, v_ref, bg_ref, u_ref, *, conv_width):
    h = _norm_mod(x_ref[...], g_ref[...], sh_ref[...], sc_ref[...])
    cos = cos_ref[...]
    sin = sin_ref[...]
    scale = HEAD_DIM ** -0.5 * LOG2E
    for j in range(ATT_WIDTH // MXU_COLS):
        qj = _dot(h, w_ref[:, j * MXU_COLS:(j + 1) * MXU_COLS])
        for jj in range(MXU_COLS // LANES):
            c0 = j * MXU_COLS + jj * LANES
            q_ref[:, c0:c0 + LANES] = _rope(qj[:, jj * LANES:(jj + 1) * LANES], cos, sin) * scale
    o = ATT_WIDTH
    kv = _dot(h, w_ref[:, o:o + 2 * KV_WIDTH])
    k_ref[...] = _rope(kv[:, :KV_WIDTH], cos, sin)
    v_ref[...] = kv[:, KV_WIDTH:]
    o += 2 * KV_WIDTH
    bg_ref[...] = _dot(h, w_ref[:, o:o + conv_width])
    o += conv_width
    cg = _dot(h, w_ref[:, o:o + conv_width])
    o += conv_width
    hv = _dot(h, w_ref[:, o:o + conv_width])
    u_ref[...] = cg * hv


def _in_call(x, g, sh, sc, w_in, cos_t, sin_t, tm=1024):
    bsz, seq, d = x.shape
    n_in = w_in.shape[1]
    cw = d - ATT_WIDTH
    vec = pl.BlockSpec((1, d), lambda b, i: (0, 0))
    bvec = pl.BlockSpec((None, 1, d), lambda b, i: (b, 0, 0))
    tab = pl.BlockSpec((tm, LANES), lambda b, i: (i, 0))

    def rows(width):
        return pl.BlockSpec((None, tm, width), lambda b, i: (b, i, 0))

    return pl.pallas_call(
        functools.partial(_in_kernel, conv_width=cw),
        grid=(bsz, seq // tm),
        in_specs=[rows(d), vec, bvec, bvec, pl.BlockSpec((d, n_in), lambda b, i: (0, 0)), tab, tab],
        out_specs=[rows(ATT_WIDTH), rows(KV_WIDTH), rows(KV_WIDTH), rows(cw), rows(cw)],
        out_shape=[jax.ShapeDtypeStruct((bsz, seq, ATT_WIDTH), F32),
                   jax.ShapeDtypeStruct((bsz, seq, KV_WIDTH), F32),
                   jax.ShapeDtypeStruct((bsz, seq, KV_WIDTH), F32),
                   jax.ShapeDtypeStruct((bsz, seq, cw), F32),
                   jax.ShapeDtypeStruct((bsz, seq, cw), F32)],
        compiler_params=_cparams(("arbitrary", "arbitrary")),
        name="in_proj",
    )(x, g, sh, sc, w_in, cos_t, sin_t)


ATTN_TILE = 32


def _attn_kernel(sink_ref, q_ref, k_ref, v_ref, kc_ref, vc_ref, o_ref,
                 s_scr, p_scr, bias_scr, m_scr, *, seq, tq):
    i = pl.program_id(1)
    nblk = tq // BLOCK
    span = 3 * BLOCK
    width = s_scr.shape[1]
    rows_all = GQA_GROUP * BLOCK
    tiles_per_head = BLOCK // ATTN_TILE
    kc = kc_ref[...]
    vc = vc_ref[...]
    for blk in range(nblk):
        n = i * nblk + blk
        ws = pl.multiple_of(jnp.clip((n - 1) * BLOCK, 0, seq - span), BLOCK)
        kw = k_ref[pl.ds(ws, span), :]
        vw = v_ref[pl.ds(ws, span), :]
        qb = q_ref[blk * BLOCK:(blk + 1) * BLOCK, :]
        qpos = n * BLOCK + lax.broadcasted_iota(jnp.int32, (BLOCK, span), 0)
        kpos = ws + lax.broadcasted_iota(jnp.int32, (BLOCK, span), 1)
        bias_scr[...] = jnp.where(jnp.abs(kpos - qpos) <= WINDOW, 0.0, NEG)
        outs = []
        for kh in range(N_KV_HEADS):
            heads = [kh * GQA_GROUP + g for g in range(GQA_GROUP)]
            qs = jnp.concatenate([qb[:, h * HEAD_DIM:(h + 1) * HEAD_DIM] for h in heads], axis=0)
            lo, hi = kh * HEAD_DIM, (kh + 1) * HEAD_DIM
            s_scr[:, 0:span] = _dot_nt(qs, kw[:, lo:hi])
            s_scr[:, span:width] = _dot_nt(qs, kc[:, lo:hi])

            for t in range(rows_all // ATTN_TILE):
                r0 = t * ATTN_TILE
                b0 = (t % tiles_per_head) * ATTN_TILE
                sk = sink_ref[kh * GQA_GROUP + t // tiles_per_head] * LOG2E
                s_loc = s_scr[pl.ds(r0, ATTN_TILE), 0:span] + bias_scr[pl.ds(b0, ATTN_TILE), :]
                s_ctx = s_scr[pl.ds(r0, ATTN_TILE), span:width]
                m = jnp.maximum(jnp.maximum(jnp.max(s_loc, axis=1, keepdims=True),
                                            jnp.max(s_ctx, axis=1, keepdims=True)), sk)
                m_scr[pl.ds(r0, ATTN_TILE), :] = m
                p_scr[pl.ds(r0, ATTN_TILE), 0:span] = jnp.exp2(s_loc - m)
                p_scr[pl.ds(r0, ATTN_TILE), span:width] = jnp.exp2(s_ctx - m)
            own = (lax.broadcasted_iota(jnp.int32, (1, KV_WIDTH), 1) // HEAD_DIM) == kh
            o_ext = (_dot(p_scr[:, 0:span], jnp.where(own, vw, 1.0))
                     + _dot(p_scr[:, span:width], jnp.where(own, vc, 1.0)))
            other = (1 - kh) * HEAD_DIM
            sk_col = jnp.concatenate([jnp.full((BLOCK, 1), sink_ref[h] * LOG2E, F32) for h in heads], axis=0)
            den = o_ext[:, other:other + 1] + jnp.exp2(sk_col - m_scr[...])
            o = o_ext[:, lo:hi] / den
            outs.extend(o[g * BLOCK:(g + 1) * BLOCK] for g in range(GQA_GROUP))
        o_ref[blk * BLOCK:(blk + 1) * BLOCK, :] = jnp.concatenate(outs, axis=1)


def _attn_call(sink, q, k, v, kc, vc, tq=512):
    bsz, seq, _ = q.shape
    n_ctx = kc.shape[1]
    span = 3 * BLOCK
    rows_all = GQA_GROUP * BLOCK
    full_kv = pl.BlockSpec((None, seq, KV_WIDTH), lambda b, i: (b, 0, 0))
    ctx_kv = pl.BlockSpec((None, n_ctx, KV_WIDTH), lambda b, i: (b, 0, 0))
    qspec = pl.BlockSpec((None, tq, ATT_WIDTH), lambda b, i: (b, i, 0))
    return pl.pallas_call(
        functools.partial(_attn_kernel, seq=seq, tq=tq),
        grid=(bsz, seq // tq),
        in_specs=[pl.BlockSpec(memory_space=pltpu.SMEM), qspec, full_kv, full_kv, ctx_kv, ctx_kv],
        out_specs=qspec,
        out_shape=jax.ShapeDtypeStruct((bsz, seq, ATT_WIDTH), F32),
        scratch_shapes=[pltpu.VMEM((rows_all, span + n_ctx), F32),
                        pltpu.VMEM((rows_all, span + n_ctx), F32),
                        pltpu.VMEM((BLOCK, span), F32),
                        pltpu.VMEM((rows_all, 1), F32)],
        compiler_params=_cparams(("arbitrary", "arbitrary")),
        name="attn",
    )(sink, q, k, v, kc, vc)


def _out_kernel(att_ref, u_ref, up_ref, un_ref, bg_ref, cw_ref, wo_ref, x_ref, gt_ref,
                g_ref, sh_ref, sc_ref, wr_ref, rows_ref, aff_ref, *, tm):
    i = pl.program_id(1)
    last = pl.num_programs(1) - 1
    u = u_ref[...]
    row = lax.broadcasted_iota(jnp.int32, u.shape, 0)
    prev_row = jnp.where(i > 0, up_ref[SUBLANES - 1:SUBLANES, :], 0.0)
    next_row = jnp.where(i < last, un_ref[0:1, :], 0.0)
    u_prev = jnp.where(row == 0, prev_row, pltpu.roll(u, 1, 0))
    u_next = jnp.where(row == tm - 1, next_row, pltpu.roll(u, tm - 1, 0))
    conv = bg_ref[...] * (cw_ref[0:1, :] * u_prev + cw_ref[1:2, :] * u + cw_ref[2:3, :] * u_next)
    y = _dot(att_ref[...], wo_ref[0:ATT_WIDTH, :]) + _dot(conv, wo_ref[ATT_WIDTH:, :])
    x1 = x_ref[...] + gt_ref[...] * y
    d = x1.shape[-1]
    h2 = _norm_mod(x1, g_ref[...], sh_ref[...], sc_ref[...])
    rows_ref[:, 0:d] = x1
    rows_ref[:, d:2 * d] = h2
    def split3(t):
        hi = t.astype(BF16)
        r1 = t - hi.astype(F32)
        mid = r1.astype(BF16)
        return hi, mid, (r1 - mid.astype(F32)).astype(BF16)

    def nt_bf16(a, b):
        return lax.dot_general(a, b, (((1,), (1,)), ((), ())), preferred_element_type=F32)

    h_hi, h_mid, h_lo = split3(h2)
    w_hi, w_mid, w_lo = split3(wr_ref[...])
    n_exp = w_hi.shape[0]
    by_hi = nt_bf16(jnp.concatenate([w_hi, w_mid, w_lo], axis=0), h_hi)
    by_mid = nt_bf16(jnp.concatenate([w_hi, w_mid], axis=0), h_mid)
    logits = (((by_hi[2 * n_exp:] + by_mid[n_exp:]) + nt_bf16(w_hi, h_lo))
              + (by_hi[n_exp:2 * n_exp] + by_mid[:n_exp])) + by_hi[:n_exp]
    mx = jnp.max(logits, axis=0, keepdims=True)
    ex = jnp.exp(logits - mx)
    aff_ref[...] = ex / jnp.sum(ex, axis=0, keepdims=True)


def _out_call(att, u, bg, conv_w, w_out, x, gt1, g_ffn, sh2, sc2, w_router_t, tm=1024):
    bsz, seq, d = x.shape
    cw = u.shape[-1]
    n_exp = w_router_t.shape[0]
    nh = seq // SUBLANES

    def rows(width):
        return pl.BlockSpec((None, tm, width), lambda b, i: (b, i, 0))

    halo_prev = pl.BlockSpec((None, SUBLANES, cw),
                             lambda b, i: (b, jnp.maximum(i * (tm // SUBLANES) - 1, 0), 0))
    halo_next = pl.BlockSpec((None, SUBLANES, cw),
                             lambda b, i: (b, jnp.minimum((i + 1) * (tm // SUBLANES), nh - 1), 0))
    vec = pl.BlockSpec((1, d), lambda b, i: (0, 0))
    bvec = pl.BlockSpec((None, 1, d), lambda b, i: (b, 0, 0))
    return pl.pallas_call(
        functools.partial(_out_kernel, tm=tm),
        grid=(bsz, seq // tm),
        in_specs=[rows(ATT_WIDTH), rows(cw), halo_prev, halo_next, rows(cw),
                  pl.BlockSpec((3, cw), lambda b, i: (0, 0)),
                  pl.BlockSpec((d, d), lambda b, i: (0, 0)),
                  rows(d), bvec, vec, bvec, bvec,
                  pl.BlockSpec((n_exp, d), lambda b, i: (0, 0))],
        out_specs=[rows(2 * d), pl.BlockSpec((None, n_exp, tm), lambda b, i: (b, 0, i))],
        out_shape=[jax.ShapeDtypeStruct((bsz, seq, 2 * d), F32),
                   jax.ShapeDtypeStruct((bsz, n_exp, seq), F32)],
        compiler_params=_cparams(("arbitrary", "arbitrary")),
        name="out_proj",
    )(att, u, u, u, bg, conv_w, w_out, x, gt1, g_ffn, sh2, sc2, w_router_t)


def _prefix_rows(m, tri, ones, low):
    n_exp, n_chunk, _ = m.shape
    mb = m.astype(BF16).reshape(n_exp * n_chunk, LANES)
    within = jnp.dot(mb, tri, preferred_element_type=F32).reshape(n_exp, n_chunk, LANES)
    tot = jnp.dot(mb, ones, preferred_element_type=F32).astype(BF16).reshape(n_exp, n_chunk, LANES)
    carry = jnp.stack([jnp.dot(low, tot[e], preferred_element_type=F32) for e in range(n_exp)], axis=0)
    return carry + within, carry


def _sel_kernel(aff_ref, pos_ref, carry_ref, *, cap):
    a = aff_ref[...]
    n_exp, n_chunk, _ = a.shape
    bits = lax.bitcast_convert_type(a, jnp.int32)

    def count(pred):
        c = jnp.sum(pred.astype(F32), axis=2, keepdims=True)
        return jnp.sum(c, axis=1, keepdims=True)

    def body(it, thr):
        cand = thr | jnp.left_shift(jnp.int32(1), 30 - it)
        return jnp.where(count(bits >= cand) >= cap, cand, thr)

    thr = lax.fori_loop(0, 31, body, jnp.zeros((n_exp, 1, 1), jnp.int32))
    gt = bits > thr
    eq = bits == thr
    need = cap - count(gt)

    r = lax.broadcasted_iota(jnp.int32, (LANES, LANES), 0)
    c = lax.broadcasted_iota(jnp.int32, (LANES, LANES), 1)
    tri = (r < c).astype(BF16)
    ones = jnp.ones((LANES, LANES), BF16)
    rr = lax.broadcasted_iota(jnp.int32, (n_chunk, n_chunk), 0)
    cc = lax.broadcasted_iota(jnp.int32, (n_chunk, n_chunk), 1)
    low = (cc < rr).astype(BF16)

    eq_rank, _ = _prefix_rows(eq, tri, ones, low)
    sel = gt | (eq & (eq_rank < need))
    pos, carry = _prefix_rows(sel, tri, ones, low)
    pos_ref[...] = jnp.where(sel, pos, -1.0)
    carry_ref[...] = carry


def _sel_call(aff4, cap):
    bsz, n_exp, n_chunk, _ = aff4.shape
    spec = pl.BlockSpec((None, n_exp, n_chunk, LANES), lambda b: (b, 0, 0, 0))
    return pl.pallas_call(
        functools.partial(_sel_kernel, cap=cap),
        grid=(bsz,),
        in_specs=[spec],
        out_specs=[spec, spec],
        out_shape=[jax.ShapeDtypeStruct(aff4.shape, F32)] * 2,
        compiler_params=_cparams(("arbitrary",)),
        name="select",
    )(aff4)


COMPACT_ROWS = 16
SLOT_BLOCK = LANES


def _compact_kernel(cum_ref, pos_ref, aff_ref, out_ref, *, n_exp, n_tchunk):
    b = pl.program_id(0)
    e = pl.program_id(1)
    base = (b * n_exp + e) * (n_tchunk + 1)
    out_ref[...] = jnp.zeros(out_ref.shape, F32)
    nt = (((1,), (1,)), ((), ()))
    rowi = lax.broadcasted_iota(jnp.int32, (COMPACT_ROWS, LANES), 0)
    lane = lax.broadcasted_iota(jnp.int32, (COMPACT_ROWS, LANES), 1).astype(F32)
    slot_b = lax.broadcasted_iota(jnp.int32, (SLOT_BLOCK, LANES), 0).astype(F32).astype(BF16)
    one_b = jnp.ones((), BF16)
    zero_b = jnp.zeros((), BF16)

    n_sblk = out_ref.shape[0] - 1

    def chunk_vals(c):
        a = aff_ref[c]
        g_hi = a.astype(BF16).astype(F32)
        r1 = a - g_hi
        g_mid = r1.astype(BF16).astype(F32)
        g_lo = r1 - g_mid
        pieces = []
        for r in range(SUBLANES):
            def bc(t):
                return jnp.broadcast_to(t[r:r + 1, :], (COMPACT_ROWS, LANES))

            piece = jnp.where(rowi == 0, float(c * SUBLANES + r),
                              jnp.where(rowi == 1, lane,
                                        jnp.where(rowi == 2, bc(g_hi),
                                                  jnp.where(rowi == 3, bc(g_mid),
                                                            jnp.where(rowi == 4, bc(g_lo), 0.0)))))
            pieces.append(piece)
        return jnp.concatenate(pieces, axis=1).astype(BF16)

    def visit(j, p, vals):
        rel = p - jnp.asarray(j * SLOT_BLOCK).astype(F32)
        rel = jnp.where((rel >= 0.0) & (rel < SLOT_BLOCK), rel, -1.0)
        pieces_oh = []
        for r in range(SUBLANES):
            row16 = jnp.broadcast_to(rel[r:r + 1, :], (2 * SUBLANES, LANES)).astype(BF16)
            blk = jnp.concatenate([row16] * (SLOT_BLOCK // (2 * SUBLANES)), axis=0)
            pieces_oh.append(jnp.where(blk == slot_b, one_b, zero_b))
        oh = jnp.concatenate(pieces_oh, axis=1)
        out_ref[j] += lax.dot_general(vals, oh, nt, preferred_element_type=F32)

    def first_block(c):
        return jnp.minimum(cum_ref[base + c] // SLOT_BLOCK, n_sblk - 1)

    for c in range(n_tchunk):
        p = pos_ref[c]
        vals = chunk_vals(c)
        j0 = first_block(c)
        visit(j0, p, vals)
        visit(j0 + 1, p, vals)

    for c in range(n_tchunk):
        lo = cum_ref[base + c]
        hi = cum_ref[base + c + 1]
        j_end = jnp.where(hi > lo, (hi - 1) // SLOT_BLOCK + 1, 0)

        def more(j, carry, c=c):
            visit(j, pos_ref[c], chunk_vals(c))
            return carry

        lax.fori_loop(first_block(c) + 2, j_end, more, 0)


def _compact_call(cum, pos5, aff5, cap):
    bsz, n_exp, n_tchunk, _, _ = pos5.shape
    n_sblk = cap // SLOT_BLOCK
    spec = pl.BlockSpec((None, None, n_tchunk, SUBLANES, LANES), lambda b, e, cum_ref: (b, e, 0, 0, 0))
    return pl.pallas_call(
        functools.partial(_compact_kernel, n_exp=n_exp, n_tchunk=n_tchunk),
        grid_spec=pltpu.PrefetchScalarGridSpec(
            num_scalar_prefetch=1,
            grid=(bsz, n_exp),
            in_specs=[spec, spec],
            out_specs=pl.BlockSpec((None, None, n_sblk + 1, COMPACT_ROWS, LANES),
                                   lambda b, e, cum_ref: (b, e, 0, 0, 0)),
        ),
        out_shape=jax.ShapeDtypeStruct((bsz, n_exp, n_sblk + 1, COMPACT_ROWS, LANES), F32),
        compiler_params=_cparams(("arbitrary", "arbitrary")),
        name="compact",
    )(cum, pos5, aff5)[:, :, :n_sblk]


FFN_ROWS = 256


FFN_COLS = 256


def _ffn_kernel(idx_ref, rows_in, g_ref, gt_ref, wg_ref, wu_ref, wd_ref, rows_hbm,
                buf0, buf1, wgb, wub, wdb, gsem, ssem, *, cap, n_exp, d):
    del rows_in
    e = pl.program_id(0)
    bufs = (buf0, buf1)
    n_chunk = cap // FFN_ROWS
    f = wgb.shape[1]
    acc_cols = pl.ds(0, d)

    def hbm_row(ee, b, s):
        return idx_ref[(b * n_exp + ee) * cap + s]

    def dma_thread(s):
        return s % 2 if isinstance(s, int) else 0

    def gather_start(ee, b, s):
        pltpu.make_async_copy(rows_hbm.at[pl.ds(hbm_row(ee, b, s), 1)], bufs[b].at[pl.ds(s, 1)],
                              gsem.at[b]).start(priority=dma_thread(s))

    def gather_wait(b):
        pltpu.make_async_copy(rows_hbm.at[pl.ds(0, cap)], bufs[b], gsem.at[b]).wait()

    def scatter_start(ee, b, s):
        pltpu.make_async_copy(bufs[b].at[pl.ds(s, 1), acc_cols],
                              rows_hbm.at[pl.ds(hbm_row(ee, b, s), 1), acc_cols],
                              ssem.at[b]).start(priority=dma_thread(s))

    def scatter_wait(b):
        pltpu.make_async_copy(bufs[b].at[:, acc_cols], rows_hbm.at[pl.ds(0, cap), acc_cols],
                              ssem.at[b]).wait()

    def compute_chunk(b, ci, todo):
        buf = bufs[b]
        rows = slice(ci * FFN_ROWS, (ci + 1) * FFN_ROWS)
        n_piece = 3 * (f // FFN_COLS)
        per_piece = -(-len(todo) // n_piece)

        def issue_some():
            for _ in range(min(per_piece, len(todo))):
                todo.pop(0)()

        xc = buf[rows, d:2 * d].astype(BF16)
        g_col = jnp.concatenate(
            [jnp.broadcast_to(g_ref[b, j:j + 1, :], (LANES, LANES)).T
             for j in range(ci * FFN_ROWS // LANES, (ci + 1) * FFN_ROWS // LANES)], axis=0)
        g_col = jnp.tile(g_col, (1, FFN_COLS // LANES))
        gt = gt_ref[b]
        hm = []
        for nj in range(f // FFN_COLS):
            cols = slice(nj * FFN_COLS, (nj + 1) * FFN_COLS)
            a = jnp.dot(xc, wgb[:, cols], preferred_element_type=F32)
            issue_some()
            u = jnp.dot(xc, wub[:, cols], preferred_element_type=F32)
            issue_some()
            hm.append((a * jax.nn.sigmoid(a) * u).astype(BF16))
        hm = jnp.concatenate(hm, axis=1)
        for nj in range(d // FFN_COLS):
            cols = slice(nj * FFN_COLS, (nj + 1) * FFN_COLS)
            y = jnp.dot(hm, wdb[:, cols], preferred_element_type=F32) * g_col
            buf[rows, cols] = buf[rows, cols] + gt[:, cols] * y
            issue_some()
        assert not todo

    def chunk_rows(ci):
        return range(ci * FFN_ROWS, (ci + 1) * FFN_ROWS)

    def split(n, parts):
        bounds = [n * k // parts for k in range(parts + 1)]
        return [range(bounds[k], bounds[k + 1]) for k in range(parts)]

    def cast_weights(todo):
        pieces = [(wgb, wg_ref), (wub, wu_ref), (wdb, wd_ref)]
        n_piece = len(pieces) * (f // FFN_COLS)
        per_piece = -(-len(todo) // n_piece)
        for dst, src in pieces:
            for nj in range(f // FFN_COLS):
                cols = slice(nj * FFN_COLS, (nj + 1) * FFN_COLS)
                dst[:, cols] = src[:, cols].astype(BF16)
                for _ in range(min(per_piece, len(todo))):
                    todo.pop(0)()
        assert not todo

    @pl.when(e == 0)
    def _():
        def first(s, carry):
            gather_start(0, 0, s)
            return carry
        lax.fori_loop(0, cap, first, 0)
        cast_weights([])

    @pl.when(e > 0)
    def _():
        cast_weights([functools.partial(scatter_start, e - 1, 1, s) for s in chunk_rows(n_chunk - 1)])
        scatter_wait(1)

    gather_wait(0)
    for ci in range(n_chunk):
        todo = [functools.partial(gather_start, e, 1, s) for s in chunk_rows(ci)]
        if ci > 0:
            todo += [functools.partial(scatter_start, e, 0, s) for s in chunk_rows(ci - 1)]
        compute_chunk(0, ci, todo)

    ne = jnp.minimum(e + 1, n_exp - 1)
    gather_wait(1)
    compute_chunk(1, 0, [functools.partial(scatter_start, e, 0, s) for s in chunk_rows(n_chunk - 1)])
    scatter_wait(0)
    for ci, part in zip(range(1, n_chunk), split(cap, n_chunk - 1)):
        todo = [functools.partial(gather_start, ne, 0, s) for s in part]
        todo += [functools.partial(scatter_start, e, 1, s) for s in chunk_rows(ci - 1)]
        compute_chunk(1, ci, todo)

    @pl.when(e == n_exp - 1)
    def _():
        for s in chunk_rows(n_chunk - 1):
            scatter_start(e, 1, s)
        scatter_wait(1)
        gather_wait(0)


def _ffn_call(idx, rows, gate, gt2, w_gate, w_up, w_down, cap):
    bsz, seq, d2 = rows.shape
    d = d2 // 2
    n_exp, _, f = w_gate.shape
    assert bsz == 2, "ffn double-buffers exactly two samples per expert"
    kern = functools.partial(_ffn_kernel, cap=cap, n_exp=n_exp, d=d)
    any_spec = pl.BlockSpec(memory_space=pl.ANY)
    out = pl.pallas_call(
        kern,
        grid_spec=pltpu.PrefetchScalarGridSpec(
            num_scalar_prefetch=1,
            grid=(n_exp,),
            in_specs=[any_spec,
                      pl.BlockSpec((bsz, None, cap // LANES, LANES), lambda e, idx_ref: (0, e, 0, 0)),
                      pl.BlockSpec((bsz, 1, d), lambda e, idx_ref: (0, 0, 0)),
                      pl.BlockSpec((None, d, f), lambda e, idx_ref: (e, 0, 0)),
                      pl.BlockSpec((None, d, f), lambda e, idx_ref: (e, 0, 0)),
                      pl.BlockSpec((None, f, d), lambda e, idx_ref: (e, 0, 0))],
            out_specs=any_spec,
            scratch_shapes=[pltpu.VMEM((cap, d2), F32), pltpu.VMEM((cap, d2), F32),
                            pltpu.VMEM((d, f), BF16), pltpu.VMEM((d, f), BF16), pltpu.VMEM((f, d), BF16),
                            pltpu.SemaphoreType.DMA((bsz,)), pltpu.SemaphoreType.DMA((bsz,))],
        ),
        out_shape=jax.ShapeDtypeStruct((bsz * seq, d2), F32),
        input_output_aliases={1: 0},
        compiler_params=_cparams(("arbitrary",)),
        name="ffn",
    )(idx, rows.reshape(bsz * seq, d2), gate, gt2, w_gate, w_up, w_down)
    return out.reshape(bsz, seq, d2)


def _final_kernel(x_ref, g_ref, o_ref):
    xf = x_ref[...]
    o_ref[...] = xf * lax.rsqrt(jnp.mean(xf * xf, axis=-1, keepdims=True) + EPS) * g_ref[...]


def _final_call(rows, g, tm=1024):
    bsz, seq, d2 = rows.shape
    d = d2 // 2
    return pl.pallas_call(
        _final_kernel,
        grid=(bsz, seq // tm),
        in_specs=[pl.BlockSpec((None, tm, d), lambda b, i: (b, i, 0)),
                  pl.BlockSpec((1, d), lambda b, i: (0, 0))],
        out_specs=pl.BlockSpec((None, tm, d), lambda b, i: (b, i, 0)),
        out_shape=jax.ShapeDtypeStruct((bsz, seq, d), F32),
        compiler_params=_cparams(("arbitrary", "arbitrary")),
        name="final",
    )(rows, g)


def _rope_tables(seq):
    t = np.arange(seq)
    n_freq = HEAD_DIM // 4
    inv = ROPE_THETA ** (-np.arange(n_freq, dtype=np.float64) / n_freq)
    ang = np.concatenate([(t // GRID_W)[:, None] * inv, (t % GRID_W)[:, None] * inv], axis=1)
    cos_h = np.concatenate([np.cos(ang), np.cos(ang)], axis=1)
    sin_h = np.concatenate([-np.sin(ang), np.sin(ang)], axis=1)
    reps = LANES // HEAD_DIM
    return (jnp.asarray(np.tile(cos_h, (1, reps)), F32), jnp.asarray(np.tile(sin_h, (1, reps)), F32))


def kernel(x, c, ctx, c_ctx, w_ada, b_ada, g_mix, w_in, conv_w, sink, w_out, g_ffn,
           w_router, w_gate, w_up, w_down, g_final):
    bsz, seq, d = x.shape
    n_exp = w_router.shape[-1]
    cap = EC_CAPACITY_FACTOR * seq // n_exp
    assert w_ada.shape[0] == 1, "single trunk layer"
    assert bsz + 1 <= SUBLANES and seq % (SUBLANES * LANES) == 0 and cap % SLOT_BLOCK == 0

    cin = jnp.zeros((SUBLANES, d), F32).at[:bsz].set(c).at[bsz].set(c_ctx)
    mod = _mod_call(cin, w_ada[0], b_ada[0][None, :])
    sh1, sc1, gt1, sh2, sc2, gt2 = [m[:bsz, None, :] for m in jnp.split(mod, 6, axis=-1)]
    csh1 = mod[bsz:bsz + 1, 0:d]
    csc1 = mod[bsz:bsz + 1, d:2 * d]

    w_in0 = w_in[0]
    g_mix_r = g_mix[0][None, :]
    kv0 = ATT_WIDTH
    k_c, v_c = _ctx_call(ctx, g_mix_r, csh1, csc1, w_in0[:, kv0:kv0 + 2 * KV_WIDTH])

    cos_t, sin_t = _rope_tables(seq)
    q, k, v, bg, u = _in_call(x, g_mix_r, sh1, sc1, w_in0, cos_t, sin_t)
    att = _attn_call(sink[0], q, k, v, k_c, v_c)
    rows, aff = _out_call(att, u, bg, conv_w[0], w_out[0], x, gt1,
                          g_ffn[0][None, :], sh2, sc2, w_router[0].T)

    n_chunk = seq // LANES
    aff4 = aff.reshape(bsz, n_exp, n_chunk, LANES)
    pos, carry = _sel_call(aff4, cap)
    n_tchunk = n_chunk // SUBLANES
    cum = jnp.concatenate([carry[:, :, ::SUBLANES, 0], jnp.full((bsz, n_exp, 1), cap, F32)], axis=-1)
    cum = cum.astype(jnp.int32).reshape(-1)
    comp = _compact_call(cum, pos.reshape(bsz, n_exp, n_tchunk, SUBLANES, LANES),
                         aff4.reshape(bsz, n_exp, n_tchunk, SUBLANES, LANES), cap)
    idx = (comp[:, :, :, 0, :] * LANES + comp[:, :, :, 1, :]).astype(jnp.int32)
    idx = (idx + (jnp.arange(bsz, dtype=jnp.int32) * seq)[:, None, None, None]).reshape(-1)
    gate = (comp[:, :, :, 2, :] + comp[:, :, :, 3, :]) + comp[:, :, :, 4, :]

    rows = _ffn_call(idx, rows, gate, gt2, w_gate[0], w_up[0], w_down[0], cap)
    return _final_call(rows, g_final[None, :])
```

```python
import functools

import numpy as np
import jax
import jax.numpy as jnp
from jax import lax
from jax.experimental import pallas as pl
from jax.experimental.pallas import tpu as pltpu

GRID_W = 64
N_HEADS = 8
N_KV_HEADS = 2
HEAD_DIM = 64
GQA_GROUP = N_HEADS // N_KV_HEADS
ATT_WIDTH = N_HEADS * HEAD_DIM
KV_WIDTH = N_KV_HEADS * HEAD_DIM
WINDOW = 128
BLOCK = 128
ROPE_THETA = 10000.0
N_EXPERTS = 16
EC_CAPACITY_FACTOR = 2
EPS = 1e-6

LANES = 128
SUBLANES = 8
MXU_COLS = 256
VMEM_LIMIT = 56 * 1024 * 1024

F32 = jnp.float32
BF16 = jnp.bfloat16
NEG = -1e30
LOG2E = 1.4426950408889634


def _cparams(sem):
    return pltpu.CompilerParams(dimension_semantics=sem, vmem_limit_bytes=VMEM_LIMIT)


def _dot(a, b):
    return jnp.dot(a, b, precision=lax.Precision.DEFAULT, preferred_element_type=F32)


def _dot_nt(a, b):
    return lax.dot_general(a, b, (((1,), (1,)), ((), ())), precision=lax.Precision.DEFAULT,
                           preferred_element_type=F32)


def _norm_mod(xf, g, shift, scale):
    r = xf * lax.rsqrt(jnp.mean(xf * xf, axis=-1, keepdims=True) + EPS)
    return (r * g) * (1.0 + scale) + shift


def _mod_kernel(c_ref, w_ref, b_ref, o_ref):
    c = c_ref[...]
    s = c * jax.nn.sigmoid(c)
    o_ref[...] = _dot(s, w_ref[...]) + b_ref[...]


def _mod_call(cin, w, bias, tn=1024):
    d, n = w.shape
    return pl.pallas_call(
        _mod_kernel,
        grid=(n // tn,),
        in_specs=[pl.BlockSpec((SUBLANES, d), lambda j: (0, 0)),
                  pl.BlockSpec((d, tn), lambda j: (0, j)),
                  pl.BlockSpec((1, tn), lambda j: (0, j))],
        out_specs=pl.BlockSpec((SUBLANES, tn), lambda j: (0, j)),
        out_shape=jax.ShapeDtypeStruct((SUBLANES, n), F32),
        compiler_params=_cparams(("arbitrary",)),
        name="mod",
    )(cin, w, bias)


def _ctx_kernel(x_ref, g_ref, sh_ref, sc_ref, w_ref, k_ref, v_ref):
    h = _norm_mod(x_ref[...], g_ref[...], sh_ref[...], sc_ref[...])
    kv = _dot(h, w_ref[...])
    k_ref[...] = kv[:, :KV_WIDTH]
    v_ref[...] = kv[:, KV_WIDTH:]


def _ctx_call(ctx, g, sh, sc, w_kv):
    bsz, n, d = ctx.shape
    vec = pl.BlockSpec((1, d), lambda b: (0, 0))
    kv_spec = pl.BlockSpec((None, n, KV_WIDTH), lambda b: (b, 0, 0))
    return pl.pallas_call(
        _ctx_kernel,
        grid=(bsz,),
        in_specs=[pl.BlockSpec((None, n, d), lambda b: (b, 0, 0)), vec, vec, vec,
                  pl.BlockSpec((d, 2 * KV_WIDTH), lambda b: (0, 0))],
        out_specs=[kv_spec, kv_spec],
        out_shape=[jax.ShapeDtypeStruct((bsz, n, KV_WIDTH), F32)] * 2,
        compiler_params=_cparams(("arbitrary",)),
        name="ctx_kv",
    )(ctx, g, sh, sc, w_kv)


def _rope(t, cos, sin_signed):
    lane = lax.broadcasted_iota(jnp.int32, t.shape, 1)
    first = (lane % HEAD_DIM) < (HEAD_DIM // 2)
    swapped = jnp.where(first, pltpu.roll(t, LANES - HEAD_DIM // 2, 1), pltpu.roll(t, HEAD_DIM // 2, 1))
    return t * cos + swapped * sin_signed


def _in_kernel(x_ref, g_ref, sh_ref, sc_ref, w_ref, cos_ref, sin_ref,
               q_ref, k_ref, v_ref, bg_ref, u_ref, *, conv_width):
    h = _norm_mod(x_ref[...], g_ref[...], sh_ref[...], sc_ref[...])
    cos = cos_ref[...]
    sin = sin_ref[...]
    scale = HEAD_DIM ** -0.5 * LOG2E
    for j in range(ATT_WIDTH // MXU_COLS):
        qj = _dot(h, w_ref[:, j * MXU_COLS:(j + 1) * MXU_COLS])
        for jj in range(MXU_COLS // LANES):
            c0 = j * MXU_COLS + jj * LANES
            q_ref[:, c0:c0 + LANES] = _rope(qj[:, jj * LANES:(jj + 1) * LANES], cos, sin) * scale
    o = ATT_WIDTH
    kv = _dot(h, w_ref[:, o:o + 2 * KV_WIDTH])
    k_ref[...] = _rope(kv[:, :KV_WIDTH], cos, sin)
    v_ref[...] = kv[:, KV_WIDTH:]
    o += 2 * KV_WIDTH
    bg_ref[...] = _dot(h, w_ref[:, o:o + conv_width])
    o += conv_width
    cg = _dot(h, w_ref[:, o:o + conv_width])
    o += conv_width
    hv = _dot(h, w_ref[:, o:o + conv_width])
    u_ref[...] = cg * hv


def _in_call(x, g, sh, sc, w_in, cos_t, sin_t, tm=1024):
    bsz, seq, d = x.shape
    n_in = w_in.shape[1]
    cw = d - ATT_WIDTH
    vec = pl.BlockSpec((1, d), lambda b, i: (0, 0))
    bvec = pl.BlockSpec((None, 1, d), lambda b, i: (b, 0, 0))
    tab = pl.BlockSpec((tm, LANES), lambda b, i: (i, 0))

    def rows(width):
        return pl.BlockSpec((None, tm, width), lambda b, i: (b, i, 0))

    return pl.pallas_call(
        functools.partial(_in_kernel, conv_width=cw),
        grid=(bsz, seq // tm),
        in_specs=[rows(d), vec, bvec, bvec, pl.BlockSpec((d, n_in), lambda b, i: (0, 0)), tab, tab],
        out_specs=[rows(ATT_WIDTH), rows(KV_WIDTH), rows(KV_WIDTH), rows(cw), rows(cw)],
        out_shape=[jax.ShapeDtypeStruct((bsz, seq, ATT_WIDTH), F32),
                   jax.ShapeDtypeStruct((bsz, seq, KV_WIDTH), F32),
                   jax.ShapeDtypeStruct((bsz, seq, KV_WIDTH), F32),
                   jax.ShapeDtypeStruct((bsz, seq, cw), F32),
                   jax.ShapeDtypeStruct((bsz, seq, cw), F32)],
        compiler_params=_cparams(("arbitrary", "arbitrary")),
        name="in_proj",
    )(x, g, sh, sc, w_in, cos_t, sin_t)


ATTN_TILE = 32


def _attn_kernel(sink_ref, q_ref, k_ref, v_ref, kc_ref, vc_ref, o_ref,
                 s_scr, p_scr, bias_scr, m_scr, *, seq, tq):
    i = pl.program_id(1)
    nblk = tq // BLOCK
    span = 3 * BLOCK
    width = s_scr.shape[1]
    rows_all = GQA_GROUP * BLOCK
    tiles_per_head = BLOCK // ATTN_TILE
    kc = kc_ref[...]
    vc = vc_ref[...]
    for blk in range(nblk):
        n = i * nblk + blk
        ws = pl.multiple_of(jnp.clip((n - 1) * BLOCK, 0, seq - span), BLOCK)
        kw = k_ref[pl.ds(ws, span), :]
        vw = v_ref[pl.ds(ws, span), :]
        qb = q_ref[blk * BLOCK:(blk + 1) * BLOCK, :]
        qpos = n * BLOCK + lax.broadcasted_iota(jnp.int32, (BLOCK, span), 0)
        kpos = ws + lax.broadcasted_iota(jnp.int32, (BLOCK, span), 1)
        bias_scr[...] = jnp.where(jnp.abs(kpos - qpos) <= WINDOW, 0.0, NEG)
        outs = []
        for kh in range(N_KV_HEADS):
            heads = [kh * GQA_GROUP + g for g in range(GQA_GROUP)]
            qs = jnp.concatenate([qb[:, h * HEAD_DIM:(h + 1) * HEAD_DIM] for h in heads], axis=0)
            lo, hi = kh * HEAD_DIM, (kh + 1) * HEAD_DIM
            s_scr[:, 0:span] = _dot_nt(qs, kw[:, lo:hi])
            s_scr[:, span:width] = _dot_nt(qs, kc[:, lo:hi])

            for t in range(rows_all // ATTN_TILE):
                r0 = t * ATTN_TILE
                b0 = (t % tiles_per_head) * ATTN_TILE
                sk = sink_ref[kh * GQA_GROUP + t // tiles_per_head] * LOG2E
                s_loc = s_scr[pl.ds(r0, ATTN_TILE), 0:span] + bias_scr[pl.ds(b0, ATTN_TILE), :]
                s_ctx = s_scr[pl.ds(r0, ATTN_TILE), span:width]
                m = jnp.maximum(jnp.maximum(jnp.max(s_loc, axis=1, keepdims=True),
                                            jnp.max(s_ctx, axis=1, keepdims=True)), sk)
                m_scr[pl.ds(r0, ATTN_TILE), :] = m
                p_scr[pl.ds(r0, ATTN_TILE), 0:span] = jnp.exp2(s_loc - m)
                p_scr[pl.ds(r0, ATTN_TILE), span:width] = jnp.exp2(s_ctx - m)
            own = (lax.broadcasted_iota(jnp.int32, (1, KV_WIDTH), 1) // HEAD_DIM) == kh
            o_ext = (_dot(p_scr[:, 0:span], jnp.where(own, vw, 1.0))
                     + _dot(p_scr[:, span:width], jnp.where(own, vc, 1.0)))
            other = (1 - kh) * HEAD_DIM
            sk_col = jnp.concatenate([jnp.full((BLOCK, 1), sink_ref[h] * LOG2E, F32) for h in heads], axis=0)
            den = o_ext[:, other:other + 1] + jnp.exp2(sk_col - m_scr[...])
            o = o_ext[:, lo:hi] / den
            outs.extend(o[g * BLOCK:(g + 1) * BLOCK] for g in range(GQA_GROUP))
        o_ref[blk * BLOCK:(blk + 1) * BLOCK, :] = jnp.concatenate(outs, axis=1)


def _attn_call(sink, q, k, v, kc, vc, tq=512):
    bsz, seq, _ = q.shape
    n_ctx = kc.shape[1]
    span = 3 * BLOCK
    rows_all = GQA_GROUP * BLOCK
    full_kv = pl.BlockSpec((None, seq, KV_WIDTH), lambda b, i: (b, 0, 0))
    ctx_kv = pl.BlockSpec((None, n_ctx, KV_WIDTH), lambda b, i: (b, 0, 0))
    qspec = pl.BlockSpec((None, tq, ATT_WIDTH), lambda b, i: (b, i, 0))
    return pl.pallas_call(
        functools.partial(_attn_kernel, seq=seq, tq=tq),
        grid=(bsz, seq // tq),
        in_specs=[pl.BlockSpec(memory_space=pltpu.SMEM), qspec, full_kv, full_kv, ctx_kv, ctx_kv],
        out_specs=qspec,
        out_shape=jax.ShapeDtypeStruct((bsz, seq, ATT_WIDTH), F32),
        scratch_shapes=[pltpu.VMEM((rows_all, span + n_ctx), F32),
                        pltpu.VMEM((rows_all, span + n_ctx), F32),
                        pltpu.VMEM((BLOCK, span), F32),
                        pltpu.VMEM((rows_all, 1), F32)],
        compiler_params=_cparams(("arbitrary", "arbitrary")),
        name="attn",
    )(sink, q, k, v, kc, vc)


def _out_kernel(att_ref, u_ref, up_ref, un_ref, bg_ref, cw_ref, wo_ref, x_ref, gt_ref,
                g_ref, sh_ref, sc_ref, wr_ref, acc_ref, h2t_ref, aff_ref, *, tm):
    i = pl.program_id(1)
    last = pl.num_programs(1) - 1
    u = u_ref[...]
    row = lax.broadcasted_iota(jnp.int32, u.shape, 0)
    prev_row = jnp.where(i > 0, up_ref[SUBLANES - 1:SUBLANES, :], 0.0)
    next_row = jnp.where(i < last, un_ref[0:1, :], 0.0)
    u_prev = jnp.where(row == 0, prev_row, pltpu.roll(u, 1, 0))
    u_next = jnp.where(row == tm - 1, next_row, pltpu.roll(u, tm - 1, 0))
    conv = bg_ref[...] * (cw_ref[0:1, :] * u_prev + cw_ref[1:2, :] * u + cw_ref[2:3, :] * u_next)
    y = _dot(att_ref[...], wo_ref[0:ATT_WIDTH, :]) + _dot(conv, wo_ref[ATT_WIDTH:, :])
    x1 = x_ref[...] + gt_ref[...] * y
    d = x1.shape[-1]
    h2 = _norm_mod(x1, g_ref[...], sh_ref[...], sc_ref[...])
    assert d == SUBLANES * LANES
    for r in range(SUBLANES):
        cols = slice(r * LANES, (r + 1) * LANES)
        acc_ref[pl.ds(r, tm, stride=SUBLANES), :] = x1[:, cols]
        h2t_ref[pl.ds(r, tm, stride=SUBLANES), :] = h2[:, cols]
    def split3(t):
        hi = t.astype(BF16)
        r1 = t - hi.astype(F32)
        mid = r1.astype(BF16)
        return hi, mid, (r1 - mid.astype(F32)).astype(BF16)

    def nt_bf16(a, b):
        return lax.dot_general(a, b, (((1,), (1,)), ((), ())), preferred_element_type=F32)

    h_hi, h_mid, h_lo = split3(h2)
    w_hi, w_mid, w_lo = split3(wr_ref[...])
    n_exp = w_hi.shape[0]
    by_hi = nt_bf16(jnp.concatenate([w_hi, w_mid, w_lo], axis=0), h_hi)
    by_mid = nt_bf16(jnp.concatenate([w_hi, w_mid], axis=0), h_mid)
    logits = (((by_hi[2 * n_exp:] + by_mid[n_exp:]) + nt_bf16(w_hi, h_lo))
              + (by_hi[n_exp:2 * n_exp] + by_mid[:n_exp])) + by_hi[:n_exp]
    mx = jnp.max(logits, axis=0, keepdims=True)
    ex = jnp.exp(logits - mx)
    aff_ref[...] = ex / jnp.sum(ex, axis=0, keepdims=True)


def _out_call(att, u, bg, conv_w, w_out, x, gt1, g_ffn, sh2, sc2, w_router_t, tm=1024):
    bsz, seq, d = x.shape
    cw = u.shape[-1]
    n_exp = w_router_t.shape[0]
    nh = seq // SUBLANES

    def rows(width):
        return pl.BlockSpec((None, tm, width), lambda b, i: (b, i, 0))

    halo_prev = pl.BlockSpec((None, SUBLANES, cw),
                             lambda b, i: (b, jnp.maximum(i * (tm // SUBLANES) - 1, 0), 0))
    halo_next = pl.BlockSpec((None, SUBLANES, cw),
                             lambda b, i: (b, jnp.minimum((i + 1) * (tm // SUBLANES), nh - 1), 0))
    vec = pl.BlockSpec((1, d), lambda b, i: (0, 0))
    bvec = pl.BlockSpec((None, 1, d), lambda b, i: (b, 0, 0))
    return pl.pallas_call(
        functools.partial(_out_kernel, tm=tm),
        grid=(bsz, seq // tm),
        in_specs=[rows(ATT_WIDTH), rows(cw), halo_prev, halo_next, rows(cw),
                  pl.BlockSpec((3, cw), lambda b, i: (0, 0)),
                  pl.BlockSpec((d, d), lambda b, i: (0, 0)),
                  rows(d), bvec, vec, bvec, bvec,
                  pl.BlockSpec((n_exp, d), lambda b, i: (0, 0))],
        out_specs=[pl.BlockSpec((None, tm * SUBLANES, LANES), lambda b, i: (b, i, 0)),
                   pl.BlockSpec((None, tm * SUBLANES, LANES), lambda b, i: (b, i, 0)),
                   pl.BlockSpec((None, n_exp, tm), lambda b, i: (b, 0, i))],
        out_shape=[jax.ShapeDtypeStruct((bsz, seq * SUBLANES, LANES), F32),
                   jax.ShapeDtypeStruct((bsz, seq * SUBLANES, LANES), F32),
                   jax.ShapeDtypeStruct((bsz, n_exp, seq), F32)],
        compiler_params=_cparams(("arbitrary", "arbitrary")),
        name="out_proj",
    )(att, u, u, u, bg, conv_w, w_out, x, gt1, g_ffn, sh2, sc2, w_router_t)


def _prefix_rows(m, tri, ones, low):
    n_exp, n_chunk, _ = m.shape
    mb = m.astype(BF16).reshape(n_exp * n_chunk, LANES)
    within = jnp.dot(mb, tri, preferred_element_type=F32).reshape(n_exp, n_chunk, LANES)
    tot = jnp.dot(mb, ones, preferred_element_type=F32).astype(BF16).reshape(n_exp, n_chunk, LANES)
    carry = jnp.stack([jnp.dot(low, tot[e], preferred_element_type=F32) for e in range(n_exp)], axis=0)
    return carry + within, carry


def _sel_kernel(aff_ref, pos_ref, carry_ref, *, cap):
    a = aff_ref[...]
    n_exp, n_chunk, _ = a.shape
    bits = lax.bitcast_convert_type(a, jnp.int32)

    def count(pred):
        c = jnp.sum(pred.astype(F32), axis=2, keepdims=True)
        return jnp.sum(c, axis=1, keepdims=True)

    def body(it, thr):
        cand = thr | jnp.left_shift(jnp.int32(1), 30 - it)
        return jnp.where(count(bits >= cand) >= cap, cand, thr)

    thr = lax.fori_loop(0, 31, body, jnp.zeros((n_exp, 1, 1), jnp.int32))
    gt = bits > thr
    eq = bits == thr
    need = cap - count(gt)

    r = lax.broadcasted_iota(jnp.int32, (LANES, LANES), 0)
    c = lax.broadcasted_iota(jnp.int32, (LANES, LANES), 1)
    tri = (r < c).astype(BF16)
    ones = jnp.ones((LANES, LANES), BF16)
    rr = lax.broadcasted_iota(jnp.int32, (n_chunk, n_chunk), 0)
    cc = lax.broadcasted_iota(jnp.int32, (n_chunk, n_chunk), 1)
    low = (cc < rr).astype(BF16)

    eq_rank, _ = _prefix_rows(eq, tri, ones, low)
    sel = gt | (eq & (eq_rank < need))
    pos, carry = _prefix_rows(sel, tri, ones, low)
    pos_ref[...] = jnp.where(sel, pos, -1.0)
    carry_ref[...] = carry


def _sel_call(aff4, cap):
    bsz, n_exp, n_chunk, _ = aff4.shape
    spec = pl.BlockSpec((None, n_exp, n_chunk, LANES), lambda b: (b, 0, 0, 0))
    return pl.pallas_call(
        functools.partial(_sel_kernel, cap=cap),
        grid=(bsz,),
        in_specs=[spec],
        out_specs=[spec, spec],
        out_shape=[jax.ShapeDtypeStruct(aff4.shape, F32)] * 2,
        compiler_params=_cparams(("arbitrary",)),
        name="select",
    )(aff4)


COMPACT_ROWS = 16
SLOT_BLOCK = LANES


def _compact_kernel(cum_ref, pos_ref, aff_ref, out_ref, *, n_exp, n_tchunk):
    b = pl.program_id(0)
    e = pl.program_id(1)
    base = (b * n_exp + e) * (n_tchunk + 1)
    out_ref[...] = jnp.zeros(out_ref.shape, F32)
    nt = (((1,), (1,)), ((), ()))
    rowi = lax.broadcasted_iota(jnp.int32, (COMPACT_ROWS, LANES), 0)
    lane = lax.broadcasted_iota(jnp.int32, (COMPACT_ROWS, LANES), 1).astype(F32)
    slot_b = lax.broadcasted_iota(jnp.int32, (SLOT_BLOCK, LANES), 0).astype(F32).astype(BF16)
    one_b = jnp.ones((), BF16)
    zero_b = jnp.zeros((), BF16)

    n_sblk = out_ref.shape[0] - 1

    def chunk_vals(c):
        a = aff_ref[c]
        g_hi = a.astype(BF16).astype(F32)
        r1 = a - g_hi
        g_mid = r1.astype(BF16).astype(F32)
        g_lo = r1 - g_mid
        pieces = []
        for r in range(SUBLANES):
            def bc(t):
                return jnp.broadcast_to(t[r:r + 1, :], (COMPACT_ROWS, LANES))

            piece = jnp.where(rowi == 0, float(c * SUBLANES + r),
                              jnp.where(rowi == 1, lane,
                                        jnp.where(rowi == 2, bc(g_hi),
                                                  jnp.where(rowi == 3, bc(g_mid),
                                                            jnp.where(rowi == 4, bc(g_lo), 0.0)))))
            pieces.append(piece)
        return jnp.concatenate(pieces, axis=1).astype(BF16)

    def visit(j, p, vals):
        rel = p - jnp.asarray(j * SLOT_BLOCK).astype(F32)
        rel = jnp.where((rel >= 0.0) & (rel < SLOT_BLOCK), rel, -1.0)
        pieces_oh = []
        for r in range(SUBLANES):
            row16 = jnp.broadcast_to(rel[r:r + 1, :], (2 * SUBLANES, LANES)).astype(BF16)
            blk = jnp.concatenate([row16] * (SLOT_BLOCK // (2 * SUBLANES)), axis=0)
            pieces_oh.append(jnp.where(blk == slot_b, one_b, zero_b))
        oh = jnp.concatenate(pieces_oh, axis=1)
        out_ref[j] += lax.dot_general(vals, oh, nt, preferred_element_type=F32)

    def first_block(c):
        return jnp.minimum(cum_ref[base + c] // SLOT_BLOCK, n_sblk - 1)

    for c in range(n_tchunk):
        p = pos_ref[c]
        vals = chunk_vals(c)
        j0 = first_block(c)
        visit(j0, p, vals)
        visit(j0 + 1, p, vals)

    for c in range(n_tchunk):
        lo = cum_ref[base + c]
        hi = cum_ref[base + c + 1]
        j_end = jnp.where(hi > lo, (hi - 1) // SLOT_BLOCK + 1, 0)

        def more(j, carry, c=c):
            visit(j, pos_ref[c], chunk_vals(c))
            return carry

        lax.fori_loop(first_block(c) + 2, j_end, more, 0)


def _compact_call(cum, pos5, aff5, cap):
    bsz, n_exp, n_tchunk, _, _ = pos5.shape
    n_sblk = cap // SLOT_BLOCK
    spec = pl.BlockSpec((None, None, n_tchunk, SUBLANES, LANES), lambda b, e, cum_ref: (b, e, 0, 0, 0))
    return pl.pallas_call(
        functools.partial(_compact_kernel, n_exp=n_exp, n_tchunk=n_tchunk),
        grid_spec=pltpu.PrefetchScalarGridSpec(
            num_scalar_prefetch=1,
            grid=(bsz, n_exp),
            in_specs=[spec, spec],
            out_specs=pl.BlockSpec((None, None, n_sblk + 1, COMPACT_ROWS, LANES),
                                   lambda b, e, cum_ref: (b, e, 0, 0, 0)),
        ),
        out_shape=jax.ShapeDtypeStruct((bsz, n_exp, n_sblk + 1, COMPACT_ROWS, LANES), F32),
        compiler_params=_cparams(("arbitrary", "arbitrary")),
        name="compact",
    )(cum, pos5, aff5)[:, :, :n_sblk]


FFN_ROWS = 256


FFN_COLS = 256


def _ffn_kernel(idx_ref, acc_in, h2t_hbm, g_ref, gt_ref, wg_ref, wu_ref, wd_ref, acc_hbm,
                abuf0, abuf1, hbuf0, hbuf1, wgb, wub, wdb, gasem, ghsem, ssem, *, cap, n_exp, d):
    del acc_in
    e = pl.program_id(0)
    abufs = (abuf0, abuf1)
    hbufs = (hbuf0, hbuf1)
    n_chunk = cap // FFN_ROWS
    f = wgb.shape[1]
    tile = SUBLANES

    def hbm_tile(ee, b, s):
        return pl.ds(pl.multiple_of(idx_ref[(b * n_exp + ee) * cap + s], tile), tile)

    def dma_thread(s):
        return s % 2 if isinstance(s, int) else 0

    def gather_start(ee, b, s):
        src = hbm_tile(ee, b, s)
        dst = pl.ds(s * tile, tile)
        pltpu.make_async_copy(acc_hbm.at[src], abufs[b].at[dst], gasem.at[b]).start(priority=dma_thread(s))
        pltpu.make_async_copy(h2t_hbm.at[src], hbufs[b].at[dst], ghsem.at[b]).start(priority=dma_thread(s))

    def gather_wait(b):
        pltpu.make_async_copy(acc_hbm.at[pl.ds(0, cap * tile)], abufs[b], gasem.at[b]).wait()
        pltpu.make_async_copy(h2t_hbm.at[pl.ds(0, cap * tile)], hbufs[b], ghsem.at[b]).wait()

    def scatter_start(ee, b, s):
        pltpu.make_async_copy(abufs[b].at[pl.ds(s * tile, tile)], acc_hbm.at[hbm_tile(ee, b, s)],
                              ssem.at[b]).start(priority=dma_thread(s))

    def scatter_wait(b):
        pltpu.make_async_copy(abufs[b], acc_hbm.at[pl.ds(0, cap * tile)], ssem.at[b]).wait()

    def feature_block(buf, ci, r):
        return buf.at[pl.ds(ci * FFN_ROWS * tile + r, FFN_ROWS, stride=tile), :]

    def compute_chunk(b, ci, todo):
        n_piece = 3 * (f // FFN_COLS)
        per_piece = -(-len(todo) // n_piece)

        def issue_some():
            for _ in range(min(per_piece, len(todo))):
                todo.pop(0)()

        xc = jnp.concatenate([feature_block(hbufs[b], ci, r)[...] for r in range(d // LANES)],
                             axis=1).astype(BF16)
        g_col = jnp.concatenate(
            [jnp.broadcast_to(g_ref[b, j:j + 1, :], (LANES, LANES)).T
             for j in range(ci * FFN_ROWS // LANES, (ci + 1) * FFN_ROWS // LANES)], axis=0)
        gt = gt_ref[b]
        hm = []
        for nj in range(f // FFN_COLS):
            cols = slice(nj * FFN_COLS, (nj + 1) * FFN_COLS)
            a = jnp.dot(xc, wgb[:, cols], preferred_element_type=F32)
            issue_some()
            u = jnp.dot(xc, wub[:, cols], preferred_element_type=F32)
            issue_some()
            hm.append((a * jax.nn.sigmoid(a) * u).astype(BF16))
        hm = jnp.concatenate(hm, axis=1)
        for nj in range(d // FFN_COLS):
            y = jnp.dot(hm, wdb[:, nj * FFN_COLS:(nj + 1) * FFN_COLS], preferred_element_type=F32)
            for k in range(FFN_COLS // LANES):
                r = nj * (FFN_COLS // LANES) + k
                acc_r = feature_block(abufs[b], ci, r)
                acc_r[...] = acc_r[...] + gt[:, r * LANES:(r + 1) * LANES] * (y[:, k * LANES:(k + 1) * LANES] * g_col)
            issue_some()
        assert not todo

    def chunk_rows(ci):
        return range(ci * FFN_ROWS, (ci + 1) * FFN_ROWS)

    def split(n, parts):
        bounds = [n * k // parts for k in range(parts + 1)]
        return [range(bounds[k], bounds[k + 1]) for k in range(parts)]

    @pl.when(e == 0)
    def _():
        def first(s, carry):
            gather_start(0, 0, s)
            return carry
        lax.fori_loop(0, cap, first, 0)

    @pl.when(e > 0)
    def _():
        scatter_wait(1)

    wgb[...] = wg_ref[...].astype(BF16)
    wub[...] = wu_ref[...].astype(BF16)
    wdb[...] = wd_ref[...].astype(BF16)

    gather_wait(0)
    for ci in range(n_chunk):
        todo = [functools.partial(gather_start, e, 1, s) for s in chunk_rows(ci)]
        if ci > 0:
            todo += [functools.partial(scatter_start, e, 0, s) for s in chunk_rows(ci - 1)]
        compute_chunk(0, ci, todo)

    ne = jnp.minimum(e + 1, n_exp - 1)
    gather_wait(1)
    compute_chunk(1, 0, [functools.partial(scatter_start, e, 0, s) for s in chunk_rows(n_chunk - 1)])
    scatter_wait(0)
    for ci, part in zip(range(1, n_chunk), split(cap, n_chunk - 1)):
        todo = [functools.partial(gather_start, ne, 0, s) for s in part]
        todo += [functools.partial(scatter_start, e, 1, s) for s in chunk_rows(ci - 1)]
        compute_chunk(1, ci, todo)
    for s in chunk_rows(n_chunk - 1):
        scatter_start(e, 1, s)

    @pl.when(e == n_exp - 1)
    def _():
        scatter_wait(1)
        gather_wait(0)


def _ffn_call(idx, acc_t, h2_t, gate, gt2, w_gate, w_up, w_down, cap):
    bsz, rows8, _ = acc_t.shape
    n_exp, d, f = w_gate.shape
    assert bsz == 2, "ffn double-buffers exactly two samples per expert"
    kern = functools.partial(_ffn_kernel, cap=cap, n_exp=n_exp, d=d)
    any_spec = pl.BlockSpec(memory_space=pl.ANY)
    tile_buf = pltpu.VMEM((cap * SUBLANES, LANES), F32)
    out = pl.pallas_call(
        kern,
        grid_spec=pltpu.PrefetchScalarGridSpec(
            num_scalar_prefetch=1,
            grid=(n_exp,),
            in_specs=[any_spec, any_spec,
                      pl.BlockSpec((bsz, None, cap // LANES, LANES), lambda e, idx_ref: (0, e, 0, 0)),
                      pl.BlockSpec((bsz, 1, d), lambda e, idx_ref: (0, 0, 0)),
                      pl.BlockSpec((None, d, f), lambda e, idx_ref: (e, 0, 0)),
                      pl.BlockSpec((None, d, f), lambda e, idx_ref: (e, 0, 0)),
                      pl.BlockSpec((None, f, d), lambda e, idx_ref: (e, 0, 0))],
            out_specs=any_spec,
            scratch_shapes=[tile_buf, tile_buf, tile_buf, tile_buf,
                            pltpu.VMEM((d, f), BF16), pltpu.VMEM((d, f), BF16), pltpu.VMEM((f, d), BF16),
                            pltpu.SemaphoreType.DMA((bsz,)), pltpu.SemaphoreType.DMA((bsz,)),
                            pltpu.SemaphoreType.DMA((bsz,))],
        ),
        out_shape=jax.ShapeDtypeStruct((bsz * rows8, LANES), F32),
        input_output_aliases={1: 0},
        compiler_params=_cparams(("arbitrary",)),
        name="ffn",
    )(idx, acc_t.reshape(bsz * rows8, LANES), h2_t.reshape(bsz * rows8, LANES), gate, gt2,
      w_gate, w_up, w_down)
    return out.reshape(bsz, rows8, LANES)


def _final_kernel(x_ref, g_ref, o_ref, *, tm):
    xf = jnp.concatenate([x_ref[pl.ds(r, tm, stride=SUBLANES), :] for r in range(SUBLANES)], axis=1)
    o_ref[...] = xf * lax.rsqrt(jnp.mean(xf * xf, axis=-1, keepdims=True) + EPS) * g_ref[...]


def _final_call(acc_t, g, tm=1024):
    bsz, rows8, _ = acc_t.shape
    seq = rows8 // SUBLANES
    d = SUBLANES * LANES
    return pl.pallas_call(
        functools.partial(_final_kernel, tm=tm),
        grid=(bsz, seq // tm),
        in_specs=[pl.BlockSpec((None, tm * SUBLANES, LANES), lambda b, i: (b, i, 0)),
                  pl.BlockSpec((1, d), lambda b, i: (0, 0))],
        out_specs=pl.BlockSpec((None, tm, d), lambda b, i: (b, i, 0)),
        out_shape=jax.ShapeDtypeStruct((bsz, seq, d), F32),
        compiler_params=_cparams(("arbitrary", "arbitrary")),
        name="final",
    )(acc_t, g)


def _rope_tables(seq):
    t = np.arange(seq)
    n_freq = HEAD_DIM // 4
    inv = ROPE_THETA ** (-np.arange(n_freq, dtype=np.float64) / n_freq)
    ang = np.concatenate([(t // GRID_W)[:, None] * inv, (t % GRID_W)[:, None] * inv], axis=1)
    cos_h = np.concatenate([np.cos(ang), np.cos(ang)], axis=1)
    sin_h = np.concatenate([-np.sin(ang), np.sin(ang)], axis=1)
    reps = LANES // HEAD_DIM
    return (jnp.asarray(np.tile(cos_h, (1, reps)), F32), jnp.asarray(np.tile(sin_h, (1, reps)), F32))


def kernel(x, c, ctx, c_ctx, w_ada, b_ada, g_mix, w_in, conv_w, sink, w_out, g_ffn,
           w_router, w_gate, w_up, w_down, g_final):
    bsz, seq, d = x.shape
    n_exp = w_router.shape[-1]
    cap = EC_CAPACITY_FACTOR * seq // n_exp
    assert w_ada.shape[0] == 1, "single trunk layer"
    assert bsz + 1 <= SUBLANES and seq % (SUBLANES * LANES) == 0 and cap % SLOT_BLOCK == 0

    cin = jnp.zeros((SUBLANES, d), F32).at[:bsz].set(c).at[bsz].set(c_ctx)
    mod = _mod_call(cin, w_ada[0], b_ada[0][None, :])
    sh1, sc1, gt1, sh2, sc2, gt2 = [m[:bsz, None, :] for m in jnp.split(mod, 6, axis=-1)]
    csh1 = mod[bsz:bsz + 1, 0:d]
    csc1 = mod[bsz:bsz + 1, d:2 * d]

    w_in0 = w_in[0]
    g_mix_r = g_mix[0][None, :]
    kv0 = ATT_WIDTH
    k_c, v_c = _ctx_call(ctx, g_mix_r, csh1, csc1, w_in0[:, kv0:kv0 + 2 * KV_WIDTH])

    cos_t, sin_t = _rope_tables(seq)
    q, k, v, bg, u = _in_call(x, g_mix_r, sh1, sc1, w_in0, cos_t, sin_t)
    att = _attn_call(sink[0], q, k, v, k_c, v_c)
    acc_t, h2_t, aff = _out_call(att, u, bg, conv_w[0], w_out[0], x, gt1,
                                 g_ffn[0][None, :], sh2, sc2, w_router[0].T)

    n_chunk = seq // LANES
    aff4 = aff.reshape(bsz, n_exp, n_chunk, LANES)
    pos, carry = _sel_call(aff4, cap)
    n_tchunk = n_chunk // SUBLANES
    cum = jnp.concatenate([carry[:, :, ::SUBLANES, 0], jnp.full((bsz, n_exp, 1), cap, F32)], axis=-1)
    cum = cum.astype(jnp.int32).reshape(-1)
    comp = _compact_call(cum, pos.reshape(bsz, n_exp, n_tchunk, SUBLANES, LANES),
                         aff4.reshape(bsz, n_exp, n_tchunk, SUBLANES, LANES), cap)
    idx = (comp[:, :, :, 0, :] * LANES + comp[:, :, :, 1, :]).astype(jnp.int32)
    idx = idx + (jnp.arange(bsz, dtype=jnp.int32) * seq)[:, None, None, None]
    idx = (idx * SUBLANES).reshape(-1)
    gate = (comp[:, :, :, 2, :] + comp[:, :, :, 3, :]) + comp[:, :, :, 4, :]

    acc_t = _ffn_call(idx, acc_t, h2_t, gate, gt2, w_gate[0], w_up[0], w_down[0], cap)
    return _final_call(acc_t, g_final[None, :])
```

```python
import functools

import numpy as np
import jax
import jax.numpy as jnp
from jax import lax
from jax.experimental import pallas as pl
from jax.experimental.pallas import tpu as pltpu

GRID_W = 64
N_HEADS = 8
N_KV_HEADS = 2
HEAD_DIM = 64
GQA_GROUP = N_HEADS // N_KV_HEADS
ATT_WIDTH = N_HEADS * HEAD_DIM
KV_WIDTH = N_KV_HEADS * HEAD_DIM
WINDOW = 128
BLOCK = 128
ROPE_THETA = 10000.0
N_EXPERTS = 16
EC_CAPACITY_FACTOR = 2
EPS = 1e-6

LANES = 128
SUBLANES = 8
MXU_COLS = 256
SLAB_ROWS = 2 * SUBLANES
VMEM_LIMIT = 56 * 1024 * 1024

F32 = jnp.float32
BF16 = jnp.bfloat16
NEG = -1e30
LOG2E = 1.4426950408889634


def _cparams(sem):
    return pltpu.CompilerParams(dimension_semantics=sem, vmem_limit_bytes=VMEM_LIMIT)


def _dot(a, b):
    return jnp.dot(a, b, precision=lax.Precision.DEFAULT, preferred_element_type=F32)


def _dot_nt(a, b):
    return lax.dot_general(a, b, (((1,), (1,)), ((), ())), precision=lax.Precision.DEFAULT,
                           preferred_element_type=F32)


def _norm_mod(xf, g, shift, scale):
    r = xf * lax.rsqrt(jnp.mean(xf * xf, axis=-1, keepdims=True) + EPS)
    return (r * g) * (1.0 + scale) + shift


def _mod_kernel(c_ref, w_ref, b_ref, o_ref):
    c = c_ref[...]
    s = c * jax.nn.sigmoid(c)
    o_ref[...] = _dot(s, w_ref[...]) + b_ref[...]


def _mod_call(cin, w, bias, tn=1024):
    d, n = w.shape
    return pl.pallas_call(
        _mod_kernel,
        grid=(n // tn,),
        in_specs=[pl.BlockSpec((SUBLANES, d), lambda j: (0, 0)),
                  pl.BlockSpec((d, tn), lambda j: (0, j)),
                  pl.BlockSpec((1, tn), lambda j: (0, j))],
        out_specs=pl.BlockSpec((SUBLANES, tn), lambda j: (0, j)),
        out_shape=jax.ShapeDtypeStruct((SUBLANES, n), F32),
        compiler_params=_cparams(("arbitrary",)),
        name="mod",
    )(cin, w, bias)


def _ctx_kernel(x_ref, g_ref, sh_ref, sc_ref, w_ref, k_ref, v_ref):
    h = _norm_mod(x_ref[...], g_ref[...], sh_ref[...], sc_ref[...])
    kv = _dot(h, w_ref[...])
    k_ref[...] = kv[:, :KV_WIDTH]
    v_ref[...] = kv[:, KV_WIDTH:]


def _ctx_call(ctx, g, sh, sc, w_kv):
    bsz, n, d = ctx.shape
    vec = pl.BlockSpec((1, d), lambda b: (0, 0))
    kv_spec = pl.BlockSpec((None, n, KV_WIDTH), lambda b: (b, 0, 0))
    return pl.pallas_call(
        _ctx_kernel,
        grid=(bsz,),
        in_specs=[pl.BlockSpec((None, n, d), lambda b: (b, 0, 0)), vec, vec, vec,
                  pl.BlockSpec((d, 2 * KV_WIDTH), lambda b: (0, 0))],
        out_specs=[kv_spec, kv_spec],
        out_shape=[jax.ShapeDtypeStruct((bsz, n, KV_WIDTH), F32)] * 2,
        compiler_params=_cparams(("arbitrary",)),
        name="ctx_kv",
    )(ctx, g, sh, sc, w_kv)


def _rope(t, cos, sin_signed):
    lane = lax.broadcasted_iota(jnp.int32, t.shape, 1)
    first = (lane % HEAD_DIM) < (HEAD_DIM // 2)
    swapped = jnp.where(first, pltpu.roll(t, LANES - HEAD_DIM // 2, 1), pltpu.roll(t, HEAD_DIM // 2, 1))
    return t * cos + swapped * sin_signed


def _in_kernel(x_ref, g_ref, sh_ref, sc_ref, w_ref, cos_ref, sin_ref,
               q_ref, k_ref, v_ref, bg_ref, u_ref, *, conv_width):
    h = _norm_mod(x_ref[...], g_ref[...], sh_ref[...], sc_ref[...])
    cos = cos_ref[...]
    sin = sin_ref[...]
    scale = HEAD_DIM ** -0.5 * LOG2E
    for j in range(ATT_WIDTH // MXU_COLS):
        qj = _dot(h, w_ref[:, j * MXU_COLS:(j + 1) * MXU_COLS])
        for jj in range(MXU_COLS // LANES):
            c0 = j * MXU_COLS + jj * LANES
            q_ref[:, c0:c0 + LANES] = _rope(qj[:, jj * LANES:(jj + 1) * LANES], cos, sin) * scale
    o = ATT_WIDTH
    kv = _dot(h, w_ref[:, o:o + 2 * KV_WIDTH])
    k_ref[...] = _rope(kv[:, :KV_WIDTH], cos, sin)
    v_ref[...] = kv[:, KV_WIDTH:]
    o += 2 * KV_WIDTH
    bg_ref[...] = _dot(h, w_ref[:, o:o + conv_width])
    o += conv_width
    cg = _dot(h, w_ref[:, o:o + conv_width])
    o += conv_width
    hv = _dot(h, w_ref[:, o:o + conv_width])
    u_ref[...] = cg * hv


def _in_call(x, g, sh, sc, w_in, cos_t, sin_t, tm=1024):
    bsz, seq, d = x.shape
    n_in = w_in.shape[1]
    cw = d - ATT_WIDTH
    vec = pl.BlockSpec((1, d), lambda b, i: (0, 0))
    bvec = pl.BlockSpec((None, 1, d), lambda b, i: (b, 0, 0))
    tab = pl.BlockSpec((tm, LANES), lambda b, i: (i, 0))

    def rows(width):
        return pl.BlockSpec((None, tm, width), lambda b, i: (b, i, 0))

    return pl.pallas_call(
        functools.partial(_in_kernel, conv_width=cw),
        grid=(bsz, seq // tm),
        in_specs=[rows(d), vec, bvec, bvec, pl.BlockSpec((d, n_in), lambda b, i: (0, 0)), tab, tab],
        out_specs=[rows(ATT_WIDTH), rows(KV_WIDTH), rows(KV_WIDTH), rows(cw), rows(cw)],
        out_shape=[jax.ShapeDtypeStruct((bsz, seq, ATT_WIDTH), F32),
                   jax.ShapeDtypeStruct((bsz, seq, KV_WIDTH), F32),
                   jax.ShapeDtypeStruct((bsz, seq, KV_WIDTH), F32),
                   jax.ShapeDtypeStruct((bsz, seq, cw), F32),
                   jax.ShapeDtypeStruct((bsz, seq, cw), F32)],
        compiler_params=_cparams(("arbitrary", "arbitrary")),
        name="in_proj",
    )(x, g, sh, sc, w_in, cos_t, sin_t)


ATTN_TILE = 32


def _attn_kernel(sink_ref, q_ref, k_ref, v_ref, kc_ref, vc_ref, o_ref,
                 s_scr, p_scr, bias_scr, m_scr, *, seq, tq):
    i = pl.program_id(1)
    nblk = tq // BLOCK
    span = 3 * BLOCK
    width = s_scr.shape[1]
    rows_all = GQA_GROUP * BLOCK
    tiles_per_head = BLOCK // ATTN_TILE
    kc = kc_ref[...]
    vc = vc_ref[...]
    for blk in range(nblk):
        n = i * nblk + blk
        ws = pl.multiple_of(jnp.clip((n - 1) * BLOCK, 0, seq - span), BLOCK)
        kw = k_ref[pl.ds(ws, span), :]
        vw = v_ref[pl.ds(ws, span), :]
        qb = q_ref[blk * BLOCK:(blk + 1) * BLOCK, :]
        qpos = n * BLOCK + lax.broadcasted_iota(jnp.int32, (BLOCK, span), 0)
        kpos = ws + lax.broadcasted_iota(jnp.int32, (BLOCK, span), 1)
        bias_scr[...] = jnp.where(jnp.abs(kpos - qpos) <= WINDOW, 0.0, NEG)
        outs = []
        for kh in range(N_KV_HEADS):
            heads = [kh * GQA_GROUP + g for g in range(GQA_GROUP)]
            qs = jnp.concatenate([qb[:, h * HEAD_DIM:(h + 1) * HEAD_DIM] for h in heads], axis=0)
            lo, hi = kh * HEAD_DIM, (kh + 1) * HEAD_DIM
            s_scr[:, 0:span] = _dot_nt(qs, kw[:, lo:hi])
            s_scr[:, span:width] = _dot_nt(qs, kc[:, lo:hi])

            for t in range(rows_all // ATTN_TILE):
                r0 = t * ATTN_TILE
                b0 = (t % tiles_per_head) * ATTN_TILE
                sk = sink_ref[kh * GQA_GROUP + t // tiles_per_head] * LOG2E
                s_loc = s_scr[pl.ds(r0, ATTN_TILE), 0:span] + bias_scr[pl.ds(b0, ATTN_TILE), :]
                s_ctx = s_scr[pl.ds(r0, ATTN_TILE), span:width]
                m = jnp.maximum(jnp.maximum(jnp.max(s_loc, axis=1, keepdims=True),
                                            jnp.max(s_ctx, axis=1, keepdims=True)), sk)
                m_scr[pl.ds(r0, ATTN_TILE), :] = m
                p_scr[pl.ds(r0, ATTN_TILE), 0:span] = jnp.exp2(s_loc - m)
                p_scr[pl.ds(r0, ATTN_TILE), span:width] = jnp.exp2(s_ctx - m)
            own = (lax.broadcasted_iota(jnp.int32, (1, KV_WIDTH), 1) // HEAD_DIM) == kh
            o_ext = (_dot(p_scr[:, 0:span], jnp.where(own, vw, 1.0))
                     + _dot(p_scr[:, span:width], jnp.where(own, vc, 1.0)))
            other = (1 - kh) * HEAD_DIM
            sk_col = jnp.concatenate([jnp.full((BLOCK, 1), sink_ref[h] * LOG2E, F32) for h in heads], axis=0)
            den = o_ext[:, other:other + 1] + jnp.exp2(sk_col - m_scr[...])
            o = o_ext[:, lo:hi] / den
            outs.extend(o[g * BLOCK:(g + 1) * BLOCK] for g in range(GQA_GROUP))
        o_ref[blk * BLOCK:(blk + 1) * BLOCK, :] = jnp.concatenate(outs, axis=1)


def _attn_call(sink, q, k, v, kc, vc, tq=512):
    bsz, seq, _ = q.shape
    n_ctx = kc.shape[1]
    span = 3 * BLOCK
    rows_all = GQA_GROUP * BLOCK
    full_kv = pl.BlockSpec((None, seq, KV_WIDTH), lambda b, i: (b, 0, 0))
    ctx_kv = pl.BlockSpec((None, n_ctx, KV_WIDTH), lambda b, i: (b, 0, 0))
    qspec = pl.BlockSpec((None, tq, ATT_WIDTH), lambda b, i: (b, i, 0))
    return pl.pallas_call(
        functools.partial(_attn_kernel, seq=seq, tq=tq),
        grid=(bsz, seq // tq),
        in_specs=[pl.BlockSpec(memory_space=pltpu.SMEM), qspec, full_kv, full_kv, ctx_kv, ctx_kv],
        out_specs=qspec,
        out_shape=jax.ShapeDtypeStruct((bsz, seq, ATT_WIDTH), F32),
        scratch_shapes=[pltpu.VMEM((rows_all, span + n_ctx), F32),
                        pltpu.VMEM((rows_all, span + n_ctx), F32),
                        pltpu.VMEM((BLOCK, span), F32),
                        pltpu.VMEM((rows_all, 1), F32)],
        compiler_params=_cparams(("arbitrary", "arbitrary")),
        name="attn",
    )(sink, q, k, v, kc, vc)


def _out_kernel(att_ref, u_ref, up_ref, un_ref, bg_ref, cw_ref, wo_ref, x_ref, gt_ref,
                g_ref, sh_ref, sc_ref, wr_ref, slab_ref, aff_ref, *, tm):
    i = pl.program_id(1)
    last = pl.num_programs(1) - 1
    u = u_ref[...]
    row = lax.broadcasted_iota(jnp.int32, u.shape, 0)
    prev_row = jnp.where(i > 0, up_ref[SUBLANES - 1:SUBLANES, :], 0.0)
    next_row = jnp.where(i < last, un_ref[0:1, :], 0.0)
    u_prev = jnp.where(row == 0, prev_row, pltpu.roll(u, 1, 0))
    u_next = jnp.where(row == tm - 1, next_row, pltpu.roll(u, tm - 1, 0))
    conv = bg_ref[...] * (cw_ref[0:1, :] * u_prev + cw_ref[1:2, :] * u + cw_ref[2:3, :] * u_next)
    y = _dot(att_ref[...], wo_ref[0:ATT_WIDTH, :]) + _dot(conv, wo_ref[ATT_WIDTH:, :])
    x1 = x_ref[...] + gt_ref[...] * y
    d = x1.shape[-1]
    h2 = _norm_mod(x1, g_ref[...], sh_ref[...], sc_ref[...])
    assert d == SUBLANES * LANES
    for r in range(SUBLANES):
        cols = slice(r * LANES, (r + 1) * LANES)
        slab_ref[pl.ds(r, tm, stride=SLAB_ROWS), :] = x1[:, cols]
        slab_ref[pl.ds(SUBLANES + r, tm, stride=SLAB_ROWS), :] = h2[:, cols]
    def split3(t):
        hi = t.astype(BF16)
        r1 = t - hi.astype(F32)
        mid = r1.astype(BF16)
        return hi, mid, (r1 - mid.astype(F32)).astype(BF16)

    def nt_bf16(a, b):
        return lax.dot_general(a, b, (((1,), (1,)), ((), ())), preferred_element_type=F32)

    h_hi, h_mid, h_lo = split3(h2)
    w_hi, w_mid, w_lo = split3(wr_ref[...])
    n_exp = w_hi.shape[0]
    by_hi = nt_bf16(jnp.concatenate([w_hi, w_mid, w_lo], axis=0), h_hi)
    by_mid = nt_bf16(jnp.concatenate([w_hi, w_mid], axis=0), h_mid)
    logits = (((by_hi[2 * n_exp:] + by_mid[n_exp:]) + nt_bf16(w_hi, h_lo))
              + (by_hi[n_exp:2 * n_exp] + by_mid[:n_exp])) + by_hi[:n_exp]
    mx = jnp.max(logits, axis=0, keepdims=True)
    ex = jnp.exp(logits - mx)
    aff_ref[...] = ex / jnp.sum(ex, axis=0, keepdims=True)


def _out_call(att, u, bg, conv_w, w_out, x, gt1, g_ffn, sh2, sc2, w_router_t, tm=1024):
    bsz, seq, d = x.shape
    cw = u.shape[-1]
    n_exp = w_router_t.shape[0]
    nh = seq // SUBLANES

    def rows(width):
        return pl.BlockSpec((None, tm, width), lambda b, i: (b, i, 0))

    halo_prev = pl.BlockSpec((None, SUBLANES, cw),
                             lambda b, i: (b, jnp.maximum(i * (tm // SUBLANES) - 1, 0), 0))
    halo_next = pl.BlockSpec((None, SUBLANES, cw),
                             lambda b, i: (b, jnp.minimum((i + 1) * (tm // SUBLANES), nh - 1), 0))
    vec = pl.BlockSpec((1, d), lambda b, i: (0, 0))
    bvec = pl.BlockSpec((None, 1, d), lambda b, i: (b, 0, 0))
    return pl.pallas_call(
        functools.partial(_out_kernel, tm=tm),
        grid=(bsz, seq // tm),
        in_specs=[rows(ATT_WIDTH), rows(cw), halo_prev, halo_next, rows(cw),
                  pl.BlockSpec((3, cw), lambda b, i: (0, 0)),
                  pl.BlockSpec((d, d), lambda b, i: (0, 0)),
                  rows(d), bvec, vec, bvec, bvec,
                  pl.BlockSpec((n_exp, d), lambda b, i: (0, 0))],
        out_specs=[pl.BlockSpec((None, tm * SLAB_ROWS, LANES), lambda b, i: (b, i, 0)),
                   pl.BlockSpec((None, n_exp, tm), lambda b, i: (b, 0, i))],
        out_shape=[jax.ShapeDtypeStruct((bsz, seq * SLAB_ROWS, LANES), F32),
                   jax.ShapeDtypeStruct((bsz, n_exp, seq), F32)],
        compiler_params=_cparams(("arbitrary", "arbitrary")),
        name="out_proj",
    )(att, u, u, u, bg, conv_w, w_out, x, gt1, g_ffn, sh2, sc2, w_router_t)


def _prefix_rows(m, tri, ones, low):
    n_exp, n_chunk, _ = m.shape
    mb = m.astype(BF16).reshape(n_exp * n_chunk, LANES)
    within = jnp.dot(mb, tri, preferred_element_type=F32).reshape(n_exp, n_chunk, LANES)
    tot = jnp.dot(mb, ones, preferred_element_type=F32).astype(BF16).reshape(n_exp, n_chunk, LANES)
    carry = jnp.stack([jnp.dot(low, tot[e], preferred_element_type=F32) for e in range(n_exp)], axis=0)
    return carry + within, carry


def _sel_kernel(aff_ref, pos_ref, carry_ref, *, cap):
    a = aff_ref[...]
    n_exp, n_chunk, _ = a.shape
    bits = lax.bitcast_convert_type(a, jnp.int32)

    def count(pred):
        c = jnp.sum(pred.astype(F32), axis=2, keepdims=True)
        return jnp.sum(c, axis=1, keepdims=True)

    def body(it, thr):
        cand = thr | jnp.left_shift(jnp.int32(1), 30 - it)
        return jnp.where(count(bits >= cand) >= cap, cand, thr)

    thr = lax.fori_loop(0, 31, body, jnp.zeros((n_exp, 1, 1), jnp.int32))
    gt = bits > thr
    eq = bits == thr
    need = cap - count(gt)

    r = lax.broadcasted_iota(jnp.int32, (LANES, LANES), 0)
    c = lax.broadcasted_iota(jnp.int32, (LANES, LANES), 1)
    tri = (r < c).astype(BF16)
    ones = jnp.ones((LANES, LANES), BF16)
    rr = lax.broadcasted_iota(jnp.int32, (n_chunk, n_chunk), 0)
    cc = lax.broadcasted_iota(jnp.int32, (n_chunk, n_chunk), 1)
    low = (cc < rr).astype(BF16)

    eq_rank, _ = _prefix_rows(eq, tri, ones, low)
    sel = gt | (eq & (eq_rank < need))
    pos, carry = _prefix_rows(sel, tri, ones, low)
    pos_ref[...] = jnp.where(sel, pos, -1.0)
    carry_ref[...] = carry


def _sel_call(aff4, cap):
    bsz, n_exp, n_chunk, _ = aff4.shape
    spec = pl.BlockSpec((None, n_exp, n_chunk, LANES), lambda b: (b, 0, 0, 0))
    return pl.pallas_call(
        functools.partial(_sel_kernel, cap=cap),
        grid=(bsz,),
        in_specs=[spec],
        out_specs=[spec, spec],
        out_shape=[jax.ShapeDtypeStruct(aff4.shape, F32)] * 2,
        compiler_params=_cparams(("arbitrary",)),
        name="select",
    )(aff4)


COMPACT_ROWS = 16
SLOT_BLOCK = LANES


def _compact_kernel(cum_ref, pos_ref, aff_ref, out_ref, *, n_exp, n_tchunk):
    b = pl.program_id(0)
    e = pl.program_id(1)
    base = (b * n_exp + e) * (n_tchunk + 1)
    out_ref[...] = jnp.zeros(out_ref.shape, F32)
    nt = (((1,), (1,)), ((), ()))
    rowi = lax.broadcasted_iota(jnp.int32, (COMPACT_ROWS, LANES), 0)
    lane = lax.broadcasted_iota(jnp.int32, (COMPACT_ROWS, LANES), 1).astype(F32)
    slot_b = lax.broadcasted_iota(jnp.int32, (SLOT_BLOCK, LANES), 0).astype(F32).astype(BF16)
    one_b = jnp.ones((), BF16)
    zero_b = jnp.zeros((), BF16)

    n_sblk = out_ref.shape[0] - 1

    def chunk_vals(c):
        a = aff_ref[c]
        g_hi = a.astype(BF16).astype(F32)
        r1 = a - g_hi
        g_mid = r1.astype(BF16).astype(F32)
        g_lo = r1 - g_mid
        pieces = []
        for r in range(SUBLANES):
            def bc(t):
                return jnp.broadcast_to(t[r:r + 1, :], (COMPACT_ROWS, LANES))

            piece = jnp.where(rowi == 0, float(c * SUBLANES + r),
                              jnp.where(rowi == 1, lane,
                                        jnp.where(rowi == 2, bc(g_hi),
                                                  jnp.where(rowi == 3, bc(g_mid),
                                                            jnp.where(rowi == 4, bc(g_lo), 0.0)))))
            pieces.append(piece)
        return jnp.concatenate(pieces, axis=1).astype(BF16)

    def visit(j, p, vals):
        rel = p - jnp.asarray(j * SLOT_BLOCK).astype(F32)
        rel = jnp.where((rel >= 0.0) & (rel < SLOT_BLOCK), rel, -1.0)
        pieces_oh = []
        for r in range(SUBLANES):
            row16 = jnp.broadcast_to(rel[r:r + 1, :], (2 * SUBLANES, LANES)).astype(BF16)
            blk = jnp.concatenate([row16] * (SLOT_BLOCK // (2 * SUBLANES)), axis=0)
            pieces_oh.append(jnp.where(blk == slot_b, one_b, zero_b))
        oh = jnp.concatenate(pieces_oh, axis=1)
        out_ref[j] += lax.dot_general(vals, oh, nt, preferred_element_type=F32)

    def first_block(c):
        return jnp.minimum(cum_ref[base + c] // SLOT_BLOCK, n_sblk - 1)

    for c in range(n_tchunk):
        p = pos_ref[c]
        vals = chunk_vals(c)
        j0 = first_block(c)
        visit(j0, p, vals)
        visit(j0 + 1, p, vals)

    for c in range(n_tchunk):
        lo = cum_ref[base + c]
        hi = cum_ref[base + c + 1]
        j_end = jnp.where(hi > lo, (hi - 1) // SLOT_BLOCK + 1, 0)

        def more(j, carry, c=c):
            visit(j, pos_ref[c], chunk_vals(c))
            return carry

        lax.fori_loop(first_block(c) + 2, j_end, more, 0)


def _compact_call(cum, pos5, aff5, cap):
    bsz, n_exp, n_tchunk, _, _ = pos5.shape
    n_sblk = cap // SLOT_BLOCK
    spec = pl.BlockSpec((None, None, n_tchunk, SUBLANES, LANES), lambda b, e, cum_ref: (b, e, 0, 0, 0))
    return pl.pallas_call(
        functools.partial(_compact_kernel, n_exp=n_exp, n_tchunk=n_tchunk),
        grid_spec=pltpu.PrefetchScalarGridSpec(
            num_scalar_prefetch=1,
            grid=(bsz, n_exp),
            in_specs=[spec, spec],
            out_specs=pl.BlockSpec((None, None, n_sblk + 1, COMPACT_ROWS, LANES),
                                   lambda b, e, cum_ref: (b, e, 0, 0, 0)),
        ),
        out_shape=jax.ShapeDtypeStruct((bsz, n_exp, n_sblk + 1, COMPACT_ROWS, LANES), F32),
        compiler_params=_cparams(("arbitrary", "arbitrary")),
        name="compact",
    )(cum, pos5, aff5)[:, :, :n_sblk]


FFN_ROWS = 256


FFN_COLS = 256


def _ffn_kernel(idx_ref, slab_in, g_ref, gt_ref, wg_ref, wu_ref, wd_ref, slab_hbm,
                buf0, buf1, wgb, wub, wdb, gsem, ssem, *, cap, n_exp, d):
    del slab_in
    e = pl.program_id(0)
    bufs = (buf0, buf1)
    n_chunk = cap // FFN_ROWS
    f = wgb.shape[1]
    tile = SUBLANES

    def hbm_row(ee, b, s):
        return pl.multiple_of(idx_ref[(b * n_exp + ee) * cap + s], SLAB_ROWS)

    def dma_thread(s):
        return s % 2 if isinstance(s, int) else 0

    def gather_start(ee, b, s):
        pltpu.make_async_copy(slab_hbm.at[pl.ds(hbm_row(ee, b, s), SLAB_ROWS)],
                              bufs[b].at[pl.ds(s * SLAB_ROWS, SLAB_ROWS)],
                              gsem.at[b]).start(priority=dma_thread(s))

    def gather_wait(b):
        pltpu.make_async_copy(slab_hbm.at[pl.ds(0, cap * SLAB_ROWS)], bufs[b], gsem.at[b]).wait()

    def scatter_start(ee, b, s):
        pltpu.make_async_copy(bufs[b].at[pl.ds(s * SLAB_ROWS, tile)],
                              slab_hbm.at[pl.ds(hbm_row(ee, b, s), tile)],
                              ssem.at[b]).start(priority=dma_thread(s))

    def scatter_wait(b):
        pltpu.make_async_copy(bufs[b].at[pl.ds(0, cap * tile)], slab_hbm.at[pl.ds(0, cap * tile)],
                              ssem.at[b]).wait()

    def feature_block(b, ci, part, r):
        return bufs[b].at[pl.ds(ci * FFN_ROWS * SLAB_ROWS + part * tile + r, FFN_ROWS, stride=SLAB_ROWS), :]

    def compute_chunk(b, ci, todo):
        n_piece = 3 * (f // FFN_COLS)
        per_piece = -(-len(todo) // n_piece)

        def issue_some():
            for _ in range(min(per_piece, len(todo))):
                todo.pop(0)()

        xc = jnp.concatenate([feature_block(b, ci, 1, r)[...] for r in range(d // LANES)],
                             axis=1).astype(BF16)
        g_col = jnp.concatenate(
            [jnp.broadcast_to(g_ref[b, j:j + 1, :], (LANES, LANES)).T
             for j in range(ci * FFN_ROWS // LANES, (ci + 1) * FFN_ROWS // LANES)], axis=0)
        gt = gt_ref[b]
        hm = []
        for nj in range(f // FFN_COLS):
            cols = slice(nj * FFN_COLS, (nj + 1) * FFN_COLS)
            a = jnp.dot(xc, wgb[:, cols], preferred_element_type=F32)
            issue_some()
            u = jnp.dot(xc, wub[:, cols], preferred_element_type=F32)
            issue_some()
            hm.append((a * jax.nn.sigmoid(a) * u).astype(BF16))
        hm = jnp.concatenate(hm, axis=1)
        for nj in range(d // FFN_COLS):
            y = jnp.dot(hm, wdb[:, nj * FFN_COLS:(nj + 1) * FFN_COLS], preferred_element_type=F32)
            for k in range(FFN_COLS // LANES):
                r = nj * (FFN_COLS // LANES) + k
                acc_r = feature_block(b, ci, 0, r)
                acc_r[...] = acc_r[...] + gt[:, r * LANES:(r + 1) * LANES] * (y[:, k * LANES:(k + 1) * LANES] * g_col)
            issue_some()
        assert not todo

    def chunk_rows(ci):
        return range(ci * FFN_ROWS, (ci + 1) * FFN_ROWS)

    def split(n, parts):
        bounds = [n * k // parts for k in range(parts + 1)]
        return [range(bounds[k], bounds[k + 1]) for k in range(parts)]

    @pl.when(e == 0)
    def _():
        def first(s, carry):
            gather_start(0, 0, s)
            return carry
        lax.fori_loop(0, cap, first, 0)

    @pl.when(e > 0)
    def _():
        scatter_wait(1)

    wgb[...] = wg_ref[...].astype(BF16)
    wub[...] = wu_ref[...].astype(BF16)
    wdb[...] = wd_ref[...].astype(BF16)

    gather_wait(0)
    for ci in range(n_chunk):
        todo = [functools.partial(gather_start, e, 1, s) for s in chunk_rows(ci)]
        if ci > 0:
            todo += [functools.partial(scatter_start, e, 0, s) for s in chunk_rows(ci - 1)]
        compute_chunk(0, ci, todo)

    ne = jnp.minimum(e + 1, n_exp - 1)
    gather_wait(1)
    compute_chunk(1, 0, [functools.partial(scatter_start, e, 0, s) for s in chunk_rows(n_chunk - 1)])
    scatter_wait(0)
    for ci, part in zip(range(1, n_chunk), split(cap, n_chunk - 1)):
        todo = [functools.partial(gather_start, ne, 0, s) for s in part]
        todo += [functools.partial(scatter_start, e, 1, s) for s in chunk_rows(ci - 1)]
        compute_chunk(1, ci, todo)
    for s in chunk_rows(n_chunk - 1):
        scatter_start(e, 1, s)

    @pl.when(e == n_exp - 1)
    def _():
        scatter_wait(1)
        gather_wait(0)


def _ffn_call(idx, slabs, gate, gt2, w_gate, w_up, w_down, cap):
    bsz, nrows, _ = slabs.shape
    n_exp, d, f = w_gate.shape
    assert bsz == 2, "ffn double-buffers exactly two samples per expert"
    kern = functools.partial(_ffn_kernel, cap=cap, n_exp=n_exp, d=d)
    any_spec = pl.BlockSpec(memory_space=pl.ANY)
    slab_buf = pltpu.VMEM((cap * SLAB_ROWS, LANES), F32)
    out = pl.pallas_call(
        kern,
        grid_spec=pltpu.PrefetchScalarGridSpec(
            num_scalar_prefetch=1,
            grid=(n_exp,),
            in_specs=[any_spec,
                      pl.BlockSpec((bsz, None, cap // LANES, LANES), lambda e, idx_ref: (0, e, 0, 0)),
                      pl.BlockSpec((bsz, 1, d), lambda e, idx_ref: (0, 0, 0)),
                      pl.BlockSpec((None, d, f), lambda e, idx_ref: (e, 0, 0)),
                      pl.BlockSpec((None, d, f), lambda e, idx_ref: (e, 0, 0)),
                      pl.BlockSpec((None, f, d), lambda e, idx_ref: (e, 0, 0))],
            out_specs=any_spec,
            scratch_shapes=[slab_buf, slab_buf,
                            pltpu.VMEM((d, f), BF16), pltpu.VMEM((d, f), BF16), pltpu.VMEM((f, d), BF16),
                            pltpu.SemaphoreType.DMA((bsz,)), pltpu.SemaphoreType.DMA((bsz,))],
        ),
        out_shape=jax.ShapeDtypeStruct((bsz * nrows, LANES), F32),
        input_output_aliases={1: 0},
        compiler_params=_cparams(("arbitrary",)),
        name="ffn",
    )(idx, slabs.reshape(bsz * nrows, LANES), gate, gt2, w_gate, w_up, w_down)
    return out.reshape(bsz, nrows, LANES)


def _final_kernel(x_ref, g_ref, o_ref):
    xf = jnp.concatenate([x_ref[:, r, :] for r in range(SUBLANES)], axis=1)
    o_ref[...] = xf * lax.rsqrt(jnp.mean(xf * xf, axis=-1, keepdims=True) + EPS) * g_ref[...]


def _final_call(slabs, g, tm=1024):
    bsz, nrows, _ = slabs.shape
    seq = nrows // SLAB_ROWS
    d = SUBLANES * LANES
    return pl.pallas_call(
        _final_kernel,
        grid=(bsz, seq // tm),
        in_specs=[pl.BlockSpec((None, tm, SUBLANES, LANES), lambda b, i: (b, i, 0, 0)),
                  pl.BlockSpec((1, d), lambda b, i: (0, 0))],
        out_specs=pl.BlockSpec((None, tm, d), lambda b, i: (b, i, 0)),
        out_shape=jax.ShapeDtypeStruct((bsz, seq, d), F32),
        compiler_params=_cparams(("arbitrary", "arbitrary")),
        name="final",
    )(slabs.reshape(bsz, seq, SLAB_ROWS, LANES), g)


def _rope_tables(seq):
    t = np.arange(seq)
    n_freq = HEAD_DIM // 4
    inv = ROPE_THETA ** (-np.arange(n_freq, dtype=np.float64) / n_freq)
    ang = np.concatenate([(t // GRID_W)[:, None] * inv, (t % GRID_W)[:, None] * inv], axis=1)
    cos_h = np.concatenate([np.cos(ang), np.cos(ang)], axis=1)
    sin_h = np.concatenate([-np.sin(ang), np.sin(ang)], axis=1)
    reps = LANES // HEAD_DIM
    return (jnp.asarray(np.tile(cos_h, (1, reps)), F32), jnp.asarray(np.tile(sin_h, (1, reps)), F32))


def kernel(x, c, ctx, c_ctx, w_ada, b_ada, g_mix, w_in, conv_w, sink, w_out, g_ffn,
           w_router, w_gate, w_up, w_down, g_final):
    bsz, seq, d = x.shape
    n_exp = w_router.shape[-1]
    cap = EC_CAPACITY_FACTOR * seq // n_exp
    assert w_ada.shape[0] == 1, "single trunk layer"
    assert bsz + 1 <= SUBLANES and seq % (SUBLANES * LANES) == 0 and cap % SLOT_BLOCK == 0

    cin = jnp.zeros((SUBLANES, d), F32).at[:bsz].set(c).at[bsz].set(c_ctx)
    mod = _mod_call(cin, w_ada[0], b_ada[0][None, :])
    sh1, sc1, gt1, sh2, sc2, gt2 = [m[:bsz, None, :] for m in jnp.split(mod, 6, axis=-1)]
    csh1 = mod[bsz:bsz + 1, 0:d]
    csc1 = mod[bsz:bsz + 1, d:2 * d]

    w_in0 = w_in[0]
    g_mix_r = g_mix[0][None, :]
    kv0 = ATT_WIDTH
    k_c, v_c = _ctx_call(ctx, g_mix_r, csh1, csc1, w_in0[:, kv0:kv0 + 2 * KV_WIDTH])

    cos_t, sin_t = _rope_tables(seq)
    q, k, v, bg, u = _in_call(x, g_mix_r, sh1, sc1, w_in0, cos_t, sin_t)
    att = _attn_call(sink[0], q, k, v, k_c, v_c)
    slabs, aff = _out_call(att, u, bg, conv_w[0], w_out[0], x, gt1,
                           g_ffn[0][None, :], sh2, sc2, w_router[0].T)

    n_chunk = seq // LANES
    aff4 = aff.reshape(bsz, n_exp, n_chunk, LANES)
    pos, carry = _sel_call(aff4, cap)
    n_tchunk = n_chunk // SUBLANES
    cum = jnp.concatenate([carry[:, :, ::SUBLANES, 0], jnp.full((bsz, n_exp, 1), cap, F32)], axis=-1)
    cum = cum.astype(jnp.int32).reshape(-1)
    comp = _compact_call(cum, pos.reshape(bsz, n_exp, n_tchunk, SUBLANES, LANES),
                         aff4.reshape(bsz, n_exp, n_tchunk, SUBLANES, LANES), cap)
    idx = (comp[:, :, :, 0, :] * LANES + comp[:, :, :, 1, :]).astype(jnp.int32)
    idx = idx + (jnp.arange(bsz, dtype=jnp.int32) * seq)[:, None, None, None]
    idx = (idx * SLAB_ROWS).reshape(-1)
    gate = (comp[:, :, :, 2, :] + comp[:, :, :, 3, :]) + comp[:, :, :, 4, :]

    slabs = _ffn_call(idx, slabs, gate, gt2, w_gate[0], w_up[0], w_down[0], cap)
    return _final_call(slabs, g_final[None, :])
```

```python
import functools

import numpy as np
import jax
import jax.numpy as jnp
from jax import lax
from jax.experimental import pallas as pl
from jax.experimental.pallas import tpu as pltpu

GRID_W = 64
N_HEADS = 8
N_KV_HEADS = 2
HEAD_DIM = 64
GQA_GROUP = N_HEADS // N_KV_HEADS
ATT_WIDTH = N_HEADS * HEAD_DIM
KV_WIDTH = N_KV_HEADS * HEAD_DIM
WINDOW = 128
BLOCK = 128
ROPE_THETA = 10000.0
N_EXPERTS = 16
EC_CAPACITY_FACTOR = 2
EPS = 1e-6

LANES = 128
SUBLANES = 8
MXU_COLS = 256
VMEM_LIMIT = 56 * 1024 * 1024

F32 = jnp.float32
BF16 = jnp.bfloat16
NEG = -1e30
LOG2E = 1.4426950408889634


def _cparams(sem):
    return pltpu.CompilerParams(dimension_semantics=sem, vmem_limit_bytes=VMEM_LIMIT)


def _dot(a, b):
    return jnp.dot(a, b, precision=lax.Precision.DEFAULT, preferred_element_type=F32)


def _dot_nt(a, b):
    return lax.dot_general(a, b, (((1,), (1,)), ((), ())), precision=lax.Precision.DEFAULT,
                           preferred_element_type=F32)


def _norm_mod(xf, g, shift, scale):
    r = xf * lax.rsqrt(jnp.mean(xf * xf, axis=-1, keepdims=True) + EPS)
    return (r * g) * (1.0 + scale) + shift


def _mod_kernel(c_ref, w_ref, b_ref, o_ref):
    c = c_ref[...]
    s = c * jax.nn.sigmoid(c)
    o_ref[...] = _dot(s, w_ref[...]) + b_ref[...]


def _mod_call(cin, w, bias, tn=1024):
    d, n = w.shape
    return pl.pallas_call(
        _mod_kernel,
        grid=(n // tn,),
        in_specs=[pl.BlockSpec((SUBLANES, d), lambda j: (0, 0)),
                  pl.BlockSpec((d, tn), lambda j: (0, j)),
                  pl.BlockSpec((1, tn), lambda j: (0, j))],
        out_specs=pl.BlockSpec((SUBLANES, tn), lambda j: (0, j)),
        out_shape=jax.ShapeDtypeStruct((SUBLANES, n), F32),
        compiler_params=_cparams(("arbitrary",)),
        name="mod",
    )(cin, w, bias)


def _ctx_kernel(x_ref, g_ref, sh_ref, sc_ref, w_ref, k_ref, v_ref):
    h = _norm_mod(x_ref[...], g_ref[...], sh_ref[...], sc_ref[...])
    kv = _dot(h, w_ref[...])
    k_ref[...] = kv[:, :KV_WIDTH]
    v_ref[...] = kv[:, KV_WIDTH:]


def _ctx_call(ctx, g, sh, sc, w_kv):
    bsz, n, d = ctx.shape
    vec = pl.BlockSpec((1, d), lambda b: (0, 0))
    kv_spec = pl.BlockSpec((None, n, KV_WIDTH), lambda b: (b, 0, 0))
    return pl.pallas_call(
        _ctx_kernel,
        grid=(bsz,),
        in_specs=[pl.BlockSpec((None, n, d), lambda b: (b, 0, 0)), vec, vec, vec,
                  pl.BlockSpec((d, 2 * KV_WIDTH), lambda b: (0, 0))],
        out_specs=[kv_spec, kv_spec],
        out_shape=[jax.ShapeDtypeStruct((bsz, n, KV_WIDTH), F32)] * 2,
        compiler_params=_cparams(("arbitrary",)),
        name="ctx_kv",
    )(ctx, g, sh, sc, w_kv)


def _rope(t, cos, sin_signed):
    lane = lax.broadcasted_iota(jnp.int32, t.shape, 1)
    first = (lane % HEAD_DIM) < (HEAD_DIM // 2)
    swapped = jnp.where(first, pltpu.roll(t, LANES - HEAD_DIM // 2, 1), pltpu.roll(t, HEAD_DIM // 2, 1))
    return t * cos + swapped * sin_signed


def _in_kernel(x_ref, g_ref, sh_ref, sc_ref, w_ref, cos_ref, sin_ref,
               q_ref, k_ref, v_ref, bg_ref, u_ref, *, conv_width):
    h = _norm_mod(x_ref[...], g_ref[...], sh_ref[...], sc_ref[...])
    cos = cos_ref[...]
    sin = sin_ref[...]
    scale = HEAD_DIM ** -0.5 * LOG2E
    for j in range(ATT_WIDTH // MXU_COLS):
        qj = _dot(h, w_ref[:, j * MXU_COLS:(j + 1) * MXU_COLS])
        for jj in range(MXU_COLS // LANES):
            c0 = j * MXU_COLS + jj * LANES
            q_ref[:, c0:c0 + LANES] = _rope(qj[:, jj * LANES:(jj + 1) * LANES], cos, sin) * scale
    o = ATT_WIDTH
    kv = _dot(h, w_ref[:, o:o + 2 * KV_WIDTH])
    k_ref[...] = _rope(kv[:, :KV_WIDTH], cos, sin)
    v_ref[...] = kv[:, KV_WIDTH:]
    o += 2 * KV_WIDTH
    bg_ref[...] = _dot(h, w_ref[:, o:o + conv_width])
    o += conv_width
    cg = _dot(h, w_ref[:, o:o + conv_width])
    o += conv_width
    hv = _dot(h, w_ref[:, o:o + conv_width])
    u_ref[...] = cg * hv


def _in_call(x, g, sh, sc, w_in, cos_t, sin_t, tm=1024):
    bsz, seq, d = x.shape
    n_in = w_in.shape[1]
    cw = d - ATT_WIDTH
    vec = pl.BlockSpec((1, d), lambda b, i: (0, 0))
    bvec = pl.BlockSpec((None, 1, d), lambda b, i: (b, 0, 0))
    tab = pl.BlockSpec((tm, LANES), lambda b, i: (i, 0))

    def rows(width):
        return pl.BlockSpec((None, tm, width), lambda b, i: (b, i, 0))

    return pl.pallas_call(
        functools.partial(_in_kernel, conv_width=cw),
        grid=(bsz, seq // tm),
        in_specs=[rows(d), vec, bvec, bvec, pl.BlockSpec((d, n_in), lambda b, i: (0, 0)), tab, tab],
        out_specs=[rows(ATT_WIDTH), rows(KV_WIDTH), rows(KV_WIDTH), rows(cw), rows(cw)],
        out_shape=[jax.ShapeDtypeStruct((bsz, seq, ATT_WIDTH), F32),
                   jax.ShapeDtypeStruct((bsz, seq, KV_WIDTH), F32),
                   jax.ShapeDtypeStruct((bsz, seq, KV_WIDTH), F32),
                   jax.ShapeDtypeStruct((bsz, seq, cw), F32),
                   jax.ShapeDtypeStruct((bsz, seq, cw), F32)],
        compiler_params=_cparams(("arbitrary", "arbitrary")),
        name="in_proj",
    )(x, g, sh, sc, w_in, cos_t, sin_t)


ATTN_TILE = 32


def _attn_kernel(sink_ref, q_ref, k_ref, v_ref, kc_ref, vc_ref, o_ref,
                 s_scr, p_scr, bias_scr, m_scr, *, seq, tq):
    i = pl.program_id(1)
    nblk = tq // BLOCK
    span = 3 * BLOCK
    width = s_scr.shape[1]
    rows_all = GQA_GROUP * BLOCK
    tiles_per_head = BLOCK // ATTN_TILE
    kc = kc_ref[...]
    vc = vc_ref[...]
    for blk in range(nblk):
        n = i * nblk + blk
        ws = pl.multiple_of(jnp.clip((n - 1) * BLOCK, 0, seq - span), BLOCK)
        kw = k_ref[pl.ds(ws, span), :]
        vw = v_ref[pl.ds(ws, span), :]
        qb = q_ref[blk * BLOCK:(blk + 1) * BLOCK, :]
        qpos = n * BLOCK + lax.broadcasted_iota(jnp.int32, (BLOCK, span), 0)
        kpos = ws + lax.broadcasted_iota(jnp.int32, (BLOCK, span), 1)
        bias_scr[...] = jnp.where(jnp.abs(kpos - qpos) <= WINDOW, 0.0, NEG)
        outs = []
        for kh in range(N_KV_HEADS):
            heads = [kh * GQA_GROUP + g for g in range(GQA_GROUP)]
            qs = jnp.concatenate([qb[:, h * HEAD_DIM:(h + 1) * HEAD_DIM] for h in heads], axis=0)
            lo, hi = kh * HEAD_DIM, (kh + 1) * HEAD_DIM
            s_scr[:, 0:span] = _dot_nt(qs, kw[:, lo:hi])
            s_scr[:, span:width] = _dot_nt(qs, kc[:, lo:hi])

            for t in range(rows_all // ATTN_TILE):
                r0 = t * ATTN_TILE
                b0 = (t % tiles_per_head) * ATTN_TILE
                sk = sink_ref[kh * GQA_GROUP + t // tiles_per_head] * LOG2E
                s_loc = s_scr[pl.ds(r0, ATTN_TILE), 0:span] + bias_scr[pl.ds(b0, ATTN_TILE), :]
                s_ctx = s_scr[pl.ds(r0, ATTN_TILE), span:width]
                m = jnp.maximum(jnp.maximum(jnp.max(s_loc, axis=1, keepdims=True),
                                            jnp.max(s_ctx, axis=1, keepdims=True)), sk)
                m_scr[pl.ds(r0, ATTN_TILE), :] = m
                p_scr[pl.ds(r0, ATTN_TILE), 0:span] = jnp.exp2(s_loc - m)
                p_scr[pl.ds(r0, ATTN_TILE), span:width] = jnp.exp2(s_ctx - m)
            own = (lax.broadcasted_iota(jnp.int32, (1, KV_WIDTH), 1) // HEAD_DIM) == kh
            o_ext = (_dot(p_scr[:, 0:span], jnp.where(own, vw, 1.0))
                     + _dot(p_scr[:, span:width], jnp.where(own, vc, 1.0)))
            other = (1 - kh) * HEAD_DIM
            sk_col = jnp.concatenate([jnp.full((BLOCK, 1), sink_ref[h] * LOG2E, F32) for h in heads], axis=0)
            den = o_ext[:, other:other + 1] + jnp.exp2(sk_col - m_scr[...])
            o = o_ext[:, lo:hi] / den
            outs.extend(o[g * BLOCK:(g + 1) * BLOCK] for g in range(GQA_GROUP))
        o_ref[blk * BLOCK:(blk + 1) * BLOCK, :] = jnp.concatenate(outs, axis=1)


def _attn_call(sink, q, k, v, kc, vc, tq=1024):
    bsz, seq, _ = q.shape
    n_ctx = kc.shape[1]
    span = 3 * BLOCK
    rows_all = GQA_GROUP * BLOCK
    full_kv = pl.BlockSpec((None, seq, KV_WIDTH), lambda b, i: (b, 0, 0))
    ctx_kv = pl.BlockSpec((None, n_ctx, KV_WIDTH), lambda b, i: (b, 0, 0))
    qspec = pl.BlockSpec((None, tq, ATT_WIDTH), lambda b, i: (b, i, 0))
    return pl.pallas_call(
        functools.partial(_attn_kernel, seq=seq, tq=tq),
        grid=(bsz, seq // tq),
        in_specs=[pl.BlockSpec(memory_space=pltpu.SMEM), qspec, full_kv, full_kv, ctx_kv, ctx_kv],
        out_specs=qspec,
        out_shape=jax.ShapeDtypeStruct((bsz, seq, ATT_WIDTH), F32),
        scratch_shapes=[pltpu.VMEM((rows_all, span + n_ctx), F32),
                        pltpu.VMEM((rows_all, span + n_ctx), F32),
                        pltpu.VMEM((BLOCK, span), F32),
                        pltpu.VMEM((rows_all, 1), F32)],
        compiler_params=_cparams(("arbitrary", "arbitrary")),
        name="attn",
    )(sink, q, k, v, kc, vc)


def _out_kernel(att_ref, u_ref, up_ref, un_ref, bg_ref, cw_ref, wo_ref, x_ref, gt_ref,
                g_ref, sh_ref, sc_ref, wr_ref, rows_ref, aff_ref, *, tm):
    i = pl.program_id(1)
    last = pl.num_programs(1) - 1
    u = u_ref[...]
    row = lax.broadcasted_iota(jnp.int32, u.shape, 0)
    prev_row = jnp.where(i > 0, up_ref[SUBLANES - 1:SUBLANES, :], 0.0)
    next_row = jnp.where(i < last, un_ref[0:1, :], 0.0)
    u_prev = jnp.where(row == 0, prev_row, pltpu.roll(u, 1, 0))
    u_next = jnp.where(row == tm - 1, next_row, pltpu.roll(u, tm - 1, 0))
    conv = bg_ref[...] * (cw_ref[0:1, :] * u_prev + cw_ref[1:2, :] * u + cw_ref[2:3, :] * u_next)
    y = _dot(att_ref[...], wo_ref[0:ATT_WIDTH, :]) + _dot(conv, wo_ref[ATT_WIDTH:, :])
    x1 = x_ref[...] + gt_ref[...] * y
    d = x1.shape[-1]
    h2 = _norm_mod(x1, g_ref[...], sh_ref[...], sc_ref[...])
    rows_ref[:, 0:d] = x1
    rows_ref[:, d:2 * d] = h2
    def split3(t):
        hi = t.astype(BF16)
        r1 = t - hi.astype(F32)
        mid = r1.astype(BF16)
        return hi, mid, (r1 - mid.astype(F32)).astype(BF16)

    def nt_bf16(a, b):
        return lax.dot_general(a, b, (((1,), (1,)), ((), ())), preferred_element_type=F32)

    h_hi, h_mid, h_lo = split3(h2)
    w_hi, w_mid, w_lo = split3(wr_ref[...])
    n_exp = w_hi.shape[0]
    by_hi = nt_bf16(jnp.concatenate([w_hi, w_mid, w_lo], axis=0), h_hi)
    by_mid = nt_bf16(jnp.concatenate([w_hi, w_mid], axis=0), h_mid)
    logits = (((by_hi[2 * n_exp:] + by_mid[n_exp:]) + nt_bf16(w_hi, h_lo))
              + (by_hi[n_exp:2 * n_exp] + by_mid[:n_exp])) + by_hi[:n_exp]
    mx = jnp.max(logits, axis=0, keepdims=True)
    ex = jnp.exp(logits - mx)
    aff_ref[...] = ex / jnp.sum(ex, axis=0, keepdims=True)


def _out_call(att, u, bg, conv_w, w_out, x, gt1, g_ffn, sh2, sc2, w_router_t, tm=1024):
    bsz, seq, d = x.shape
    cw = u.shape[-1]
    n_exp = w_router_t.shape[0]
    nh = seq // SUBLANES

    def rows(width):
        return pl.BlockSpec((None, tm, width), lambda b, i: (b, i, 0))

    halo_prev = pl.BlockSpec((None, SUBLANES, cw),
                             lambda b, i: (b, jnp.maximum(i * (tm // SUBLANES) - 1, 0), 0))
    halo_next = pl.BlockSpec((None, SUBLANES, cw),
                             lambda b, i: (b, jnp.minimum((i + 1) * (tm // SUBLANES), nh - 1), 0))
    vec = pl.BlockSpec((1, d), lambda b, i: (0, 0))
    bvec = pl.BlockSpec((None, 1, d), lambda b, i: (b, 0, 0))
    return pl.pallas_call(
        functools.partial(_out_kernel, tm=tm),
        grid=(bsz, seq // tm),
        in_specs=[rows(ATT_WIDTH), rows(cw), halo_prev, halo_next, rows(cw),
                  pl.BlockSpec((3, cw), lambda b, i: (0, 0)),
                  pl.BlockSpec((d, d), lambda b, i: (0, 0)),
                  rows(d), bvec, vec, bvec, bvec,
                  pl.BlockSpec((n_exp, d), lambda b, i: (0, 0))],
        out_specs=[rows(2 * d), pl.BlockSpec((None, n_exp, tm), lambda b, i: (b, 0, i))],
        out_shape=[jax.ShapeDtypeStruct((bsz, seq, 2 * d), F32),
                   jax.ShapeDtypeStruct((bsz, n_exp, seq), F32)],
        compiler_params=_cparams(("arbitrary", "arbitrary")),
        name="out_proj",
    )(att, u, u, u, bg, conv_w, w_out, x, gt1, g_ffn, sh2, sc2, w_router_t)


def _prefix_rows(m, tri, ones, low):
    n_exp, n_chunk, _ = m.shape
    mb = m.astype(BF16).reshape(n_exp * n_chunk, LANES)
    within = jnp.dot(mb, tri, preferred_element_type=F32).reshape(n_exp, n_chunk, LANES)
    tot = jnp.dot(mb, ones, preferred_element_type=F32).astype(BF16).reshape(n_exp, n_chunk, LANES)
    carry = jnp.stack([jnp.dot(low, tot[e], preferred_element_type=F32) for e in range(n_exp)], axis=0)
    return carry + within, carry


def _sel_kernel(aff_ref, pos_ref, carry_ref, *, cap):
    a = aff_ref[...]
    n_exp, n_chunk, _ = a.shape
    bits = lax.bitcast_convert_type(a, jnp.int32)

    def count(pred):
        c = jnp.sum(pred.astype(F32), axis=1, keepdims=True)
        return jnp.sum(c, axis=2, keepdims=True)

    def body(it, thr):
        shift = 29 - 2 * it
        for k in (1, 2, 3):
            cand = thr | jnp.left_shift(jnp.int32(k), shift)
            ok = count(bits >= cand) >= cap
            best = jnp.where(ok, cand, thr) if k == 1 else jnp.where(ok, cand, best)
        return best

    thr = lax.fori_loop(0, 15, body, jnp.zeros((n_exp, 1, 1), jnp.int32))
    thr = jnp.where(count(bits >= (thr | 1)) >= cap, thr | 1, thr)
    gt = bits > thr
    eq = bits == thr
    need = cap - count(gt)

    r = lax.broadcasted_iota(jnp.int32, (LANES, LANES), 0)
    c = lax.broadcasted_iota(jnp.int32, (LANES, LANES), 1)
    tri = (r < c).astype(BF16)
    ones = jnp.ones((LANES, LANES), BF16)
    rr = lax.broadcasted_iota(jnp.int32, (n_chunk, n_chunk), 0)
    cc = lax.broadcasted_iota(jnp.int32, (n_chunk, n_chunk), 1)
    low = (cc < rr).astype(BF16)

    eq_rank, _ = _prefix_rows(eq, tri, ones, low)
    sel = gt | (eq & (eq_rank < need))
    pos, carry = _prefix_rows(sel, tri, ones, low)
    pos_ref[...] = jnp.where(sel, pos, -1.0)
    carry_ref[...] = carry


def _sel_call(aff4, cap):
    bsz, n_exp, n_chunk, _ = aff4.shape
    spec = pl.BlockSpec((None, n_exp, n_chunk, LANES), lambda b: (b, 0, 0, 0))
    return pl.pallas_call(
        functools.partial(_sel_kernel, cap=cap),
        grid=(bsz,),
        in_specs=[spec],
        out_specs=[spec, spec],
        out_shape=[jax.ShapeDtypeStruct(aff4.shape, F32)] * 2,
        compiler_params=_cparams(("arbitrary",)),
        name="select",
    )(aff4)


COMPACT_ROWS = 16
SLOT_BLOCK = LANES


def _compact_kernel(cum_ref, pos_ref, aff_ref, out_ref, *, n_exp, n_tchunk):
    b = pl.program_id(0)
    e = pl.program_id(1)
    base = (b * n_exp + e) * (n_tchunk + 1)
    out_ref[...] = jnp.zeros(out_ref.shape, F32)
    nt = (((1,), (1,)), ((), ()))
    rowi = lax.broadcasted_iota(jnp.int32, (COMPACT_ROWS, LANES), 0)
    lane = lax.broadcasted_iota(jnp.int32, (COMPACT_ROWS, LANES), 1).astype(F32)
    slot_b = lax.broadcasted_iota(jnp.int32, (SLOT_BLOCK, LANES), 0).astype(F32).astype(BF16)
    one_b = jnp.ones((), BF16)
    zero_b = jnp.zeros((), BF16)

    n_sblk = out_ref.shape[0] - 1

    def chunk_vals(c):
        a = aff_ref[c]
        g_hi = a.astype(BF16).astype(F32)
        r1 = a - g_hi
        g_mid = r1.astype(BF16).astype(F32)
        g_lo = r1 - g_mid
        pieces = []
        for r in range(SUBLANES):
            def bc(t):
                return jnp.broadcast_to(t[r:r + 1, :], (COMPACT_ROWS, LANES))

            piece = jnp.where(rowi == 0, float(c * SUBLANES + r),
                              jnp.where(rowi == 1, lane,
                                        jnp.where(rowi == 2, bc(g_hi),
                                                  jnp.where(rowi == 3, bc(g_mid),
                                                            jnp.where(rowi == 4, bc(g_lo), 0.0)))))
            pieces.append(piece)
        return jnp.concatenate(pieces, axis=1).astype(BF16)

    def visit(j, p, vals):
        rel = p - jnp.asarray(j * SLOT_BLOCK).astype(F32)
        rel = jnp.where((rel >= 0.0) & (rel < SLOT_BLOCK), rel, -1.0)
        pieces_oh = []
        for r in range(SUBLANES):
            row16 = jnp.broadcast_to(rel[r:r + 1, :], (2 * SUBLANES, LANES)).astype(BF16)
            blk = jnp.concatenate([row16] * (SLOT_BLOCK // (2 * SUBLANES)), axis=0)
            pieces_oh.append(jnp.where(blk == slot_b, one_b, zero_b))
        oh = jnp.concatenate(pieces_oh, axis=1)
        out_ref[j] += lax.dot_general(vals, oh, nt, preferred_element_type=F32)

    def first_block(c):
        return jnp.minimum(cum_ref[base + c] // SLOT_BLOCK, n_sblk - 1)

    for c in range(n_tchunk):
        p = pos_ref[c]
        vals = chunk_vals(c)
        j0 = first_block(c)
        visit(j0, p, vals)
        visit(j0 + 1, p, vals)

    for c in range(n_tchunk):
        lo = cum_ref[base + c]
        hi = cum_ref[base + c + 1]
        j_end = jnp.where(hi > lo, (hi - 1) // SLOT_BLOCK + 1, 0)

        def more(j, carry, c=c):
            visit(j, pos_ref[c], chunk_vals(c))
            return carry

        lax.fori_loop(first_block(c) + 2, j_end, more, 0)


def _compact_call(cum, pos5, aff5, cap):
    bsz, n_exp, n_tchunk, _, _ = pos5.shape
    n_sblk = cap // SLOT_BLOCK
    spec = pl.BlockSpec((None, None, n_tchunk, SUBLANES, LANES), lambda b, e, cum_ref: (b, e, 0, 0, 0))
    return pl.pallas_call(
        functools.partial(_compact_kernel, n_exp=n_exp, n_tchunk=n_tchunk),
        grid_spec=pltpu.PrefetchScalarGridSpec(
            num_scalar_prefetch=1,
            grid=(bsz, n_exp),
            in_specs=[spec, spec],
            out_specs=pl.BlockSpec((None, None, n_sblk + 1, COMPACT_ROWS, LANES),
                                   lambda b, e, cum_ref: (b, e, 0, 0, 0)),
        ),
        out_shape=jax.ShapeDtypeStruct((bsz, n_exp, n_sblk + 1, COMPACT_ROWS, LANES), F32),
        compiler_params=_cparams(("arbitrary", "arbitrary")),
        name="compact",
    )(cum, pos5, aff5)[:, :, :n_sblk]


FFN_ROWS = 256


FFN_COLS = 256


def _ffn_kernel(idx_ref, rows_in, g_ref, gt_ref, wg_ref, wu_ref, wd_ref, rows_hbm,
                buf0, buf1, wgb, wub, wdb, gsem, ssem, *, cap, n_exp, d):
    del rows_in
    e = pl.program_id(0)
    bufs = (buf0, buf1)
    n_chunk = cap // FFN_ROWS
    f = wgb.shape[1]
    acc_cols = pl.ds(0, d)

    def hbm_row(ee, b, s):
        return idx_ref[(b * n_exp + ee) * cap + s]

    def dma_thread(s):
        return s % 2 if isinstance(s, int) else 0

    def gather_start(ee, b, s):
        pltpu.make_async_copy(rows_hbm.at[pl.ds(hbm_row(ee, b, s), 1)], bufs[b].at[pl.ds(s, 1)],
                              gsem.at[b]).start(priority=dma_thread(s))

    def gather_wait(b):
        pltpu.make_async_copy(rows_hbm.at[pl.ds(0, cap)], bufs[b], gsem.at[b]).wait()

    def scatter_start(ee, b, s):
        pltpu.make_async_copy(bufs[b].at[pl.ds(s, 1), acc_cols],
                              rows_hbm.at[pl.ds(hbm_row(ee, b, s), 1), acc_cols],
                              ssem.at[b]).start(priority=dma_thread(s))

    def scatter_wait(b):
        pltpu.make_async_copy(bufs[b].at[:, acc_cols], rows_hbm.at[pl.ds(0, cap), acc_cols],
                              ssem.at[b]).wait()

    def compute_chunk(b, ci, todo):
        buf = bufs[b]
        rows = slice(ci * FFN_ROWS, (ci + 1) * FFN_ROWS)
        n_piece = 3 * (f // FFN_COLS)
        per_piece = -(-len(todo) // n_piece)

        def issue_some():
            for _ in range(min(per_piece, len(todo))):
                todo.pop(0)()

        xc = buf[rows, d:2 * d].astype(BF16)
        g_col = jnp.concatenate(
            [jnp.broadcast_to(g_ref[b, j:j + 1, :], (LANES, LANES)).T
             for j in range(ci * FFN_ROWS // LANES, (ci + 1) * FFN_ROWS // LANES)], axis=0)
        g_col = jnp.tile(g_col, (1, FFN_COLS // LANES))
        gt = gt_ref[b]
        hm = []
        for nj in range(f // FFN_COLS):
            cols = slice(nj * FFN_COLS, (nj + 1) * FFN_COLS)
            a = jnp.dot(xc, wgb[:, cols], preferred_element_type=F32)
            issue_some()
            u = jnp.dot(xc, wub[:, cols], preferred_element_type=F32)
            issue_some()
            hm.append((a * jax.nn.sigmoid(a) * u).astype(BF16))
        hm = jnp.concatenate(hm, axis=1)
        for nj in range(d // FFN_COLS):
            cols = slice(nj * FFN_COLS, (nj + 1) * FFN_COLS)
            y = jnp.dot(hm, wdb[:, cols], preferred_element_type=F32) * g_col
            buf[rows, cols] = buf[rows, cols] + gt[:, cols] * y
            issue_some()
        assert not todo

    def chunk_rows(ci):
        return range(ci * FFN_ROWS, (ci + 1) * FFN_ROWS)

    def split(n, parts):
        bounds = [n * k // parts for k in range(parts + 1)]
        return [range(bounds[k], bounds[k + 1]) for k in range(parts)]

    @pl.when(e == 0)
    def _():
        def first(s, carry):
            gather_start(0, 0, s)
            return carry
        lax.fori_loop(0, cap, first, 0)

    @pl.when(e > 0)
    def _():
        scatter_wait(1)

    wgb[...] = wg_ref[...].astype(BF16)
    wub[...] = wu_ref[...].astype(BF16)
    wdb[...] = wd_ref[...].astype(BF16)

    gather_wait(0)
    for ci in range(n_chunk):
        todo = [functools.partial(gather_start, e, 1, s) for s in chunk_rows(ci)]
        if ci > 0:
            todo += [functools.partial(scatter_start, e, 0, s) for s in chunk_rows(ci - 1)]
        compute_chunk(0, ci, todo)

    ne = jnp.minimum(e + 1, n_exp - 1)
    gather_wait(1)
    compute_chunk(1, 0, [functools.partial(scatter_start, e, 0, s) for s in chunk_rows(n_chunk - 1)])
    scatter_wait(0)
    for ci, part in zip(range(1, n_chunk), split(cap, n_chunk - 1)):
        todo = [functools.partial(gather_start, ne, 0, s) for s in part]
        todo += [functools.partial(scatter_start, e, 1, s) for s in chunk_rows(ci - 1)]
        compute_chunk(1, ci, todo)
    for s in chunk_rows(n_chunk - 1):
        scatter_start(e, 1, s)

    @pl.when(e == n_exp - 1)
    def _():
        scatter_wait(1)
        gather_wait(0)


def _ffn_call(idx, rows, gate, gt2, w_gate, w_up, w_down, cap):
    bsz, seq, d2 = rows.shape
    d = d2 // 2
    n_exp, _, f = w_gate.shape
    assert bsz == 2, "ffn double-buffers exactly two samples per expert"
    kern = functools.partial(_ffn_kernel, cap=cap, n_exp=n_exp, d=d)
    any_spec = pl.BlockSpec(memory_space=pl.ANY)
    out = pl.pallas_call(
        kern,
        grid_spec=pltpu.PrefetchScalarGridSpec(
            num_scalar_prefetch=1,
            grid=(n_exp,),
            in_specs=[any_spec,
                      pl.BlockSpec((bsz, None, cap // LANES, LANES), lambda e, idx_ref: (0, e, 0, 0)),
                      pl.BlockSpec((bsz, 1, d), lambda e, idx_ref: (0, 0, 0)),
                      pl.BlockSpec((None, d, f), lambda e, idx_ref: (e, 0, 0)),
                      pl.BlockSpec((None, d, f), lambda e, idx_ref: (e, 0, 0)),
                      pl.BlockSpec((None, f, d), lambda e, idx_ref: (e, 0, 0))],
            out_specs=any_spec,
            scratch_shapes=[pltpu.VMEM((cap, d2), F32), pltpu.VMEM((cap, d2), F32),
                            pltpu.VMEM((d, f), BF16), pltpu.VMEM((d, f), BF16), pltpu.VMEM((f, d), BF16),
                            pltpu.SemaphoreType.DMA((bsz,)), pltpu.SemaphoreType.DMA((bsz,))],
        ),
        out_shape=jax.ShapeDtypeStruct((bsz * seq, d2), F32),
        input_output_aliases={1: 0},
        compiler_params=_cparams(("arbitrary",)),
        name="ffn",
    )(idx, rows.reshape(bsz * seq, d2), gate, gt2, w_gate, w_up, w_down)
    return out.reshape(bsz, seq, d2)


def _final_kernel(x_ref, g_ref, o_ref):
    xf = x_ref[...]
    o_ref[...] = xf * lax.rsqrt(jnp.mean(xf * xf, axis=-1, keepdims=True) + EPS) * g_ref[...]


def _final_call(rows, g, tm=2048):
    bsz, seq, d2 = rows.shape
    d = d2 // 2
    return pl.pallas_call(
        _final_kernel,
        grid=(bsz, seq // tm),
        in_specs=[pl.BlockSpec((None, tm, d), lambda b, i: (b, i, 0)),
                  pl.BlockSpec((1, d), lambda b, i: (0, 0))],
        out_specs=pl.BlockSpec((None, tm, d), lambda b, i: (b, i, 0)),
        out_shape=jax.ShapeDtypeStruct((bsz, seq, d), F32),
        compiler_params=_cparams(("arbitrary", "arbitrary")),
        name="final",
    )(rows, g)


def _rope_tables(seq):
    t = np.arange(seq)
    n_freq = HEAD_DIM // 4
    inv = ROPE_THETA ** (-np.arange(n_freq, dtype=np.float64) / n_freq)
    ang = np.concatenate([(t // GRID_W)[:, None] * inv, (t % GRID_W)[:, None] * inv], axis=1)
    cos_h = np.concatenate([np.cos(ang), np.cos(ang)], axis=1)
    sin_h = np.concatenate([-np.sin(ang), np.sin(ang)], axis=1)
    reps = LANES // HEAD_DIM
    return (jnp.asarray(np.tile(cos_h, (1, reps)), F32), jnp.asarray(np.tile(sin_h, (1, reps)), F32))


def kernel(x, c, ctx, c_ctx, w_ada, b_ada, g_mix, w_in, conv_w, sink, w_out, g_ffn,
           w_router, w_gate, w_up, w_down, g_final):
    bsz, seq, d = x.shape
    n_exp = w_router.shape[-1]
    cap = EC_CAPACITY_FACTOR * seq // n_exp
    assert w_ada.shape[0] == 1, "single trunk layer"
    assert bsz + 1 <= SUBLANES and seq % (SUBLANES * LANES) == 0 and cap % SLOT_BLOCK == 0

    cin = jnp.zeros((SUBLANES, d), F32).at[:bsz].set(c).at[bsz].set(c_ctx)
    mod = _mod_call(cin, w_ada[0], b_ada[0][None, :])
    sh1, sc1, gt1, sh2, sc2, gt2 = [m[:bsz, None, :] for m in jnp.split(mod, 6, axis=-1)]
    csh1 = mod[bsz:bsz + 1, 0:d]
    csc1 = mod[bsz:bsz + 1, d:2 * d]

    w_in0 = w_in[0]
    g_mix_r = g_mix[0][None, :]
    kv0 = ATT_WIDTH
    k_c, v_c = _ctx_call(ctx, g_mix_r, csh1, csc1, w_in0[:, kv0:kv0 + 2 * KV_WIDTH])

    cos_t, sin_t = _rope_tables(seq)
    q, k, v, bg, u = _in_call(x, g_mix_r, sh1, sc1, w_in0, cos_t, sin_t)
    att = _attn_call(sink[0], q, k, v, k_c, v_c)
    rows, aff = _out_call(att, u, bg, conv_w[0], w_out[0], x, gt1,
                          g_ffn[0][None, :], sh2, sc2, w_router[0].T)

    n_chunk = seq // LANES
    aff4 = aff.reshape(bsz, n_exp, n_chunk, LANES)
    pos, carry = _sel_call(aff4, cap)
    n_tchunk = n_chunk // SUBLANES
    cum = jnp.concatenate([carry[:, :, ::SUBLANES, 0], jnp.full((bsz, n_exp, 1), cap, F32)], axis=-1)
    cum = cum.astype(jnp.int32).reshape(-1)
    comp = _compact_call(cum, pos.reshape(bsz, n_exp, n_tchunk, SUBLANES, LANES),
                         aff4.reshape(bsz, n_exp, n_tchunk, SUBLANES, LANES), cap)
    idx = (comp[:, :, :, 0, :] * LANES + comp[:, :, :, 1, :]).astype(jnp.int32)
    idx = (idx + (jnp.arange(bsz, dtype=jnp.int32) * seq)[:, None, None, None]).reshape(-1)
    gate = (comp[:, :, :, 2, :] + comp[:, :, :, 3, :]) + comp[:, :, :, 4, :]

    rows = _ffn_call(idx, rows, gate, gt2, w_gate[0], w_up[0], w_down[0], cap)
    return _final_call(rows, g_final[None, :])
```

```python
import functools

import numpy as np
import jax
import jax.numpy as jnp
from jax import lax
from jax.experimental import pallas as pl
from jax.experimental.pallas import tpu as pltpu

GRID_W = 64
N_HEADS = 8
N_KV_HEADS = 2
HEAD_DIM = 64
GQA_GROUP = N_HEADS // N_KV_HEADS
ATT_WIDTH = N_HEADS * HEAD_DIM
KV_WIDTH = N_KV_HEADS * HEAD_DIM
WINDOW = 128
BLOCK = 128
ROPE_THETA = 10000.0
N_EXPERTS = 16
EC_CAPACITY_FACTOR = 2
EPS = 1e-6

LANES = 128
SUBLANES = 8
MXU_COLS = 256
VMEM_LIMIT = 56 * 1024 * 1024

F32 = jnp.float32
BF16 = jnp.bfloat16
NEG = -1e30
LOG2E = 1.4426950408889634


def _cparams(sem):
    return pltpu.CompilerParams(dimension_semantics=sem, vmem_limit_bytes=VMEM_LIMIT)


def _dot(a, b):
    return jnp.dot(a, b, precision=lax.Precision.DEFAULT, preferred_element_type=F32)


def _dot_nt(a, b):
    return lax.dot_general(a, b, (((1,), (1,)), ((), ())), precision=lax.Precision.DEFAULT,
                           preferred_element_type=F32)


def _norm_mod(xf, g, shift, scale):
    r = xf * lax.rsqrt(jnp.mean(xf * xf, axis=-1, keepdims=True) + EPS)
    return (r * g) * (1.0 + scale) + shift


def _mod_kernel(c_ref, w_ref, b_ref, o_ref):
    c = c_ref[...]
    s = c * jax.nn.sigmoid(c)
    o_ref[...] = _dot(s, w_ref[...]) + b_ref[...]


def _mod_call(cin, w, bias, tn=1024):
    d, n = w.shape
    return pl.pallas_call(
        _mod_kernel,
        grid=(n // tn,),
        in_specs=[pl.BlockSpec((SUBLANES, d), lambda j: (0, 0)),
                  pl.BlockSpec((d, tn), lambda j: (0, j)),
                  pl.BlockSpec((1, tn), lambda j: (0, j))],
        out_specs=pl.BlockSpec((SUBLANES, tn), lambda j: (0, j)),
        out_shape=jax.ShapeDtypeStruct((SUBLANES, n), F32),
        compiler_params=_cparams(("arbitrary",)),
        name="mod",
    )(cin, w, bias)


def _ctx_kernel(x_ref, g_ref, sh_ref, sc_ref, w_ref, k_ref, v_ref):
    h = _norm_mod(x_ref[...], g_ref[...], sh_ref[...], sc_ref[...])
    kv = _dot(h, w_ref[...])
    k_ref[...] = kv[:, :KV_WIDTH]
    v_ref[...] = kv[:, KV_WIDTH:]


def _ctx_call(ctx, g, sh, sc, w_kv):
    bsz, n, d = ctx.shape
    vec = pl.BlockSpec((1, d), lambda b: (0, 0))
    kv_spec = pl.BlockSpec((None, n, KV_WIDTH), lambda b: (b, 0, 0))
    return pl.pallas_call(
        _ctx_kernel,
        grid=(bsz,),
        in_specs=[pl.BlockSpec((None, n, d), lambda b: (b, 0, 0)), vec, vec, vec,
                  pl.BlockSpec((d, 2 * KV_WIDTH), lambda b: (0, 0))],
        out_specs=[kv_spec, kv_spec],
        out_shape=[jax.ShapeDtypeStruct((bsz, n, KV_WIDTH), F32)] * 2,
        compiler_params=_cparams(("arbitrary",)),
        name="ctx_kv",
    )(ctx, g, sh, sc, w_kv)


def _rope(t, cos, sin_signed):
    lane = lax.broadcasted_iota(jnp.int32, t.shape, 1)
    first = (lane % HEAD_DIM) < (HEAD_DIM // 2)
    swapped = jnp.where(first, pltpu.roll(t, LANES - HEAD_DIM // 2, 1), pltpu.roll(t, HEAD_DIM // 2, 1))
    return t * cos + swapped * sin_signed


def _in_kernel(x_ref, xp_ref, xn_ref, g_ref, sh_ref, sc_ref, w_ref, cw_ref, cos_ref, sin_ref,
               q_ref, k_ref, v_ref, conv_ref, *, conv_width, tm):
    i = pl.program_id(1)
    last = pl.num_programs(1) - 1
    x_ext = jnp.concatenate([x_ref[...], xp_ref[...], xn_ref[...]], axis=0)
    h_ext = _norm_mod(x_ext, g_ref[...], sh_ref[...], sc_ref[...])
    h = h_ext[0:tm]
    cos = cos_ref[...]
    sin = sin_ref[...]
    scale = HEAD_DIM ** -0.5 * LOG2E
    for j in range(ATT_WIDTH // MXU_COLS):
        qj = _dot(h, w_ref[:, j * MXU_COLS:(j + 1) * MXU_COLS])
        for jj in range(MXU_COLS // LANES):
            c0 = j * MXU_COLS + jj * LANES
            q_ref[:, c0:c0 + LANES] = _rope(qj[:, jj * LANES:(jj + 1) * LANES], cos, sin) * scale
    o = ATT_WIDTH
    kv = _dot(h, w_ref[:, o:o + 2 * KV_WIDTH])
    k_ref[...] = _rope(kv[:, :KV_WIDTH], cos, sin)
    v_ref[...] = kv[:, KV_WIDTH:]
    o += 2 * KV_WIDTH
    bg = _dot(h, w_ref[:, o:o + conv_width])
    o += conv_width
    cg = _dot(h_ext, w_ref[:, o:o + conv_width])
    o += conv_width
    hv = _dot(h_ext, w_ref[:, o:o + conv_width])
    u_ext = cg * hv
    u = u_ext[0:tm]
    row = lax.broadcasted_iota(jnp.int32, u.shape, 0)
    prev_row = jnp.where(i > 0, u_ext[tm + SUBLANES - 1:tm + SUBLANES], 0.0)
    next_row = jnp.where(i < last, u_ext[tm + SUBLANES:tm + SUBLANES + 1], 0.0)
    u_prev = jnp.where(row == 0, prev_row, pltpu.roll(u, 1, 0))
    u_next = jnp.where(row == tm - 1, next_row, pltpu.roll(u, tm - 1, 0))
    conv_ref[...] = bg * (cw_ref[0:1, :] * u_prev + cw_ref[1:2, :] * u + cw_ref[2:3, :] * u_next)


def _in_call(x, g, sh, sc, w_in, conv_w, cos_t, sin_t, tm=1024):
    bsz, seq, d = x.shape
    n_in = w_in.shape[1]
    cw = d - ATT_WIDTH
    nh = seq // SUBLANES
    vec = pl.BlockSpec((1, d), lambda b, i: (0, 0))
    bvec = pl.BlockSpec((None, 1, d), lambda b, i: (b, 0, 0))
    tab = pl.BlockSpec((tm, LANES), lambda b, i: (i, 0))
    halo_prev = pl.BlockSpec((None, SUBLANES, d),
                             lambda b, i: (b, jnp.maximum(i * (tm // SUBLANES) - 1, 0), 0))
    halo_next = pl.BlockSpec((None, SUBLANES, d),
                             lambda b, i: (b, jnp.minimum((i + 1) * (tm // SUBLANES), nh - 1), 0))

    def rows(width):
        return pl.BlockSpec((None, tm, width), lambda b, i: (b, i, 0))

    return pl.pallas_call(
        functools.partial(_in_kernel, conv_width=cw, tm=tm),
        grid=(bsz, seq // tm),
        in_specs=[rows(d), halo_prev, halo_next, vec, bvec, bvec,
                  pl.BlockSpec((d, n_in), lambda b, i: (0, 0)),
                  pl.BlockSpec((3, cw), lambda b, i: (0, 0)), tab, tab],
        out_specs=[rows(ATT_WIDTH), rows(KV_WIDTH), rows(KV_WIDTH), rows(cw)],
        out_shape=[jax.ShapeDtypeStruct((bsz, seq, ATT_WIDTH), F32),
                   jax.ShapeDtypeStruct((bsz, seq, KV_WIDTH), F32),
                   jax.ShapeDtypeStruct((bsz, seq, KV_WIDTH), F32),
                   jax.ShapeDtypeStruct((bsz, seq, cw), F32)],
        compiler_params=_cparams(("arbitrary", "arbitrary")),
        name="in_proj",
    )(x, x, x, g, sh, sc, w_in, conv_w, cos_t, sin_t)


ATTN_TILE = 32


def _attn_kernel(sink_ref, q_ref, k_ref, v_ref, kc_ref, vc_ref, o_ref,
                 s_scr, p_scr, bias_scr, m_scr, *, seq, tq):
    i = pl.program_id(1)
    nblk = tq // BLOCK
    span = 3 * BLOCK
    width = s_scr.shape[1]
    rows_all = GQA_GROUP * BLOCK
    tiles_per_head = BLOCK // ATTN_TILE
    kc = kc_ref[...]
    vc = vc_ref[...]
    for blk in range(nblk):
        n = i * nblk + blk
        ws = pl.multiple_of(jnp.clip((n - 1) * BLOCK, 0, seq - span), BLOCK)
        kw = k_ref[pl.ds(ws, span), :]
        vw = v_ref[pl.ds(ws, span), :]
        qb = q_ref[blk * BLOCK:(blk + 1) * BLOCK, :]
        qpos = n * BLOCK + lax.broadcasted_iota(jnp.int32, (BLOCK, span), 0)
        kpos = ws + lax.broadcasted_iota(jnp.int32, (BLOCK, span), 1)
        bias_scr[...] = jnp.where(jnp.abs(kpos - qpos) <= WINDOW, 0.0, NEG)
        outs = []
        for kh in range(N_KV_HEADS):
            heads = [kh * GQA_GROUP + g for g in range(GQA_GROUP)]
            qs = jnp.concatenate([qb[:, h * HEAD_DIM:(h + 1) * HEAD_DIM] for h in heads], axis=0)
            lo, hi = kh * HEAD_DIM, (kh + 1) * HEAD_DIM
            s_scr[:, 0:span] = _dot_nt(qs, kw[:, lo:hi])
            s_scr[:, span:width] = _dot_nt(qs, kc[:, lo:hi])

            for t in range(rows_all // ATTN_TILE):
                r0 = t * ATTN_TILE
                b0 = (t % tiles_per_head) * ATTN_TILE
                sk = sink_ref[kh * GQA_GROUP + t // tiles_per_head] * LOG2E
                s_loc = s_scr[pl.ds(r0, ATTN_TILE), 0:span] + bias_scr[pl.ds(b0, ATTN_TILE), :]
                s_ctx = s_scr[pl.ds(r0, ATTN_TILE), span:width]
                m = jnp.maximum(jnp.maximum(jnp.max(s_loc, axis=1, keepdims=True),
                                            jnp.max(s_ctx, axis=1, keepdims=True)), sk)
                m_scr[pl.ds(r0, ATTN_TILE), :] = m
                p_scr[pl.ds(r0, ATTN_TILE), 0:span] = jnp.exp2(s_loc - m)
                p_scr[pl.ds(r0, ATTN_TILE), span:width] = jnp.exp2(s_ctx - m)
            own = (lax.broadcasted_iota(jnp.int32, (1, KV_WIDTH), 1) // HEAD_DIM) == kh
            o_ext = (_dot(p_scr[:, 0:span], jnp.where(own, vw, 1.0))
                     + _dot(p_scr[:, span:width], jnp.where(own, vc, 1.0)))
            other = (1 - kh) * HEAD_DIM
            sk_col = jnp.concatenate([jnp.full((BLOCK, 1), sink_ref[h] * LOG2E, F32) for h in heads], axis=0)
            den = o_ext[:, other:other + 1] + jnp.exp2(sk_col - m_scr[...])
            o = o_ext[:, lo:hi] / den
            outs.extend(o[g * BLOCK:(g + 1) * BLOCK] for g in range(GQA_GROUP))
        o_ref[blk * BLOCK:(blk + 1) * BLOCK, :] = jnp.concatenate(outs, axis=1)


def _attn_call(sink, q, k, v, kc, vc, tq=1024):
    bsz, seq, _ = q.shape
    n_ctx = kc.shape[1]
    span = 3 * BLOCK
    rows_all = GQA_GROUP * BLOCK
    full_kv = pl.BlockSpec((None, seq, KV_WIDTH), lambda b, i: (b, 0, 0))
    ctx_kv = pl.BlockSpec((None, n_ctx, KV_WIDTH), lambda b, i: (b, 0, 0))
    qspec = pl.BlockSpec((None, tq, ATT_WIDTH), lambda b, i: (b, i, 0))
    return pl.pallas_call(
        functools.partial(_attn_kernel, seq=seq, tq=tq),
        grid=(bsz, seq // tq),
        in_specs=[pl.BlockSpec(memory_space=pltpu.SMEM), qspec, full_kv, full_kv, ctx_kv, ctx_kv],
        out_specs=qspec,
        out_shape=jax.ShapeDtypeStruct((bsz, seq, ATT_WIDTH), F32),
        scratch_shapes=[pltpu.VMEM((rows_all, span + n_ctx), F32),
                        pltpu.VMEM((rows_all, span + n_ctx), F32),
                        pltpu.VMEM((BLOCK, span), F32),
                        pltpu.VMEM((rows_all, 1), F32)],
        compiler_params=_cparams(("arbitrary", "arbitrary")),
        name="attn",
    )(sink, q, k, v, kc, vc)


def _out_kernel(att_ref, conv_ref, wo_ref, x_ref, gt_ref,
                g_ref, sh_ref, sc_ref, wr_ref, rows_ref, aff_ref):
    y = _dot(att_ref[...], wo_ref[0:ATT_WIDTH, :]) + _dot(conv_ref[...], wo_ref[ATT_WIDTH:, :])
    x1 = x_ref[...] + gt_ref[...] * y
    d = x1.shape[-1]
    h2 = _norm_mod(x1, g_ref[...], sh_ref[...], sc_ref[...])
    rows_ref[:, 0:d] = x1
    rows_ref[:, d:2 * d] = h2
    def split3(t):
        hi = t.astype(BF16)
        r1 = t - hi.astype(F32)
        mid = r1.astype(BF16)
        return hi, mid, (r1 - mid.astype(F32)).astype(BF16)

    def nt_bf16(a, b):
        return lax.dot_general(a, b, (((1,), (1,)), ((), ())), preferred_element_type=F32)

    h_hi, h_mid, h_lo = split3(h2)
    w_hi, w_mid, w_lo = split3(wr_ref[...])
    n_exp = w_hi.shape[0]
    by_hi = nt_bf16(jnp.concatenate([w_hi, w_mid, w_lo], axis=0), h_hi)
    by_mid = nt_bf16(jnp.concatenate([w_hi, w_mid], axis=0), h_mid)
    logits = (((by_hi[2 * n_exp:] + by_mid[n_exp:]) + nt_bf16(w_hi, h_lo))
              + (by_hi[n_exp:2 * n_exp] + by_mid[:n_exp])) + by_hi[:n_exp]
    mx = jnp.max(logits, axis=0, keepdims=True)
    ex = jnp.exp(logits - mx)
    aff_ref[...] = ex / jnp.sum(ex, axis=0, keepdims=True)


def _out_call(att, conv, w_out, x, gt1, g_ffn, sh2, sc2, w_router_t, tm=1024):
    bsz, seq, d = x.shape
    cw = conv.shape[-1]
    n_exp = w_router_t.shape[0]

    def rows(width):
        return pl.BlockSpec((None, tm, width), lambda b, i: (b, i, 0))

    vec = pl.BlockSpec((1, d), lambda b, i: (0, 0))
    bvec = pl.BlockSpec((None, 1, d), lambda b, i: (b, 0, 0))
    return pl.pallas_call(
        _out_kernel,
        grid=(bsz, seq // tm),
        in_specs=[rows(ATT_WIDTH), rows(cw),
                  pl.BlockSpec((d, d), lambda b, i: (0, 0)),
                  rows(d), bvec, vec, bvec, bvec,
                  pl.BlockSpec((n_exp, d), lambda b, i: (0, 0))],
        out_specs=[rows(2 * d), pl.BlockSpec((None, n_exp, tm), lambda b, i: (b, 0, i))],
        out_shape=[jax.ShapeDtypeStruct((bsz, seq, 2 * d), F32),
                   jax.ShapeDtypeStruct((bsz, n_exp, seq), F32)],
        compiler_params=_cparams(("arbitrary", "arbitrary")),
        name="out_proj",
    )(att, conv, w_out, x, gt1, g_ffn, sh2, sc2, w_router_t)


def _prefix_rows(m, tri, ones, low):
    n_exp, n_chunk, _ = m.shape
    mb = m.astype(BF16).reshape(n_exp * n_chunk, LANES)
    within = jnp.dot(mb, tri, preferred_element_type=F32).reshape(n_exp, n_chunk, LANES)
    tot = jnp.dot(mb, ones, preferred_element_type=F32).astype(BF16).reshape(n_exp, n_chunk, LANES)
    carry = jnp.stack([jnp.dot(low, tot[e], preferred_element_type=F32) for e in range(n_exp)], axis=0)
    return carry + within, carry


def _sel_kernel(aff_ref, pos_ref, carry_ref, *, cap):
    a = aff_ref[...]
    n_exp, n_chunk, _ = a.shape
    bits = lax.bitcast_convert_type(a, jnp.int32)

    def count(pred):
        c = jnp.sum(pred.astype(F32), axis=1, keepdims=True)
        return jnp.sum(c, axis=2, keepdims=True)

    def body(it, thr):
        shift = 29 - 2 * it
        for k in (1, 2, 3):
            cand = thr | jnp.left_shift(jnp.int32(k), shift)
            ok = count(bits >= cand) >= cap
            best = jnp.where(ok, cand, thr) if k == 1 else jnp.where(ok, cand, best)
        return best

    thr = lax.fori_loop(0, 15, body, jnp.zeros((n_exp, 1, 1), jnp.int32))
    thr = jnp.where(count(bits >= (thr | 1)) >= cap, thr | 1, thr)
    gt = bits > thr
    eq = bits == thr
    need = cap - count(gt)

    r = lax.broadcasted_iota(jnp.int32, (LANES, LANES), 0)
    c = lax.broadcasted_iota(jnp.int32, (LANES, LANES), 1)
    tri = (r < c).astype(BF16)
    ones = jnp.ones((LANES, LANES), BF16)
    rr = lax.broadcasted_iota(jnp.int32, (n_chunk, n_chunk), 0)
    cc = lax.broadcasted_iota(jnp.int32, (n_chunk, n_chunk), 1)
    low = (cc < rr).astype(BF16)

    eq_rank, _ = _prefix_rows(eq, tri, ones, low)
    sel = gt | (eq & (eq_rank < need))
    pos, carry = _prefix_rows(sel, tri, ones, low)
    pos_ref[...] = jnp.where(sel, pos, -1.0)
    carry_ref[...] = carry


def _sel_call(aff4, cap):
    bsz, n_exp, n_chunk, _ = aff4.shape
    spec = pl.BlockSpec((None, n_exp, n_chunk, LANES), lambda b: (b, 0, 0, 0))
    return pl.pallas_call(
        functools.partial(_sel_kernel, cap=cap),
        grid=(bsz,),
        in_specs=[spec],
        out_specs=[spec, spec],
        out_shape=[jax.ShapeDtypeStruct(aff4.shape, F32)] * 2,
        compiler_params=_cparams(("arbitrary",)),
        name="select",
    )(aff4)


COMPACT_ROWS = 16
SLOT_BLOCK = LANES


def _compact_kernel(cum_ref, pos_ref, aff_ref, out_ref, *, n_exp, n_tchunk):
    b = pl.program_id(0)
    e = pl.program_id(1)
    base = (b * n_exp + e) * (n_tchunk + 1)
    out_ref[...] = jnp.zeros(out_ref.shape, F32)
    nt = (((1,), (1,)), ((), ()))
    rowi = lax.broadcasted_iota(jnp.int32, (COMPACT_ROWS, LANES), 0)
    lane = lax.broadcasted_iota(jnp.int32, (COMPACT_ROWS, LANES), 1).astype(F32)
    slot_b = lax.broadcasted_iota(jnp.int32, (SLOT_BLOCK, LANES), 0).astype(F32).astype(BF16)
    one_b = jnp.ones((), BF16)
    zero_b = jnp.zeros((), BF16)

    n_sblk = out_ref.shape[0] - 1

    def chunk_vals(c):
        a = aff_ref[c]
        g_hi = a.astype(BF16).astype(F32)
        r1 = a - g_hi
        g_mid = r1.astype(BF16).astype(F32)
        g_lo = r1 - g_mid
        pieces = []
        for r in range(SUBLANES):
            def bc(t):
                return jnp.broadcast_to(t[r:r + 1, :], (COMPACT_ROWS, LANES))

            piece = jnp.where(rowi == 0, float(c * SUBLANES + r),
                              jnp.where(rowi == 1, lane,
                                        jnp.where(rowi == 2, bc(g_hi),
                                                  jnp.where(rowi == 3, bc(g_mid),
                                                            jnp.where(rowi == 4, bc(g_lo), 0.0)))))
            pieces.append(piece)
        return jnp.concatenate(pieces, axis=1).astype(BF16)

    def visit(j, p, vals):
        rel = p - jnp.asarray(j * SLOT_BLOCK).astype(F32)
        rel = jnp.where((rel >= 0.0) & (rel < SLOT_BLOCK), rel, -1.0)
        pieces_oh = []
        for r in range(SUBLANES):
            row16 = jnp.broadcast_to(rel[r:r + 1, :], (2 * SUBLANES, LANES)).astype(BF16)
            blk = jnp.concatenate([row16] * (SLOT_BLOCK // (2 * SUBLANES)), axis=0)
            pieces_oh.append(jnp.where(blk == slot_b, one_b, zero_b))
        oh = jnp.concatenate(pieces_oh, axis=1)
        out_ref[j] += lax.dot_general(vals, oh, nt, preferred_element_type=F32)

    def first_block(c):
        return jnp.minimum(cum_ref[base + c] // SLOT_BLOCK, n_sblk - 1)

    for c in range(n_tchunk):
        p = pos_ref[c]
        vals = chunk_vals(c)
        j0 = first_block(c)
        visit(j0, p, vals)
        visit(j0 + 1, p, vals)

    for c in range(n_tchunk):
        lo = cum_ref[base + c]
        hi = cum_ref[base + c + 1]
        j_end = jnp.where(hi > lo, (hi - 1) // SLOT_BLOCK + 1, 0)

        def more(j, carry, c=c):
            visit(j, pos_ref[c], chunk_vals(c))
            return carry

        lax.fori_loop(first_block(c) + 2, j_end, more, 0)


def _compact_call(cum, pos5, aff5, cap):
    bsz, n_exp, n_tchunk, _, _ = pos5.shape
    n_sblk = cap // SLOT_BLOCK
    spec = pl.BlockSpec((None, None, n_tchunk, SUBLANES, LANES), lambda b, e, cum_ref: (b, e, 0, 0, 0))
    return pl.pallas_call(
        functools.partial(_compact_kernel, n_exp=n_exp, n_tchunk=n_tchunk),
        grid_spec=pltpu.PrefetchScalarGridSpec(
            num_scalar_prefetch=1,
            grid=(bsz, n_exp),
            in_specs=[spec, spec],
            out_specs=pl.BlockSpec((None, None, n_sblk + 1, COMPACT_ROWS, LANES),
                                   lambda b, e, cum_ref: (b, e, 0, 0, 0)),
        ),
        out_shape=jax.ShapeDtypeStruct((bsz, n_exp, n_sblk + 1, COMPACT_ROWS, LANES), F32),
        compiler_params=_cparams(("arbitrary", "arbitrary")),
        name="compact",
    )(cum, pos5, aff5)[:, :, :n_sblk]


FFN_ROWS = 256


FFN_COLS = 256


def _ffn_kernel(idx_ref, rows_in, g_ref, gt_ref, wg_ref, wu_ref, wd_ref, rows_hbm,
                buf0, buf1, wgb, wub, wdb, gsem, ssem, *, cap, n_exp, d):
    del rows_in
    e = pl.program_id(0)
    bufs = (buf0, buf1)
    n_chunk = cap // FFN_ROWS
    f = wgb.shape[1]
    acc_cols = pl.ds(0, d)

    def hbm_row(ee, b, s):
        return idx_ref[(b * n_exp + ee) * cap + s]

    def dma_thread(s):
        return s % 2 if isinstance(s, int) else 0

    def gather_start(ee, b, s):
        pltpu.make_async_copy(rows_hbm.at[pl.ds(hbm_row(ee, b, s), 1)], bufs[b].at[pl.ds(s, 1)],
                              gsem.at[b]).start(priority=dma_thread(s))

    def gather_wait(b):
        pltpu.make_async_copy(rows_hbm.at[pl.ds(0, cap)], bufs[b], gsem.at[b]).wait()

    def scatter_start(ee, b, s):
        pltpu.make_async_copy(bufs[b].at[pl.ds(s, 1), acc_cols],
                              rows_hbm.at[pl.ds(hbm_row(ee, b, s), 1), acc_cols],
                              ssem.at[b]).start(priority=dma_thread(s))

    def scatter_wait(b):
        pltpu.make_async_copy(bufs[b].at[:, acc_cols], rows_hbm.at[pl.ds(0, cap), acc_cols],
                              ssem.at[b]).wait()

    def compute_chunk(b, ci, todo):
        buf = bufs[b]
        rows = slice(ci * FFN_ROWS, (ci + 1) * FFN_ROWS)
        n_piece = 3 * (f // FFN_COLS)
        per_piece = -(-len(todo) // n_piece)

        def issue_some():
            for _ in range(min(per_piece, len(todo))):
                todo.pop(0)()

        xc = buf[rows, d:2 * d].astype(BF16)
        g_col = jnp.concatenate(
            [jnp.broadcast_to(g_ref[b, j:j + 1, :], (LANES, LANES)).T
             for j in range(ci * FFN_ROWS // LANES, (ci + 1) * FFN_ROWS // LANES)], axis=0)
        g_col = jnp.tile(g_col, (1, FFN_COLS // LANES))
        gt = gt_ref[b]
        hm = []
        for nj in range(f // FFN_COLS):
            cols = slice(nj * FFN_COLS, (nj + 1) * FFN_COLS)
            a = jnp.dot(xc, wgb[:, cols], preferred_element_type=F32)
            issue_some()
            u = jnp.dot(xc, wub[:, cols], preferred_element_type=F32)
            issue_some()
            hm.append((a * jax.nn.sigmoid(a) * u).astype(BF16))
        hm = jnp.concatenate(hm, axis=1)
        for nj in range(d // FFN_COLS):
            cols = slice(nj * FFN_COLS, (nj + 1) * FFN_COLS)
            y = jnp.dot(hm, wdb[:, cols], preferred_element_type=F32) * g_col
            buf[rows, cols] = buf[rows, cols] + gt[:, cols] * y
            issue_some()
        assert not todo

    def chunk_rows(ci):
        return range(ci * FFN_ROWS, (ci + 1) * FFN_ROWS)

    def split(n, parts):
        bounds = [n * k // parts for k in range(parts + 1)]
        return [range(bounds[k], bounds[k + 1]) for k in range(parts)]

    @pl.when(e == 0)
    def _():
        def first(s, carry):
            gather_start(0, 0, s)
            return carry
        lax.fori_loop(0, cap, first, 0)

    wgb[...] = wg_ref[...].astype(BF16)
    wub[...] = wu_ref[...].astype(BF16)
    wdb[...] = wd_ref[...].astype(BF16)

    @pl.when(e > 0)
    def _():
        scatter_wait(1)

    gather_wait(0)
    for ci in range(n_chunk):
        todo = [functools.partial(gather_start, e, 1, s) for s in chunk_rows(ci)]
        if ci > 0:
            todo += [functools.partial(scatter_start, e, 0, s) for s in chunk_rows(ci - 1)]
        compute_chunk(0, ci, todo)

    ne = jnp.minimum(e + 1, n_exp - 1)
    gather_wait(1)
    compute_chunk(1, 0, [functools.partial(scatter_start, e, 0, s) for s in chunk_rows(n_chunk - 1)])
    scatter_wait(0)
    for ci, part in zip(range(1, n_chunk), split(cap, n_chunk - 1)):
        todo = [functools.partial(gather_start, ne, 0, s) for s in part]
        todo += [functools.partial(scatter_start, e, 1, s) for s in chunk_rows(ci - 1)]
        compute_chunk(1, ci, todo)
    for s in chunk_rows(n_chunk - 1):
        scatter_start(e, 1, s)

    @pl.when(e == n_exp - 1)
    def _():
        scatter_wait(1)
        gather_wait(0)


def _ffn_call(idx, rows, gate, gt2, w_gate, w_up, w_down, cap):
    bsz, seq, d2 = rows.shape
    d = d2 // 2
    n_exp, _, f = w_gate.shape
    assert bsz == 2, "ffn double-buffers exactly two samples per expert"
    kern = functools.partial(_ffn_kernel, cap=cap, n_exp=n_exp, d=d)
    any_spec = pl.BlockSpec(memory_space=pl.ANY)
    out = pl.pallas_call(
        kern,
        grid_spec=pltpu.PrefetchScalarGridSpec(
            num_scalar_prefetch=1,
            grid=(n_exp,),
            in_specs=[any_spec,
                      pl.BlockSpec((bsz, None, cap // LANES, LANES), lambda e, idx_ref: (0, e, 0, 0)),
                      pl.BlockSpec((bsz, 1, d), lambda e, idx_ref: (0, 0, 0)),
                      pl.BlockSpec((None, d, f), lambda e, idx_ref: (e, 0, 0)),
                      pl.BlockSpec((None, d, f), lambda e, idx_ref: (e, 0, 0)),
                      pl.BlockSpec((None, f, d), lambda e, idx_ref: (e, 0, 0))],
            out_specs=any_spec,
            scratch_shapes=[pltpu.VMEM((cap, d2), F32), pltpu.VMEM((cap, d2), F32),
                            pltpu.VMEM((d, f), BF16), pltpu.VMEM((d, f), BF16), pltpu.VMEM((f, d), BF16),
                            pltpu.SemaphoreType.DMA((bsz,)), pltpu.SemaphoreType.DMA((bsz,))],
        ),
        out_shape=jax.ShapeDtypeStruct((bsz * seq, d2), F32),
        input_output_aliases={1: 0},
        compiler_params=_cparams(("arbitrary",)),
        name="ffn",
    )(idx, rows.reshape(bsz * seq, d2), gate, gt2, w_gate, w_up, w_down)
    return out.reshape(bsz, seq, d2)


def _final_kernel(x_ref, g_ref, o_ref):
    xf = x_ref[...]
    o_ref[...] = xf * lax.rsqrt(jnp.mean(xf * xf, axis=-1, keepdims=True) + EPS) * g_ref[...]


def _final_call(rows, g, tm=2048):
    bsz, seq, d2 = rows.shape
    d = d2 // 2
    return pl.pallas_call(
        _final_kernel,
        grid=(bsz, seq // tm),
        in_specs=[pl.BlockSpec((None, tm, d), lambda b, i: (b, i, 0)),
                  pl.BlockSpec((1, d), lambda b, i: (0, 0))],
        out_specs=pl.BlockSpec((None, tm, d), lambda b, i: (b, i, 0)),
        out_shape=jax.ShapeDtypeStruct((bsz, seq, d), F32),
        compiler_params=_cparams(("arbitrary", "arbitrary")),
        name="final",
    )(rows, g)


def _rope_tables(seq):
    t = np.arange(seq)
    n_freq = HEAD_DIM // 4
    inv = ROPE_THETA ** (-np.arange(n_freq, dtype=np.float64) / n_freq)
    ang = np.concatenate([(t // GRID_W)[:, None] * inv, (t % GRID_W)[:, None] * inv], axis=1)
    cos_h = np.concatenate([np.cos(ang), np.cos(ang)], axis=1)
    sin_h = np.concatenate([-np.sin(ang), np.sin(ang)], axis=1)
    reps = LANES // HEAD_DIM
    return (jnp.asarray(np.tile(cos_h, (1, reps)), F32), jnp.asarray(np.tile(sin_h, (1, reps)), F32))


def kernel(x, c, ctx, c_ctx, w_ada, b_ada, g_mix, w_in, conv_w, sink, w_out, g_ffn,
           w_router, w_gate, w_up, w_down, g_final):
    bsz, seq, d = x.shape
    n_exp = w_router.shape[-1]
    cap = EC_CAPACITY_FACTOR * seq // n_exp
    assert w_ada.shape[0] == 1, "single trunk layer"
    assert bsz + 1 <= SUBLANES and seq % (SUBLANES * LANES) == 0 and cap % SLOT_BLOCK == 0

    cin = jnp.zeros((SUBLANES, d), F32).at[:bsz].set(c).at[bsz].set(c_ctx)
    mod = _mod_call(cin, w_ada[0], b_ada[0][None, :])
    sh1, sc1, gt1, sh2, sc2, gt2 = [m[:bsz, None, :] for m in jnp.split(mod, 6, axis=-1)]
    csh1 = mod[bsz:bsz + 1, 0:d]
    csc1 = mod[bsz:bsz + 1, d:2 * d]

    w_in0 = w_in[0]
    g_mix_r = g_mix[0][None, :]
    kv0 = ATT_WIDTH
    k_c, v_c = _ctx_call(ctx, g_mix_r, csh1, csc1, w_in0[:, kv0:kv0 + 2 * KV_WIDTH])

    cos_t, sin_t = _rope_tables(seq)
    q, k, v, conv = _in_call(x, g_mix_r, sh1, sc1, w_in0, conv_w[0], cos_t, sin_t)
    att = _attn_call(sink[0], q, k, v, k_c, v_c)
    rows, aff = _out_call(att, conv, w_out[0], x, gt1, g_ffn[0][None, :], sh2, sc2, w_router[0].T)

    n_chunk = seq // LANES
    aff4 = aff.reshape(bsz, n_exp, n_chunk, LANES)
    pos, carry = _sel_call(aff4, cap)
    n_tchunk = n_chunk // SUBLANES
    cum = jnp.concatenate([carry[:, :, ::SUBLANES, 0], jnp.full((bsz, n_exp, 1), cap, F32)], axis=-1)
    cum = cum.astype(jnp.int32).reshape(-1)
    comp = _compact_call(cum, pos.reshape(bsz, n_exp, n_tchunk, SUBLANES, LANES),
                         aff4.reshape(bsz, n_exp, n_tchunk, SUBLANES, LANES), cap)
    idx = (comp[:, :, :, 0, :] * LANES + comp[:, :, :, 1, :]).astype(jnp.int32)
    idx = (idx + (jnp.arange(bsz, dtype=jnp.int32) * seq)[:, None, None, None]).reshape(-1)
    gate = (comp[:, :, :, 2, :] + comp[:, :, :, 3, :]) + comp[:, :, :, 4, :]

    rows = _ffn_call(idx, rows, gate, gt2, w_gate[0], w_up[0], w_down[0], cap)
    return _final_call(rows, g_final[None, :])
```

```python
import functools

import numpy as np
import jax
import jax.numpy as jnp
from jax import lax
from jax.experimental import pallas as pl
from jax.experimental.pallas import tpu as pltpu

GRID_W = 64
N_HEADS = 8
N_KV_HEADS = 2
HEAD_DIM = 64
GQA_GROUP = N_HEADS // N_KV_HEADS
ATT_WIDTH = N_HEADS * HEAD_DIM
KV_WIDTH = N_KV_HEADS * HEAD_DIM
WINDOW = 128
BLOCK = 128
ROPE_THETA = 10000.0
N_EXPERTS = 16
EC_CAPACITY_FACTOR = 2
EPS = 1e-6

LANES = 128
SUBLANES = 8
MXU_COLS = 256
VMEM_LIMIT = 56 * 1024 * 1024

F32 = jnp.float32
BF16 = jnp.bfloat16
NEG = -1e30
LOG2E = 1.4426950408889634


def _cparams(sem):
    return pltpu.CompilerParams(dimension_semantics=sem, vmem_limit_bytes=VMEM_LIMIT)


def _dot(a, b):
    return jnp.dot(a, b, precision=lax.Precision.DEFAULT, preferred_element_type=F32)


def _dot_nt(a, b):
    return lax.dot_general(a, b, (((1,), (1,)), ((), ())), precision=lax.Precision.DEFAULT,
                           preferred_element_type=F32)


def _norm_mod(xf, g, shift, scale):
    r = xf * lax.rsqrt(jnp.mean(xf * xf, axis=-1, keepdims=True) + EPS)
    return (r * g) * (1.0 + scale) + shift


def _mod_kernel(c_ref, w_ref, b_ref, o_ref):
    c = c_ref[...]
    s = c * jax.nn.sigmoid(c)
    o_ref[...] = _dot(s, w_ref[...]) + b_ref[...]


def _mod_call(cin, w, bias, tn=1024):
    d, n = w.shape
    return pl.pallas_call(
        _mod_kernel,
        grid=(n // tn,),
        in_specs=[pl.BlockSpec((SUBLANES, d), lambda j: (0, 0)),
                  pl.BlockSpec((d, tn), lambda j: (0, j)),
                  pl.BlockSpec((1, tn), lambda j: (0, j))],
        out_specs=pl.BlockSpec((SUBLANES, tn), lambda j: (0, j)),
        out_shape=jax.ShapeDtypeStruct((SUBLANES, n), F32),
        compiler_params=_cparams(("arbitrary",)),
        name="mod",
    )(cin, w, bias)


def _ctx_kernel(x_ref, g_ref, sh_ref, sc_ref, w_ref, k_ref, v_ref):
    h = _norm_mod(x_ref[...], g_ref[...], sh_ref[...], sc_ref[...])
    kv = _dot(h, w_ref[...])
    k_ref[...] = kv[:, :KV_WIDTH]
    v_ref[...] = kv[:, KV_WIDTH:]


def _ctx_call(ctx, g, sh, sc, w_kv):
    bsz, n, d = ctx.shape
    vec = pl.BlockSpec((1, d), lambda b: (0, 0))
    kv_spec = pl.BlockSpec((None, n, KV_WIDTH), lambda b: (b, 0, 0))
    return pl.pallas_call(
        _ctx_kernel,
        grid=(bsz,),
        in_specs=[pl.BlockSpec((None, n, d), lambda b: (b, 0, 0)), vec, vec, vec,
                  pl.BlockSpec((d, 2 * KV_WIDTH), lambda b: (0, 0))],
        out_specs=[kv_spec, kv_spec],
        out_shape=[jax.ShapeDtypeStruct((bsz, n, KV_WIDTH), F32)] * 2,
        compiler_params=_cparams(("arbitrary",)),
        name="ctx_kv",
    )(ctx, g, sh, sc, w_kv)


def _rope(t, cos, sin_signed):
    lane = lax.broadcasted_iota(jnp.int32, t.shape, 1)
    first = (lane % HEAD_DIM) < (HEAD_DIM // 2)
    swapped = jnp.where(first, pltpu.roll(t, LANES - HEAD_DIM // 2, 1), pltpu.roll(t, HEAD_DIM // 2, 1))
    return t * cos + swapped * sin_signed


def _in_kernel(x_ref, xp_ref, xn_ref, g_ref, sh_ref, sc_ref, w_ref, cw_ref, cos_ref, sin_ref,
               q_ref, k_ref, v_ref, conv_ref, *, conv_width, tm):
    i = pl.program_id(1)
    last = pl.num_programs(1) - 1
    x_ext = jnp.concatenate([x_ref[...], xp_ref[...], xn_ref[...]], axis=0)
    h_ext = _norm_mod(x_ext, g_ref[...], sh_ref[...], sc_ref[...])
    h = h_ext[0:tm]
    cos = cos_ref[...]
    sin = sin_ref[...]
    scale = HEAD_DIM ** -0.5 * LOG2E
    for j in range(ATT_WIDTH // MXU_COLS):
        qj = _dot(h, w_ref[:, j * MXU_COLS:(j + 1) * MXU_COLS])
        for jj in range(MXU_COLS // LANES):
            c0 = j * MXU_COLS + jj * LANES
            q_ref[:, c0:c0 + LANES] = _rope(qj[:, jj * LANES:(jj + 1) * LANES], cos, sin) * scale
    o = ATT_WIDTH
    kv = _dot(h, w_ref[:, o:o + 2 * KV_WIDTH])
    k_ref[...] = _rope(kv[:, :KV_WIDTH], cos, sin)
    v_ref[...] = kv[:, KV_WIDTH:]
    o += 2 * KV_WIDTH
    bg = _dot(h, w_ref[:, o:o + conv_width])
    o += conv_width
    cg = _dot(h_ext, w_ref[:, o:o + conv_width])
    o += conv_width
    hv = _dot(h_ext, w_ref[:, o:o + conv_width])
    u_ext = cg * hv
    u = u_ext[0:tm]
    row = lax.broadcasted_iota(jnp.int32, u.shape, 0)
    prev_row = jnp.where(i > 0, u_ext[tm + SUBLANES - 1:tm + SUBLANES], 0.0)
    next_row = jnp.where(i < last, u_ext[tm + SUBLANES:tm + SUBLANES + 1], 0.0)
    u_prev = jnp.where(row == 0, prev_row, pltpu.roll(u, 1, 0))
    u_next = jnp.where(row == tm - 1, next_row, pltpu.roll(u, tm - 1, 0))
    conv_ref[...] = bg * (cw_ref[0:1, :] * u_prev + cw_ref[1:2, :] * u + cw_ref[2:3, :] * u_next)


def _in_call(x, g, sh, sc, w_in, conv_w, cos_t, sin_t, tm=1024):
    bsz, seq, d = x.shape
    n_in = w_in.shape[1]
    cw = d - ATT_WIDTH
    nh = seq // SUBLANES
    vec = pl.BlockSpec((1, d), lambda b, i: (0, 0))
    bvec = pl.BlockSpec((None, 1, d), lambda b, i: (b, 0, 0))
    tab = pl.BlockSpec((tm, LANES), lambda b, i: (i, 0))
    halo_prev = pl.BlockSpec((None, SUBLANES, d),
                             lambda b, i: (b, jnp.maximum(i * (tm // SUBLANES) - 1, 0), 0))
    halo_next = pl.BlockSpec((None, SUBLANES, d),
                             lambda b, i: (b, jnp.minimum((i + 1) * (tm // SUBLANES), nh - 1), 0))

    def rows(width):
        return pl.BlockSpec((None, tm, width), lambda b, i: (b, i, 0))

    return pl.pallas_call(
        functools.partial(_in_kernel, conv_width=cw, tm=tm),
        grid=(bsz, seq // tm),
        in_specs=[rows(d), halo_prev, halo_next, vec, bvec, bvec,
                  pl.BlockSpec((d, n_in), lambda b, i: (0, 0)),
                  pl.BlockSpec((3, cw), lambda b, i: (0, 0)), tab, tab],
        out_specs=[rows(ATT_WIDTH), rows(KV_WIDTH), rows(KV_WIDTH), rows(cw)],
        out_shape=[jax.ShapeDtypeStruct((bsz, seq, ATT_WIDTH), F32),
                   jax.ShapeDtypeStruct((bsz, seq, KV_WIDTH), F32),
                   jax.ShapeDtypeStruct((bsz, seq, KV_WIDTH), F32),
                   jax.ShapeDtypeStruct((bsz, seq, cw), F32)],
        compiler_params=_cparams(("arbitrary", "arbitrary")),
        name="in_proj",
    )(x, x, x, g, sh, sc, w_in, conv_w, cos_t, sin_t)


ATTN_TILE = 32


def _attn_kernel(sink_ref, q_ref, k_ref, v_ref, kc_ref, vc_ref, o_ref,
                 s_scr, p_scr, bias_scr, m_scr, *, seq, tq):
    i = pl.program_id(1)
    nblk = tq // BLOCK
    span = 3 * BLOCK
    width = s_scr.shape[1]
    rows_all = GQA_GROUP * BLOCK
    tiles_per_head = BLOCK // ATTN_TILE
    kc = kc_ref[...]
    vc = vc_ref[...]
    for blk in range(nblk):
        n = i * nblk + blk
        ws = pl.multiple_of(jnp.clip((n - 1) * BLOCK, 0, seq - span), BLOCK)
        kw = k_ref[pl.ds(ws, span), :]
        vw = v_ref[pl.ds(ws, span), :]
        qb = q_ref[blk * BLOCK:(blk + 1) * BLOCK, :]
        qpos = n * BLOCK + lax.broadcasted_iota(jnp.int32, (BLOCK, span), 0)
        kpos = ws + lax.broadcasted_iota(jnp.int32, (BLOCK, span), 1)
        bias_scr[...] = jnp.where(jnp.abs(kpos - qpos) <= WINDOW, 0.0, NEG)
        outs = []
        for kh in range(N_KV_HEADS):
            heads = [kh * GQA_GROUP + g for g in range(GQA_GROUP)]
            qs = jnp.concatenate([qb[:, h * HEAD_DIM:(h + 1) * HEAD_DIM] for h in heads], axis=0)
            lo, hi = kh * HEAD_DIM, (kh + 1) * HEAD_DIM
            s_scr[:, 0:span] = _dot_nt(qs, kw[:, lo:hi])
            s_scr[:, span:width] = _dot_nt(qs, kc[:, lo:hi])

            for t in range(rows_all // ATTN_TILE):
                r0 = t * ATTN_TILE
                b0 = (t % tiles_per_head) * ATTN_TILE
                sk = sink_ref[kh * GQA_GROUP + t // tiles_per_head] * LOG2E
                s_loc = s_scr[pl.ds(r0, ATTN_TILE), 0:span] + bias_scr[pl.ds(b0, ATTN_TILE), :]
                s_ctx = s_scr[pl.ds(r0, ATTN_TILE), span:width]
                m = jnp.maximum(jnp.maximum(jnp.max(s_loc, axis=1, keepdims=True),
                                            jnp.max(s_ctx, axis=1, keepdims=True)), sk)
                m_scr[pl.ds(r0, ATTN_TILE), :] = m
                p_scr[pl.ds(r0, ATTN_TILE), 0:span] = jnp.exp2(s_loc - m)
                p_scr[pl.ds(r0, ATTN_TILE), span:width] = jnp.exp2(s_ctx - m)
            own = (lax.broadcasted_iota(jnp.int32, (1, KV_WIDTH), 1) // HEAD_DIM) == kh
            o_ext = (_dot(p_scr[:, 0:span], jnp.where(own, vw, 1.0))
                     + _dot(p_scr[:, span:width], jnp.where(own, vc, 1.0)))
            other = (1 - kh) * HEAD_DIM
            sk_col = jnp.concatenate([jnp.full((BLOCK, 1), sink_ref[h] * LOG2E, F32) for h in heads], axis=0)
            den = o_ext[:, other:other + 1] + jnp.exp2(sk_col - m_scr[...])
            o = o_ext[:, lo:hi] / den
            outs.extend(o[g * BLOCK:(g + 1) * BLOCK] for g in range(GQA_GROUP))
        o_ref[blk * BLOCK:(blk + 1) * BLOCK, :] = jnp.concatenate(outs, axis=1)


def _attn_call(sink, q, k, v, kc, vc, tq=1024):
    bsz, seq, _ = q.shape
    n_ctx = kc.shape[1]
    span = 3 * BLOCK
    rows_all = GQA_GROUP * BLOCK
    full_kv = pl.BlockSpec((None, seq, KV_WIDTH), lambda b, i: (b, 0, 0))
    ctx_kv = pl.BlockSpec((None, n_ctx, KV_WIDTH), lambda b, i: (b, 0, 0))
    qspec = pl.BlockSpec((None, tq, ATT_WIDTH), lambda b, i: (b, i, 0))
    return pl.pallas_call(
        functools.partial(_attn_kernel, seq=seq, tq=tq),
        grid=(bsz, seq // tq),
        in_specs=[pl.BlockSpec(memory_space=pltpu.SMEM), qspec, full_kv, full_kv, ctx_kv, ctx_kv],
        out_specs=qspec,
        out_shape=jax.ShapeDtypeStruct((bsz, seq, ATT_WIDTH), F32),
        scratch_shapes=[pltpu.VMEM((rows_all, span + n_ctx), F32),
                        pltpu.VMEM((rows_all, span + n_ctx), F32),
                        pltpu.VMEM((BLOCK, span), F32),
                        pltpu.VMEM((rows_all, 1), F32)],
        compiler_params=_cparams(("arbitrary", "arbitrary")),
        name="attn",
    )(sink, q, k, v, kc, vc)


def _out_kernel(att_ref, conv_ref, wo_ref, x_ref, gt_ref,
                g_ref, sh_ref, sc_ref, wr_ref, rows_ref, aff_ref):
    y = _dot(att_ref[...], wo_ref[0:ATT_WIDTH, :]) + _dot(conv_ref[...], wo_ref[ATT_WIDTH:, :])
    x1 = x_ref[...] + gt_ref[...] * y
    d = x1.shape[-1]
    h2 = _norm_mod(x1, g_ref[...], sh_ref[...], sc_ref[...])
    rows_ref[:, 0:d] = x1
    rows_ref[:, d:2 * d] = h2
    def split3(t):
        hi = t.astype(BF16)
        r1 = t - hi.astype(F32)
        mid = r1.astype(BF16)
        return hi, mid, (r1 - mid.astype(F32)).astype(BF16)

    def nt_bf16(a, b):
        return lax.dot_general(a, b, (((1,), (1,)), ((), ())), preferred_element_type=F32)

    h_hi, h_mid, h_lo = split3(h2)
    w_hi, w_mid, w_lo = split3(wr_ref[...])
    n_exp = w_hi.shape[0]
    by_hi = nt_bf16(jnp.concatenate([w_hi, w_mid, w_lo], axis=0), h_hi)
    by_mid = nt_bf16(jnp.concatenate([w_hi, w_mid], axis=0), h_mid)
    logits = (((by_hi[2 * n_exp:] + by_mid[n_exp:]) + nt_bf16(w_hi, h_lo))
              + (by_hi[n_exp:2 * n_exp] + by_mid[:n_exp])) + by_hi[:n_exp]
    mx = jnp.max(logits, axis=0, keepdims=True)
    ex = jnp.exp(logits - mx)
    aff_ref[...] = ex / jnp.sum(ex, axis=0, keepdims=True)


def _out_call(att, conv, w_out, x, gt1, g_ffn, sh2, sc2, w_router_t, tm=1024):
    bsz, seq, d = x.shape
    cw = conv.shape[-1]
    n_exp = w_router_t.shape[0]

    def rows(width):
        return pl.BlockSpec((None, tm, width), lambda b, i: (b, i, 0))

    vec = pl.BlockSpec((1, d), lambda b, i: (0, 0))
    bvec = pl.BlockSpec((None, 1, d), lambda b, i: (b, 0, 0))
    return pl.pallas_call(
        _out_kernel,
        grid=(bsz, seq // tm),
        in_specs=[rows(ATT_WIDTH), rows(cw),
                  pl.BlockSpec((d, d), lambda b, i: (0, 0)),
                  rows(d), bvec, vec, bvec, bvec,
                  pl.BlockSpec((n_exp, d), lambda b, i: (0, 0))],
        out_specs=[rows(2 * d), pl.BlockSpec((None, n_exp, tm), lambda b, i: (b, 0, i))],
        out_shape=[jax.ShapeDtypeStruct((bsz, seq, 2 * d), F32),
                   jax.ShapeDtypeStruct((bsz, n_exp, seq), F32)],
        compiler_params=_cparams(("arbitrary", "arbitrary")),
        name="out_proj",
    )(att, conv, w_out, x, gt1, g_ffn, sh2, sc2, w_router_t)


def _prefix_rows(m, tri, ones, low):
    n_exp, n_chunk, _ = m.shape
    mb = m.astype(BF16).reshape(n_exp * n_chunk, LANES)
    within = jnp.dot(mb, tri, preferred_element_type=F32).reshape(n_exp, n_chunk, LANES)
    tot = jnp.dot(mb, ones, preferred_element_type=F32).astype(BF16).reshape(n_exp, n_chunk, LANES)
    carry = jnp.stack([jnp.dot(low, tot[e], preferred_element_type=F32) for e in range(n_exp)], axis=0)
    return carry + within, carry


def _sel_kernel(aff_ref, pos_ref, carry_ref, *, cap):
    a = aff_ref[...]
    n_exp, n_chunk, _ = a.shape
    bits = lax.bitcast_convert_type(a, jnp.int32)

    def count(pred):
        c = jnp.sum(pred.astype(F32), axis=1, keepdims=True)
        return jnp.sum(c, axis=2, keepdims=True)

    def body(it, thr):
        shift = 29 - 2 * it
        for k in (1, 2, 3):
            cand = thr | jnp.left_shift(jnp.int32(k), shift)
            ok = count(bits >= cand) >= cap
            best = jnp.where(ok, cand, thr) if k == 1 else jnp.where(ok, cand, best)
        return best

    thr = lax.fori_loop(0, 15, body, jnp.zeros((n_exp, 1, 1), jnp.int32))
    thr = jnp.where(count(bits >= (thr | 1)) >= cap, thr | 1, thr)
    gt = bits > thr
    eq = bits == thr
    need = cap - count(gt)

    r = lax.broadcasted_iota(jnp.int32, (LANES, LANES), 0)
    c = lax.broadcasted_iota(jnp.int32, (LANES, LANES), 1)
    tri = (r < c).astype(BF16)
    ones = jnp.ones((LANES, LANES), BF16)
    rr = lax.broadcasted_iota(jnp.int32, (n_chunk, n_chunk), 0)
    cc = lax.broadcasted_iota(jnp.int32, (n_chunk, n_chunk), 1)
    low = (cc < rr).astype(BF16)

    eq_rank, _ = _prefix_rows(eq, tri, ones, low)
    sel = gt | (eq & (eq_rank < need))
    pos, carry = _prefix_rows(sel, tri, ones, low)
    pos_ref[...] = jnp.where(sel, pos, -1.0)
    carry_ref[...] = carry


def _sel_call(aff4, cap):
    bsz, n_exp, n_chunk, _ = aff4.shape
    spec = pl.BlockSpec((None, n_exp, n_chunk, LANES), lambda b: (b, 0, 0, 0))
    return pl.pallas_call(
        functools.partial(_sel_kernel, cap=cap),
        grid=(bsz,),
        in_specs=[spec],
        out_specs=[spec, spec],
        out_shape=[jax.ShapeDtypeStruct(aff4.shape, F32)] * 2,
        compiler_params=_cparams(("arbitrary",)),
        name="select",
    )(aff4)


COMPACT_ROWS = 16
SLOT_BLOCK = LANES


def _compact_kernel(cum_ref, pos_ref, aff_ref, out_ref, *, n_exp, n_tchunk):
    b = pl.program_id(0)
    e = pl.program_id(1)
    base = (b * n_exp + e) * (n_tchunk + 1)
    out_ref[...] = jnp.zeros(out_ref.shape, F32)
    nt = (((1,), (1,)), ((), ()))
    rowi = lax.broadcasted_iota(jnp.int32, (COMPACT_ROWS, LANES), 0)
    lane = lax.broadcasted_iota(jnp.int32, (COMPACT_ROWS, LANES), 1).astype(F32)
    slot_b = lax.broadcasted_iota(jnp.int32, (SLOT_BLOCK, LANES), 0).astype(F32).astype(BF16)
    one_b = jnp.ones((), BF16)
    zero_b = jnp.zeros((), BF16)

    n_sblk = out_ref.shape[0] - 1

    def chunk_vals(c):
        a = aff_ref[c]
        g_hi = a.astype(BF16).astype(F32)
        r1 = a - g_hi
        g_mid = r1.astype(BF16).astype(F32)
        g_lo = r1 - g_mid
        pieces = []
        for r in range(SUBLANES):
            def bc(t):
                return jnp.broadcast_to(t[r:r + 1, :], (COMPACT_ROWS, LANES))

            piece = jnp.where(rowi == 0, float(c * SUBLANES + r),
                              jnp.where(rowi == 1, lane,
                                        jnp.where(rowi == 2, bc(g_hi),
                                                  jnp.where(rowi == 3, bc(g_mid),
                                                            jnp.where(rowi == 4, bc(g_lo), 0.0)))))
            pieces.append(piece)
        return jnp.concatenate(pieces, axis=1).astype(BF16)

    def visit(j, p, vals):
        rel = p - jnp.asarray(j * SLOT_BLOCK).astype(F32)
        rel = jnp.where((rel >= 0.0) & (rel < SLOT_BLOCK), rel, -1.0)
        pieces_oh = []
        for r in range(SUBLANES):
            row16 = jnp.broadcast_to(rel[r:r + 1, :], (2 * SUBLANES, LANES)).astype(BF16)
            blk = jnp.concatenate([row16] * (SLOT_BLOCK // (2 * SUBLANES)), axis=0)
            pieces_oh.append(jnp.where(blk == slot_b, one_b, zero_b))
        oh = jnp.concatenate(pieces_oh, axis=1)
        out_ref[j] += lax.dot_general(vals, oh, nt, preferred_element_type=F32)

    def first_block(c):
        return jnp.minimum(cum_ref[base + c] // SLOT_BLOCK, n_sblk - 1)

    for c in range(n_tchunk):
        p = pos_ref[c]
        vals = chunk_vals(c)
        j0 = first_block(c)
        visit(j0, p, vals)
        visit(j0 + 1, p, vals)

    for c in range(n_tchunk):
        lo = cum_ref[base + c]
        hi = cum_ref[base + c + 1]
        j_end = jnp.where(hi > lo, (hi - 1) // SLOT_BLOCK + 1, 0)

        def more(j, carry, c=c):
            visit(j, pos_ref[c], chunk_vals(c))
            return carry

        lax.fori_loop(first_block(c) + 2, j_end, more, 0)


def _compact_call(cum, pos5, aff5, cap):
    bsz, n_exp, n_tchunk, _, _ = pos5.shape
    n_sblk = cap // SLOT_BLOCK
    spec = pl.BlockSpec((None, None, n_tchunk, SUBLANES, LANES), lambda b, e, cum_ref: (b, e, 0, 0, 0))
    return pl.pallas_call(
        functools.partial(_compact_kernel, n_exp=n_exp, n_tchunk=n_tchunk),
        grid_spec=pltpu.PrefetchScalarGridSpec(
            num_scalar_prefetch=1,
            grid=(bsz, n_exp),
            in_specs=[spec, spec],
            out_specs=pl.BlockSpec((None, None, n_sblk + 1, COMPACT_ROWS, LANES),
                                   lambda b, e, cum_ref: (b, e, 0, 0, 0)),
        ),
        out_shape=jax.ShapeDtypeStruct((bsz, n_exp, n_sblk + 1, COMPACT_ROWS, LANES), F32),
        compiler_params=_cparams(("arbitrary", "arbitrary")),
        name="compact",
    )(cum, pos5, aff5)[:, :, :n_sblk]


FFN_ROWS = 256


FFN_COLS = 256


def _ffn_kernel(idx_ref, rows_in, g_ref, gt_ref, wg_ref, wu_ref, wd_ref, rows_hbm,
                buf0, buf1, wg32, wu32, wd32, wgb, wub, wdb, gsem, ssem, wsem, *, cap, n_exp, d):
    del rows_in
    e = pl.program_id(0)
    bufs = (buf0, buf1)
    n_chunk = cap // FFN_ROWS
    f = wgb.shape[1]
    acc_cols = pl.ds(0, d)

    def hbm_row(ee, b, s):
        return idx_ref[(b * n_exp + ee) * cap + s]

    def dma_thread(s):
        return 0

    def gather_start(ee, b, s):
        pltpu.make_async_copy(rows_hbm.at[pl.ds(hbm_row(ee, b, s), 1)], bufs[b].at[pl.ds(s, 1)],
                              gsem.at[b]).start(priority=dma_thread(s))

    def gather_wait(b):
        pltpu.make_async_copy(rows_hbm.at[pl.ds(0, cap)], bufs[b], gsem.at[b]).wait()

    def scatter_start(ee, b, s):
        pltpu.make_async_copy(bufs[b].at[pl.ds(s, 1), acc_cols],
                              rows_hbm.at[pl.ds(hbm_row(ee, b, s), 1), acc_cols],
                              ssem.at[b]).start(priority=dma_thread(s))

    def scatter_wait(b):
        pltpu.make_async_copy(bufs[b].at[:, acc_cols], rows_hbm.at[pl.ds(0, cap), acc_cols],
                              ssem.at[b]).wait()

    def compute_chunk(b, ci, todo):
        buf = bufs[b]
        rows = slice(ci * FFN_ROWS, (ci + 1) * FFN_ROWS)
        n_piece = 3 * (f // FFN_COLS)
        per_piece = -(-len(todo) // n_piece)

        def issue_some():
            for _ in range(min(per_piece, len(todo))):
                todo.pop(0)()

        xc = buf[rows, d:2 * d].astype(BF16)
        g_col = jnp.concatenate(
            [jnp.broadcast_to(g_ref[b, j:j + 1, :], (LANES, LANES)).T
             for j in range(ci * FFN_ROWS // LANES, (ci + 1) * FFN_ROWS // LANES)], axis=0)
        g_col = jnp.tile(g_col, (1, FFN_COLS // LANES))
        gt = gt_ref[b]
        hm = []
        for nj in range(f // FFN_COLS):
            cols = slice(nj * FFN_COLS, (nj + 1) * FFN_COLS)
            a = jnp.dot(xc, wgb[:, cols], preferred_element_type=F32)
            issue_some()
            u = jnp.dot(xc, wub[:, cols], preferred_element_type=F32)
            issue_some()
            hm.append((a * jax.nn.sigmoid(a) * u).astype(BF16))
        hm = jnp.concatenate(hm, axis=1)
        for nj in range(d // FFN_COLS):
            cols = slice(nj * FFN_COLS, (nj + 1) * FFN_COLS)
            y = jnp.dot(hm, wdb[:, cols], preferred_element_type=F32) * g_col
            buf[rows, cols] = buf[rows, cols] + gt[:, cols] * y
            issue_some()
        assert not todo

    def chunk_rows(ci):
        return range(ci * FFN_ROWS, (ci + 1) * FFN_ROWS)

    def split(n, parts):
        bounds = [n * k // parts for k in range(parts + 1)]
        return [range(bounds[k], bounds[k + 1]) for k in range(parts)]

    slot = e % 2
    weights = ((wg_ref, wg32), (wu_ref, wu32), (wd_ref, wd32))

    def weights_copy(ee, sl):
        return [pltpu.make_async_copy(src.at[ee], dst.at[sl], wsem.at[sl]) for src, dst in weights]

    @pl.when(e == 0)
    def _():
        for cp in weights_copy(0, 0):
            cp.start(priority=1)

        def first(s, carry):
            gather_start(0, 0, s)
            return carry
        lax.fori_loop(0, cap, first, 0)

    for cp in weights_copy(e, slot):
        cp.wait()

    @pl.when(e + 1 < n_exp)
    def _():
        for cp in weights_copy(e + 1, 1 - slot):
            cp.start(priority=1)

    wgb[...] = wg32[slot].astype(BF16)
    wub[...] = wu32[slot].astype(BF16)
    wdb[...] = wd32[slot].astype(BF16)

    @pl.when(e > 0)
    def _():
        scatter_wait(1)

    gather_wait(0)
    for ci in range(n_chunk):
        todo = [functools.partial(gather_start, e, 1, s) for s in chunk_rows(ci)]
        if ci > 0:
            todo += [functools.partial(scatter_start, e, 0, s) for s in chunk_rows(ci - 1)]
        compute_chunk(0, ci, todo)

    ne = jnp.minimum(e + 1, n_exp - 1)
    gather_wait(1)
    compute_chunk(1, 0, [functools.partial(scatter_start, e, 0, s) for s in chunk_rows(n_chunk - 1)])
    scatter_wait(0)
    for ci, part in zip(range(1, n_chunk), split(cap, n_chunk - 1)):
        todo = [functools.partial(gather_start, ne, 0, s) for s in part]
        todo += [functools.partial(scatter_start, e, 1, s) for s in chunk_rows(ci - 1)]
        compute_chunk(1, ci, todo)
    for s in chunk_rows(n_chunk - 1):
        scatter_start(e, 1, s)

    @pl.when(e == n_exp - 1)
    def _():
        scatter_wait(1)
        gather_wait(0)


def _ffn_call(idx, rows, gate, gt2, w_gate, w_up, w_down, cap):
    bsz, seq, d2 = rows.shape
    d = d2 // 2
    n_exp, _, f = w_gate.shape
    assert bsz == 2, "ffn double-buffers exactly two samples per expert"
    kern = functools.partial(_ffn_kernel, cap=cap, n_exp=n_exp, d=d)
    any_spec = pl.BlockSpec(memory_space=pl.ANY)
    out = pl.pallas_call(
        kern,
        grid_spec=pltpu.PrefetchScalarGridSpec(
            num_scalar_prefetch=1,
            grid=(n_exp,),
            in_specs=[any_spec,
                      pl.BlockSpec((bsz, None, cap // LANES, LANES), lambda e, idx_ref: (0, e, 0, 0)),
                      pl.BlockSpec((bsz, 1, d), lambda e, idx_ref: (0, 0, 0)),
                      any_spec, any_spec, any_spec],
            out_specs=any_spec,
            scratch_shapes=[pltpu.VMEM((cap, d2), F32), pltpu.VMEM((cap, d2), F32),
                            pltpu.VMEM((2, d, f), F32), pltpu.VMEM((2, d, f), F32), pltpu.VMEM((2, f, d), F32),
                            pltpu.VMEM((d, f), BF16), pltpu.VMEM((d, f), BF16), pltpu.VMEM((f, d), BF16),
                            pltpu.SemaphoreType.DMA((bsz,)), pltpu.SemaphoreType.DMA((bsz,)),
                            pltpu.SemaphoreType.DMA((2,))],
        ),
        out_shape=jax.ShapeDtypeStruct((bsz * seq, d2), F32),
        input_output_aliases={1: 0},
        compiler_params=_cparams(("arbitrary",)),
        name="ffn",
    )(idx, rows.reshape(bsz * seq, d2), gate, gt2, w_gate, w_up, w_down)
    return out.reshape(bsz, seq, d2)


def _final_kernel(x_ref, g_ref, o_ref):
    xf = x_ref[...]
    o_ref[...] = xf * lax.rsqrt(jnp.mean(xf * xf, axis=-1, keepdims=True) + EPS) * g_ref[...]


def _final_call(rows, g, tm=2048):
    bsz, seq, d2 = rows.shape
    d = d2 // 2
    return pl.pallas_call(
        _final_kernel,
        grid=(bsz, seq // tm),
        in_specs=[pl.BlockSpec((None, tm, d), lambda b, i: (b, i, 0)),
                  pl.BlockSpec((1, d), lambda b, i: (0, 0))],
        out_specs=pl.BlockSpec((None, tm, d), lambda b, i: (b, i, 0)),
        out_shape=jax.ShapeDtypeStruct((bsz, seq, d), F32),
        compiler_params=_cparams(("arbitrary", "arbitrary")),
        name="final",
    )(rows, g)


def _rope_tables(seq):
    t = np.arange(seq)
    n_freq = HEAD_DIM // 4
    inv = ROPE_THETA ** (-np.arange(n_freq, dtype=np.float64) / n_freq)
    ang = np.concatenate([(t // GRID_W)[:, None] * inv, (t % GRID_W)[:, None] * inv], axis=1)
    cos_h = np.concatenate([np.cos(ang), np.cos(ang)], axis=1)
    sin_h = np.concatenate([-np.sin(ang), np.sin(ang)], axis=1)
    reps = LANES // HEAD_DIM
    return (jnp.asarray(np.tile(cos_h, (1, reps)), F32), jnp.asarray(np.tile(sin_h, (1, reps)), F32))


def kernel(x, c, ctx, c_ctx, w_ada, b_ada, g_mix, w_in, conv_w, sink, w_out, g_ffn,
           w_router, w_gate, w_up, w_down, g_final):
    bsz, seq, d = x.shape
    n_exp = w_router.shape[-1]
    cap = EC_CAPACITY_FACTOR * seq // n_exp
    assert w_ada.shape[0] == 1, "single trunk layer"
    assert bsz + 1 <= SUBLANES and seq % (SUBLANES * LANES) == 0 and cap % SLOT_BLOCK == 0

    cin = jnp.zeros((SUBLANES, d), F32).at[:bsz].set(c).at[bsz].set(c_ctx)
    mod = _mod_call(cin, w_ada[0], b_ada[0][None, :])
    sh1, sc1, gt1, sh2, sc2, gt2 = [m[:bsz, None, :] for m in jnp.split(mod, 6, axis=-1)]
    csh1 = mod[bsz:bsz + 1, 0:d]
    csc1 = mod[bsz:bsz + 1, d:2 * d]

    w_in0 = w_in[0]
    g_mix_r = g_mix[0][None, :]
    kv0 = ATT_WIDTH
    k_c, v_c = _ctx_call(ctx, g_mix_r, csh1, csc1, w_in0[:, kv0:kv0 + 2 * KV_WIDTH])

    cos_t, sin_t = _rope_tables(seq)
    q, k, v, conv = _in_call(x, g_mix_r, sh1, sc1, w_in0, conv_w[0], cos_t, sin_t)
    att = _attn_call(sink[0], q, k, v, k_c, v_c)
    rows, aff = _out_call(att, conv, w_out[0], x, gt1, g_ffn[0][None, :], sh2, sc2, w_router[0].T)

    n_chunk = seq // LANES
    aff4 = aff.reshape(bsz, n_exp, n_chunk, LANES)
    pos, carry = _sel_call(aff4, cap)
    n_tchunk = n_chunk // SUBLANES
    cum = jnp.concatenate([carry[:, :, ::SUBLANES, 0], jnp.full((bsz, n_exp, 1), cap, F32)], axis=-1)
    cum = cum.astype(jnp.int32).reshape(-1)
    comp = _compact_call(cum, pos.reshape(bsz, n_exp, n_tchunk, SUBLANES, LANES),
                         aff4.reshape(bsz, n_exp, n_tchunk, SUBLANES, LANES), cap)
    idx = (comp[:, :, :, 0, :] * LANES + comp[:, :, :, 1, :]).astype(jnp.int32)
    idx = (idx + (jnp.arange(bsz, dtype=jnp.int32) * seq)[:, None, None, None]).reshape(-1)
    gate = (comp[:, :, :, 2, :] + comp[:, :, :, 3, :]) + comp[:, :, :, 4, :]

    rows = _ffn_call(idx, rows, gate, gt2, w_gate[0], w_up[0], w_down[0], cap)
    return _final_call(rows, g_final[None, :])
```

```python
import functools

import numpy as np
import jax
import jax.numpy as jnp
from jax import lax
from jax.experimental import pallas as pl
from jax.experimental.pallas import tpu as pltpu

GRID_W = 64
N_HEADS = 8
N_KV_HEADS = 2
HEAD_DIM = 64
GQA_GROUP = N_HEADS // N_KV_HEADS
ATT_WIDTH = N_HEADS * HEAD_DIM
KV_WIDTH = N_KV_HEADS * HEAD_DIM
WINDOW = 128
BLOCK = 128
ROPE_THETA = 10000.0
N_EXPERTS = 16
EC_CAPACITY_FACTOR = 2
EPS = 1e-6

LANES = 128
SUBLANES = 8
MXU_COLS = 256
VMEM_LIMIT = 56 * 1024 * 1024

F32 = jnp.float32
BF16 = jnp.bfloat16
NEG = -1e30
LOG2E = 1.4426950408889634


def _cparams(sem):
    return pltpu.CompilerParams(dimension_semantics=sem, vmem_limit_bytes=VMEM_LIMIT)


def _dot(a, b):
    return jnp.dot(a, b, precision=lax.Precision.DEFAULT, preferred_element_type=F32)


def _dot_nt(a, b):
    return lax.dot_general(a, b, (((1,), (1,)), ((), ())), precision=lax.Precision.DEFAULT,
                           preferred_element_type=F32)


def _norm_mod(xf, g, shift, scale):
    r = xf * lax.rsqrt(jnp.mean(xf * xf, axis=-1, keepdims=True) + EPS)
    return (r * g) * (1.0 + scale) + shift


def _mod_kernel(c_ref, w_ref, b_ref, o_ref):
    c = c_ref[...]
    s = c * jax.nn.sigmoid(c)
    o_ref[...] = _dot(s, w_ref[...]) + b_ref[...]


def _mod_call(cin, w, bias, tn=1024):
    d, n = w.shape
    return pl.pallas_call(
        _mod_kernel,
        grid=(n // tn,),
        in_specs=[pl.BlockSpec((SUBLANES, d), lambda j: (0, 0)),
                  pl.BlockSpec((d, tn), lambda j: (0, j)),
                  pl.BlockSpec((1, tn), lambda j: (0, j))],
        out_specs=pl.BlockSpec((SUBLANES, tn), lambda j: (0, j)),
        out_shape=jax.ShapeDtypeStruct((SUBLANES, n), F32),
        compiler_params=_cparams(("arbitrary",)),
        name="mod",
    )(cin, w, bias)


def _ctx_kernel(x_ref, g_ref, sh_ref, sc_ref, w_ref, k_ref, v_ref):
    h = _norm_mod(x_ref[...], g_ref[...], sh_ref[...], sc_ref[...])
    kv = _dot(h, w_ref[...])
    k_ref[...] = kv[:, :KV_WIDTH]
    v_ref[...] = kv[:, KV_WIDTH:]


def _ctx_call(ctx, g, sh, sc, w_kv):
    bsz, n, d = ctx.shape
    vec = pl.BlockSpec((1, d), lambda b: (0, 0))
    kv_spec = pl.BlockSpec((None, n, KV_WIDTH), lambda b: (b, 0, 0))
    return pl.pallas_call(
        _ctx_kernel,
        grid=(bsz,),
        in_specs=[pl.BlockSpec((None, n, d), lambda b: (b, 0, 0)), vec, vec, vec,
                  pl.BlockSpec((d, 2 * KV_WIDTH), lambda b: (0, 0))],
        out_specs=[kv_spec, kv_spec],
        out_shape=[jax.ShapeDtypeStruct((bsz, n, KV_WIDTH), F32)] * 2,
        compiler_params=_cparams(("arbitrary",)),
        name="ctx_kv",
    )(ctx, g, sh, sc, w_kv)


def _rope(t, cos, sin_signed):
    lane = lax.broadcasted_iota(jnp.int32, t.shape, 1)
    first = (lane % HEAD_DIM) < (HEAD_DIM // 2)
    swapped = jnp.where(first, pltpu.roll(t, LANES - HEAD_DIM // 2, 1), pltpu.roll(t, HEAD_DIM // 2, 1))
    return t * cos + swapped * sin_signed


def _in_kernel(x_ref, xp_ref, xn_ref, g_ref, sh_ref, sc_ref, w_ref, cw_ref, cos_ref, sin_ref,
               q_ref, k_ref, v_ref, conv_ref, *, conv_width, tm):
    i = pl.program_id(1)
    last = pl.num_programs(1) - 1
    x_ext = jnp.concatenate([x_ref[...], xp_ref[...], xn_ref[...]], axis=0)
    h_ext = _norm_mod(x_ext, g_ref[...], sh_ref[...], sc_ref[...])
    h = h_ext[0:tm]
    cos = cos_ref[...]
    sin = sin_ref[...]
    scale = HEAD_DIM ** -0.5 * LOG2E
    for j in range(ATT_WIDTH // MXU_COLS):
        qj = _dot(h, w_ref[:, j * MXU_COLS:(j + 1) * MXU_COLS])
        for jj in range(MXU_COLS // LANES):
            c0 = j * MXU_COLS + jj * LANES
            q_ref[:, c0:c0 + LANES] = _rope(qj[:, jj * LANES:(jj + 1) * LANES], cos, sin) * scale
    o = ATT_WIDTH
    kv = _dot(h, w_ref[:, o:o + 2 * KV_WIDTH])
    k_ref[...] = _rope(kv[:, :KV_WIDTH], cos, sin)
    v_ref[...] = kv[:, KV_WIDTH:]
    o += 2 * KV_WIDTH
    bg = _dot(h, w_ref[:, o:o + conv_width])
    o += conv_width
    cg = _dot(h_ext, w_ref[:, o:o + conv_width])
    o += conv_width
    hv = _dot(h_ext, w_ref[:, o:o + conv_width])
    u_ext = cg * hv
    u = u_ext[0:tm]
    row = lax.broadcasted_iota(jnp.int32, u.shape, 0)
    prev_row = jnp.where(i > 0, u_ext[tm + SUBLANES - 1:tm + SUBLANES], 0.0)
    next_row = jnp.where(i < last, u_ext[tm + SUBLANES:tm + SUBLANES + 1], 0.0)
    u_prev = jnp.where(row == 0, prev_row, pltpu.roll(u, 1, 0))
    u_next = jnp.where(row == tm - 1, next_row, pltpu.roll(u, tm - 1, 0))
    conv_ref[...] = bg * (cw_ref[0:1, :] * u_prev + cw_ref[1:2, :] * u + cw_ref[2:3, :] * u_next)


def _in_call(x, g, sh, sc, w_in, conv_w, cos_t, sin_t, tm=1024):
    bsz, seq, d = x.shape
    n_in = w_in.shape[1]
    cw = d - ATT_WIDTH
    nh = seq // SUBLANES
    vec = pl.BlockSpec((1, d), lambda b, i: (0, 0))
    bvec = pl.BlockSpec((None, 1, d), lambda b, i: (b, 0, 0))
    tab = pl.BlockSpec((tm, LANES), lambda b, i: (i, 0))
    halo_prev = pl.BlockSpec((None, SUBLANES, d),
                             lambda b, i: (b, jnp.maximum(i * (tm // SUBLANES) - 1, 0), 0))
    halo_next = pl.BlockSpec((None, SUBLANES, d),
                             lambda b, i: (b, jnp.minimum((i + 1) * (tm // SUBLANES), nh - 1), 0))

    def rows(width):
        return pl.BlockSpec((None, tm, width), lambda b, i: (b, i, 0))

    return pl.pallas_call(
        functools.partial(_in_kernel, conv_width=cw, tm=tm),
        grid=(bsz, seq // tm),
        in_specs=[rows(d), halo_prev, halo_next, vec, bvec, bvec,
                  pl.BlockSpec((d, n_in), lambda b, i: (0, 0)),
                  pl.BlockSpec((3, cw), lambda b, i: (0, 0)), tab, tab],
        out_specs=[rows(ATT_WIDTH), rows(KV_WIDTH), rows(KV_WIDTH), rows(cw)],
        out_shape=[jax.ShapeDtypeStruct((bsz, seq, ATT_WIDTH), F32),
                   jax.ShapeDtypeStruct((bsz, seq, KV_WIDTH), F32),
                   jax.ShapeDtypeStruct((bsz, seq, KV_WIDTH), F32),
                   jax.ShapeDtypeStruct((bsz, seq, cw), F32)],
        compiler_params=_cparams(("arbitrary", "arbitrary")),
        name="in_proj",
    )(x, x, x, g, sh, sc, w_in, conv_w, cos_t, sin_t)


ATTN_TILE = 32


def _attn_kernel(sink_ref, q_ref, k_ref, v_ref, kc_ref, vc_ref, o_ref,
                 s_scr, p_scr, bias_scr, m_scr, *, seq, tq):
    i = pl.program_id(1)
    nblk = tq // BLOCK
    span = 3 * BLOCK
    width = s_scr.shape[1]
    rows_all = GQA_GROUP * BLOCK
    tiles_per_head = BLOCK // ATTN_TILE
    kc = kc_ref[...]
    vc = vc_ref[...]
    for blk in range(nblk):
        n = i * nblk + blk
        ws = pl.multiple_of(jnp.clip((n - 1) * BLOCK, 0, seq - span), BLOCK)
        kw = k_ref[pl.ds(ws, span), :]
        vw = v_ref[pl.ds(ws, span), :]
        qb = q_ref[blk * BLOCK:(blk + 1) * BLOCK, :]
        qpos = n * BLOCK + lax.broadcasted_iota(jnp.int32, (BLOCK, span), 0)
        kpos = ws + lax.broadcasted_iota(jnp.int32, (BLOCK, span), 1)
        bias_scr[...] = jnp.where(jnp.abs(kpos - qpos) <= WINDOW, 0.0, NEG)
        outs = []
        for kh in range(N_KV_HEADS):
            heads = [kh * GQA_GROUP + g for g in range(GQA_GROUP)]
            qs = jnp.concatenate([qb[:, h * HEAD_DIM:(h + 1) * HEAD_DIM] for h in heads], axis=0)
            lo, hi = kh * HEAD_DIM, (kh + 1) * HEAD_DIM
            s_scr[:, 0:span] = _dot_nt(qs, kw[:, lo:hi])
            s_scr[:, span:width] = _dot_nt(qs, kc[:, lo:hi])

            for t in range(rows_all // ATTN_TILE):
                r0 = t * ATTN_TILE
                b0 = (t % tiles_per_head) * ATTN_TILE
                sk = sink_ref[kh * GQA_GROUP + t // tiles_per_head] * LOG2E
                s_loc = s_scr[pl.ds(r0, ATTN_TILE), 0:span] + bias_scr[pl.ds(b0, ATTN_TILE), :]
                s_ctx = s_scr[pl.ds(r0, ATTN_TILE), span:width]
                m = jnp.maximum(jnp.maximum(jnp.max(s_loc, axis=1, keepdims=True),
                                            jnp.max(s_ctx, axis=1, keepdims=True)), sk)
                m_scr[pl.ds(r0, ATTN_TILE), :] = m
                p_scr[pl.ds(r0, ATTN_TILE), 0:span] = jnp.exp2(s_loc - m)
                p_scr[pl.ds(r0, ATTN_TILE), span:width] = jnp.exp2(s_ctx - m)
            own = (lax.broadcasted_iota(jnp.int32, (1, KV_WIDTH), 1) // HEAD_DIM) == kh
            o_ext = (_dot(p_scr[:, 0:span], jnp.where(own, vw, 1.0))
                     + _dot(p_scr[:, span:width], jnp.where(own, vc, 1.0)))
            other = (1 - kh) * HEAD_DIM
            sk_col = jnp.concatenate([jnp.full((BLOCK, 1), sink_ref[h] * LOG2E, F32) for h in heads], axis=0)
            den = o_ext[:, other:other + 1] + jnp.exp2(sk_col - m_scr[...])
            o = o_ext[:, lo:hi] / den
            outs.extend(o[g * BLOCK:(g + 1) * BLOCK] for g in range(GQA_GROUP))
        o_ref[blk * BLOCK:(blk + 1) * BLOCK, :] = jnp.concatenate(outs, axis=1)


def _attn_call(sink, q, k, v, kc, vc, tq=1024):
    bsz, seq, _ = q.shape
    n_ctx = kc.shape[1]
    span = 3 * BLOCK
    rows_all = GQA_GROUP * BLOCK
    full_kv = pl.BlockSpec((None, seq, KV_WIDTH), lambda b, i: (b, 0, 0))
    ctx_kv = pl.BlockSpec((None, n_ctx, KV_WIDTH), lambda b, i: (b, 0, 0))
    qspec = pl.BlockSpec((None, tq, ATT_WIDTH), lambda b, i: (b, i, 0))
    return pl.pallas_call(
        functools.partial(_attn_kernel, seq=seq, tq=tq),
        grid=(bsz, seq // tq),
        in_specs=[pl.BlockSpec(memory_space=pltpu.SMEM), qspec, full_kv, full_kv, ctx_kv, ctx_kv],
        out_specs=qspec,
        out_shape=jax.ShapeDtypeStruct((bsz, seq, ATT_WIDTH), F32),
        scratch_shapes=[pltpu.VMEM((rows_all, span + n_ctx), F32),
                        pltpu.VMEM((rows_all, span + n_ctx), F32),
                        pltpu.VMEM((BLOCK, span), F32),
                        pltpu.VMEM((rows_all, 1), F32)],
        compiler_params=_cparams(("arbitrary", "arbitrary")),
        name="attn",
    )(sink, q, k, v, kc, vc)


def _out_kernel(att_ref, conv_ref, wo_ref, x_ref, gt_ref,
                g_ref, sh_ref, sc_ref, wr_ref, rows_ref, aff_ref):
    y = _dot(att_ref[...], wo_ref[0:ATT_WIDTH, :]) + _dot(conv_ref[...], wo_ref[ATT_WIDTH:, :])
    x1 = x_ref[...] + gt_ref[...] * y
    d = x1.shape[-1]
    h2 = _norm_mod(x1, g_ref[...], sh_ref[...], sc_ref[...])
    rows_ref[:, 0:d] = x1
    rows_ref[:, d:2 * d] = h2
    def split3(t):
        hi = t.astype(BF16)
        r1 = t - hi.astype(F32)
        mid = r1.astype(BF16)
        return hi, mid, (r1 - mid.astype(F32)).astype(BF16)

    def nt_bf16(a, b):
        return lax.dot_general(a, b, (((1,), (1,)), ((), ())), preferred_element_type=F32)

    h_hi, h_mid, h_lo = split3(h2)
    w_hi, w_mid, w_lo = split3(wr_ref[...])
    n_exp = w_hi.shape[0]
    by_hi = nt_bf16(jnp.concatenate([w_hi, w_mid, w_lo], axis=0), h_hi)
    by_mid = nt_bf16(jnp.concatenate([w_hi, w_mid], axis=0), h_mid)
    logits = (((by_hi[2 * n_exp:] + by_mid[n_exp:]) + nt_bf16(w_hi, h_lo))
              + (by_hi[n_exp:2 * n_exp] + by_mid[:n_exp])) + by_hi[:n_exp]
    mx = jnp.max(logits, axis=0, keepdims=True)
    ex = jnp.exp(logits - mx)
    aff_ref[...] = ex / jnp.sum(ex, axis=0, keepdims=True)


def _out_call(att, conv, w_out, x, gt1, g_ffn, sh2, sc2, w_router_t, tm=1024):
    bsz, seq, d = x.shape
    cw = conv.shape[-1]
    n_exp = w_router_t.shape[0]

    def rows(width):
        return pl.BlockSpec((None, tm, width), lambda b, i: (b, i, 0))

    vec = pl.BlockSpec((1, d), lambda b, i: (0, 0))
    bvec = pl.BlockSpec((None, 1, d), lambda b, i: (b, 0, 0))
    return pl.pallas_call(
        _out_kernel,
        grid=(bsz, seq // tm),
        in_specs=[rows(ATT_WIDTH), rows(cw),
                  pl.BlockSpec((d, d), lambda b, i: (0, 0)),
                  rows(d), bvec, vec, bvec, bvec,
                  pl.BlockSpec((n_exp, d), lambda b, i: (0, 0))],
        out_specs=[rows(2 * d), pl.BlockSpec((None, n_exp, tm), lambda b, i: (b, 0, i))],
        out_shape=[jax.ShapeDtypeStruct((bsz, seq, 2 * d), F32),
                   jax.ShapeDtypeStruct((bsz, n_exp, seq), F32)],
        compiler_params=_cparams(("arbitrary", "arbitrary")),
        name="out_proj",
    )(att, conv, w_out, x, gt1, g_ffn, sh2, sc2, w_router_t)


def _prefix_rows(m, tri, ones, low):
    n_exp, n_chunk, _ = m.shape
    mb = m.astype(BF16).reshape(n_exp * n_chunk, LANES)
    within = jnp.dot(mb, tri, preferred_element_type=F32).reshape(n_exp, n_chunk, LANES)
    tot = jnp.dot(mb, ones, preferred_element_type=F32).astype(BF16).reshape(n_exp, n_chunk, LANES)
    carry = jnp.stack([jnp.dot(low, tot[e], preferred_element_type=F32) for e in range(n_exp)], axis=0)
    return carry + within, carry


def _sel_kernel(aff_ref, pos_ref, carry_ref, *, cap):
    a = aff_ref[...]
    n_exp, n_chunk, _ = a.shape
    bits = lax.bitcast_convert_type(a, jnp.int32)

    def count(pred):
        c = jnp.sum(pred.astype(F32), axis=1, keepdims=True)
        return jnp.sum(c, axis=2, keepdims=True)

    def body(it, thr):
        shift = 29 - 2 * it
        for k in (1, 2, 3):
            cand = thr | jnp.left_shift(jnp.int32(k), shift)
            ok = count(bits >= cand) >= cap
            best = jnp.where(ok, cand, thr) if k == 1 else jnp.where(ok, cand, best)
        return best

    thr = lax.fori_loop(0, 15, body, jnp.zeros((n_exp, 1, 1), jnp.int32))
    thr = jnp.where(count(bits >= (thr | 1)) >= cap, thr | 1, thr)
    gt = bits > thr
    eq = bits == thr
    need = cap - count(gt)

    r = lax.broadcasted_iota(jnp.int32, (LANES, LANES), 0)
    c = lax.broadcasted_iota(jnp.int32, (LANES, LANES), 1)
    tri = (r < c).astype(BF16)
    ones = jnp.ones((LANES, LANES), BF16)
    rr = lax.broadcasted_iota(jnp.int32, (n_chunk, n_chunk), 0)
    cc = lax.broadcasted_iota(jnp.int32, (n_chunk, n_chunk), 1)
    low = (cc < rr).astype(BF16)

    eq_rank, _ = _prefix_rows(eq, tri, ones, low)
    sel = gt | (eq & (eq_rank < need))
    pos, carry = _prefix_rows(sel, tri, ones, low)
    pos_ref[...] = jnp.where(sel, pos, -1.0)
    carry_ref[...] = carry


def _sel_call(aff4, cap):
    bsz, n_exp, n_chunk, _ = aff4.shape
    spec = pl.BlockSpec((None, n_exp, n_chunk, LANES), lambda b: (b, 0, 0, 0))
    return pl.pallas_call(
        functools.partial(_sel_kernel, cap=cap),
        grid=(bsz,),
        in_specs=[spec],
        out_specs=[spec, spec],
        out_shape=[jax.ShapeDtypeStruct(aff4.shape, F32)] * 2,
        compiler_params=_cparams(("arbitrary",)),
        name="select",
    )(aff4)


COMPACT_ROWS = 16
SLOT_BLOCK = LANES


def _compact_kernel(cum_ref, pos_ref, aff_ref, out_ref, *, n_exp, n_tchunk):
    b = pl.program_id(0)
    e = pl.program_id(1)
    base = (b * n_exp + e) * (n_tchunk + 1)
    out_ref[...] = jnp.zeros(out_ref.shape, F32)
    nt = (((1,), (1,)), ((), ()))
    rowi = lax.broadcasted_iota(jnp.int32, (COMPACT_ROWS, LANES), 0)
    lane = lax.broadcasted_iota(jnp.int32, (COMPACT_ROWS, LANES), 1).astype(F32)
    slot_b = lax.broadcasted_iota(jnp.int32, (SLOT_BLOCK, LANES), 0).astype(F32).astype(BF16)
    one_b = jnp.ones((), BF16)
    zero_b = jnp.zeros((), BF16)

    n_sblk = out_ref.shape[0] - 1

    def chunk_vals(c):
        a = aff_ref[c]
        g_hi = a.astype(BF16).astype(F32)
        r1 = a - g_hi
        g_mid = r1.astype(BF16).astype(F32)
        g_lo = r1 - g_mid
        pieces = []
        for r in range(SUBLANES):
            def bc(t):
                return jnp.broadcast_to(t[r:r + 1, :], (COMPACT_ROWS, LANES))

            piece = jnp.where(rowi == 0, float(c * SUBLANES + r),
                              jnp.where(rowi == 1, lane,
                                        jnp.where(rowi == 2, bc(g_hi),
                                                  jnp.where(rowi == 3, bc(g_mid),
                                                            jnp.where(rowi == 4, bc(g_lo), 0.0)))))
            pieces.append(piece)
        return jnp.concatenate(pieces, axis=1).astype(BF16)

    def visit(j, p, vals):
        rel = p - jnp.asarray(j * SLOT_BLOCK).astype(F32)
        rel = jnp.where((rel >= 0.0) & (rel < SLOT_BLOCK), rel, -1.0)
        pieces_oh = []
        for r in range(SUBLANES):
            row16 = jnp.broadcast_to(rel[r:r + 1, :], (2 * SUBLANES, LANES)).astype(BF16)
            blk = jnp.concatenate([row16] * (SLOT_BLOCK // (2 * SUBLANES)), axis=0)
            pieces_oh.append(jnp.where(blk == slot_b, one_b, zero_b))
        oh = jnp.concatenate(pieces_oh, axis=1)
        out_ref[j] += lax.dot_general(vals, oh, nt, preferred_element_type=F32)

    def first_block(c):
        return jnp.minimum(cum_ref[base + c] // SLOT_BLOCK, n_sblk - 1)

    for c in range(n_tchunk):
        p = pos_ref[c]
        vals = chunk_vals(c)
        j0 = first_block(c)
        visit(j0, p, vals)
        visit(j0 + 1, p, vals)

    for c in range(n_tchunk):
        lo = cum_ref[base + c]
        hi = cum_ref[base + c + 1]
        j_end = jnp.where(hi > lo, (hi - 1) // SLOT_BLOCK + 1, 0)

        def more(j, carry, c=c):
            visit(j, pos_ref[c], chunk_vals(c))
            return carry

        lax.fori_loop(first_block(c) + 2, j_end, more, 0)


def _compact_call(cum, pos5, aff5, cap):
    bsz, n_exp, n_tchunk, _, _ = pos5.shape
    n_sblk = cap // SLOT_BLOCK
    spec = pl.BlockSpec((None, None, n_tchunk, SUBLANES, LANES), lambda b, e, cum_ref: (b, e, 0, 0, 0))
    return pl.pallas_call(
        functools.partial(_compact_kernel, n_exp=n_exp, n_tchunk=n_tchunk),
        grid_spec=pltpu.PrefetchScalarGridSpec(
            num_scalar_prefetch=1,
            grid=(bsz, n_exp),
            in_specs=[spec, spec],
            out_specs=pl.BlockSpec((None, None, n_sblk + 1, COMPACT_ROWS, LANES),
                                   lambda b, e, cum_ref: (b, e, 0, 0, 0)),
        ),
        out_shape=jax.ShapeDtypeStruct((bsz, n_exp, n_sblk + 1, COMPACT_ROWS, LANES), F32),
        compiler_params=_cparams(("arbitrary", "arbitrary")),
        name="compact",
    )(cum, pos5, aff5)[:, :, :n_sblk]


FFN_ROWS = 256


FFN_COLS = 256


def _ffn_kernel(idx_ref, rows_in, g_ref, gt_ref, wg_ref, wu_ref, wd_ref, rows_hbm,
                buf0, buf1, wg32, wu32, wd32, wgb, wub, wdb, gsem, ssem, wsem, *, cap, n_exp, d):
    del rows_in
    e = pl.program_id(0)
    bufs = (buf0, buf1)
    n_chunk = cap // FFN_ROWS
    f = wgb.shape[1]
    acc_cols = pl.ds(0, d)

    def hbm_row(ee, b, s):
        return idx_ref[(b * n_exp + ee) * cap + s]

    gather_thread, scatter_thread = 0, 1

    def gather_start(ee, b, s):
        pltpu.make_async_copy(rows_hbm.at[pl.ds(hbm_row(ee, b, s), 1)], bufs[b].at[pl.ds(s, 1)],
                              gsem.at[b]).start(priority=gather_thread)

    def gather_wait(b):
        pltpu.make_async_copy(rows_hbm.at[pl.ds(0, cap)], bufs[b], gsem.at[b]).wait()

    def scatter_start(ee, b, s):
        pltpu.make_async_copy(bufs[b].at[pl.ds(s, 1), acc_cols],
                              rows_hbm.at[pl.ds(hbm_row(ee, b, s), 1), acc_cols],
                              ssem.at[b]).start(priority=scatter_thread)

    def scatter_wait(b):
        pltpu.make_async_copy(bufs[b].at[:, acc_cols], rows_hbm.at[pl.ds(0, cap), acc_cols],
                              ssem.at[b]).wait()

    def compute_chunk(b, ci, todo):
        buf = bufs[b]
        rows = slice(ci * FFN_ROWS, (ci + 1) * FFN_ROWS)
        n_piece = 3 * (f // FFN_COLS)
        per_piece = -(-len(todo) // n_piece)

        def issue_some():
            for _ in range(min(per_piece, len(todo))):
                todo.pop(0)()

        xc = buf[rows, d:2 * d].astype(BF16)
        g_col = jnp.concatenate(
            [jnp.broadcast_to(g_ref[b, j:j + 1, :], (LANES, LANES)).T
             for j in range(ci * FFN_ROWS // LANES, (ci + 1) * FFN_ROWS // LANES)], axis=0)
        g_col = jnp.tile(g_col, (1, FFN_COLS // LANES))
        gt = gt_ref[b]
        hm = []
        for nj in range(f // FFN_COLS):
            cols = slice(nj * FFN_COLS, (nj + 1) * FFN_COLS)
            a = jnp.dot(xc, wgb[:, cols], preferred_element_type=F32)
            issue_some()
            u = jnp.dot(xc, wub[:, cols], preferred_element_type=F32)
            issue_some()
            hm.append((a * jax.nn.sigmoid(a) * u).astype(BF16))
        hm = jnp.concatenate(hm, axis=1)
        for nj in range(d // FFN_COLS):
            cols = slice(nj * FFN_COLS, (nj + 1) * FFN_COLS)
            y = jnp.dot(hm, wdb[:, cols], preferred_element_type=F32) * g_col
            buf[rows, cols] = buf[rows, cols] + gt[:, cols] * y
            issue_some()
        assert not todo

    def chunk_rows(ci):
        return range(ci * FFN_ROWS, (ci + 1) * FFN_ROWS)

    def split(n, parts):
        bounds = [n * k // parts for k in range(parts + 1)]
        return [range(bounds[k], bounds[k + 1]) for k in range(parts)]

    slot = e % 2
    weights = ((wg_ref, wg32), (wu_ref, wu32), (wd_ref, wd32))

    def weights_copy(ee, sl):
        return [pltpu.make_async_copy(src.at[ee], dst.at[sl], wsem.at[sl]) for src, dst in weights]

    @pl.when(e == 0)
    def _():
        for cp in weights_copy(0, 0):
            cp.start(priority=1)

        def first(s, carry):
            gather_start(0, 0, s)
            return carry
        lax.fori_loop(0, cap, first, 0)

    for cp in weights_copy(e, slot):
        cp.wait()

    @pl.when(e + 1 < n_exp)
    def _():
        for cp in weights_copy(e + 1, 1 - slot):
            cp.start(priority=1)

    wgb[...] = wg32[slot].astype(BF16)
    wub[...] = wu32[slot].astype(BF16)
    wdb[...] = wd32[slot].astype(BF16)

    @pl.when(e > 0)
    def _():
        scatter_wait(1)

    gather_wait(0)
    for ci in range(n_chunk):
        todo = [functools.partial(gather_start, e, 1, s) for s in chunk_rows(ci)]
        if ci > 0:
            todo += [functools.partial(scatter_start, e, 0, s) for s in chunk_rows(ci - 1)]
        compute_chunk(0, ci, todo)

    ne = jnp.minimum(e + 1, n_exp - 1)
    gather_wait(1)
    compute_chunk(1, 0, [functools.partial(scatter_start, e, 0, s) for s in chunk_rows(n_chunk - 1)])
    scatter_wait(0)
    for ci, part in zip(range(1, n_chunk), split(cap, n_chunk - 1)):
        todo = [functools.partial(gather_start, ne, 0, s) for s in part]
        todo += [functools.partial(scatter_start, e, 1, s) for s in chunk_rows(ci - 1)]
        compute_chunk(1, ci, todo)
    for s in chunk_rows(n_chunk - 1):
        scatter_start(e, 1, s)

    @pl.when(e == n_exp - 1)
    def _():
        scatter_wait(1)
        gather_wait(0)


def _ffn_call(idx, rows, gate, gt2, w_gate, w_up, w_down, cap):
    bsz, seq, d2 = rows.shape
    d = d2 // 2
    n_exp, _, f = w_gate.shape
    assert bsz == 2, "ffn double-buffers exactly two samples per expert"
    kern = functools.partial(_ffn_kernel, cap=cap, n_exp=n_exp, d=d)
    any_spec = pl.BlockSpec(memory_space=pl.ANY)
    out = pl.pallas_call(
        kern,
        grid_spec=pltpu.PrefetchScalarGridSpec(
            num_scalar_prefetch=1,
            grid=(n_exp,),
            in_specs=[any_spec,
                      pl.BlockSpec((bsz, None, cap // LANES, LANES), lambda e, idx_ref: (0, e, 0, 0)),
                      pl.BlockSpec((bsz, 1, d), lambda e, idx_ref: (0, 0, 0)),
                      any_spec, any_spec, any_spec],
            out_specs=any_spec,
            scratch_shapes=[pltpu.VMEM((cap, d2), F32), pltpu.VMEM((cap, d2), F32),
                            pltpu.VMEM((2, d, f), F32), pltpu.VMEM((2, d, f), F32), pltpu.VMEM((2, f, d), F32),
                            pltpu.VMEM((d, f), BF16), pltpu.VMEM((d, f), BF16), pltpu.VMEM((f, d), BF16),
                            pltpu.SemaphoreType.DMA((bsz,)), pltpu.SemaphoreType.DMA((bsz,)),
                            pltpu.SemaphoreType.DMA((2,))],
        ),
        out_shape=jax.ShapeDtypeStruct((bsz * seq, d2), F32),
        input_output_aliases={1: 0},
        compiler_params=_cparams(("arbitrary",)),
        name="ffn",
    )(idx, rows.reshape(bsz * seq, d2), gate, gt2, w_gate, w_up, w_down)
    return out.reshape(bsz, seq, d2)


def _final_kernel(x_ref, g_ref, o_ref):
    xf = x_ref[...]
    o_ref[...] = xf * lax.rsqrt(jnp.mean(xf * xf, axis=-1, keepdims=True) + EPS) * g_ref[...]


def _final_call(rows, g, tm=2048):
    bsz, seq, d2 = rows.shape
    d = d2 // 2
    return pl.pallas_call(
        _final_kernel,
        grid=(bsz, seq // tm),
        in_specs=[pl.BlockSpec((None, tm, d), lambda b, i: (b, i, 0)),
                  pl.BlockSpec((1, d), lambda b, i: (0, 0))],
        out_specs=pl.BlockSpec((None, tm, d), lambda b, i: (b, i, 0)),
        out_shape=jax.ShapeDtypeStruct((bsz, seq, d), F32),
        compiler_params=_cparams(("arbitrary", "arbitrary")),
        name="final",
    )(rows, g)


def _rope_tables(seq):
    t = np.arange(seq)
    n_freq = HEAD_DIM // 4
    inv = ROPE_THETA ** (-np.arange(n_freq, dtype=np.float64) / n_freq)
    ang = np.concatenate([(t // GRID_W)[:, None] * inv, (t % GRID_W)[:, None] * inv], axis=1)
    cos_h = np.concatenate([np.cos(ang), np.cos(ang)], axis=1)
    sin_h = np.concatenate([-np.sin(ang), np.sin(ang)], axis=1)
    reps = LANES // HEAD_DIM
    return (jnp.asarray(np.tile(cos_h, (1, reps)), F32), jnp.asarray(np.tile(sin_h, (1, reps)), F32))


def kernel(x, c, ctx, c_ctx, w_ada, b_ada, g_mix, w_in, conv_w, sink, w_out, g_ffn,
           w_router, w_gate, w_up, w_down, g_final):
    bsz, seq, d = x.shape
    n_exp = w_router.shape[-1]
    cap = EC_CAPACITY_FACTOR * seq // n_exp
    assert w_ada.shape[0] == 1, "single trunk layer"
    assert bsz + 1 <= SUBLANES and seq % (SUBLANES * LANES) == 0 and cap % SLOT_BLOCK == 0

    cin = jnp.zeros((SUBLANES, d), F32).at[:bsz].set(c).at[bsz].set(c_ctx)
    mod = _mod_call(cin, w_ada[0], b_ada[0][None, :])
    sh1, sc1, gt1, sh2, sc2, gt2 = [m[:bsz, None, :] for m in jnp.split(mod, 6, axis=-1)]
    csh1 = mod[bsz:bsz + 1, 0:d]
    csc1 = mod[bsz:bsz + 1, d:2 * d]

    w_in0 = w_in[0]
    g_mix_r = g_mix[0][None, :]
    kv0 = ATT_WIDTH
    k_c, v_c = _ctx_call(ctx, g_mix_r, csh1, csc1, w_in0[:, kv0:kv0 + 2 * KV_WIDTH])

    cos_t, sin_t = _rope_tables(seq)
    q, k, v, conv = _in_call(x, g_mix_r, sh1, sc1, w_in0, conv_w[0], cos_t, sin_t)
    att = _attn_call(sink[0], q, k, v, k_c, v_c)
    rows, aff = _out_call(att, conv, w_out[0], x, gt1, g_ffn[0][None, :], sh2, sc2, w_router[0].T)

    n_chunk = seq // LANES
    aff4 = aff.reshape(bsz, n_exp, n_chunk, LANES)
    pos, carry = _sel_call(aff4, cap)
    n_tchunk = n_chunk // SUBLANES
    cum = jnp.concatenate([carry[:, :, ::SUBLANES, 0], jnp.full((bsz, n_exp, 1), cap, F32)], axis=-1)
    cum = cum.astype(jnp.int32).reshape(-1)
    comp = _compact_call(cum, pos.reshape(bsz, n_exp, n_tchunk, SUBLANES, LANES),
                         aff4.reshape(bsz, n_exp, n_tchunk, SUBLANES, LANES), cap)
    idx = (comp[:, :, :, 0, :] * LANES + comp[:, :, :, 1, :]).astype(jnp.int32)
    idx = (idx + (jnp.arange(bsz, dtype=jnp.int32) * seq)[:, None, None, None]).reshape(-1)
    gate = (comp[:, :, :, 2, :] + comp[:, :, :, 3, :]) + comp[:, :, :, 4, :]

    rows = _ffn_call(idx, rows, gate, gt2, w_gate[0], w_up[0], w_down[0], cap)
    return _final_call(rows, g_final[None, :])
```

```python
import functools

import numpy as np
import jax
import jax.numpy as jnp
from jax import lax
from jax.experimental import pallas as pl
from jax.experimental.pallas import tpu as pltpu

GRID_W = 64
N_HEADS = 8
N_KV_HEADS = 2
HEAD_DIM = 64
GQA_GROUP = N_HEADS // N_KV_HEADS
ATT_WIDTH = N_HEADS * HEAD_DIM
KV_WIDTH = N_KV_HEADS * HEAD_DIM
WINDOW = 128
BLOCK = 128
ROPE_THETA = 10000.0
N_EXPERTS = 16
EC_CAPACITY_FACTOR = 2
EPS = 1e-6

LANES = 128
SUBLANES = 8
MXU_COLS = 256
VMEM_LIMIT = 56 * 1024 * 1024

F32 = jnp.float32
BF16 = jnp.bfloat16
NEG = -1e30
LOG2E = 1.4426950408889634


def _cparams(sem):
    return pltpu.CompilerParams(dimension_semantics=sem, vmem_limit_bytes=VMEM_LIMIT)


def _dot(a, b):
    return jnp.dot(a, b, precision=lax.Precision.DEFAULT, preferred_element_type=F32)


def _dot_nt(a, b):
    return lax.dot_general(a, b, (((1,), (1,)), ((), ())), precision=lax.Precision.DEFAULT,
                           preferred_element_type=F32)


def _norm_mod(xf, g, shift, scale):
    r = xf * lax.rsqrt(jnp.mean(xf * xf, axis=-1, keepdims=True) + EPS)
    return (r * g) * (1.0 + scale) + shift


def _mod_kernel(c_ref, w_ref, b_ref, o_ref):
    c = c_ref[...]
    s = c * jax.nn.sigmoid(c)
    o_ref[...] = _dot(s, w_ref[...]) + b_ref[...]


def _mod_call(cin, w, bias, tn=1024):
    d, n = w.shape
    return pl.pallas_call(
        _mod_kernel,
        grid=(n // tn,),
        in_specs=[pl.BlockSpec((SUBLANES, d), lambda j: (0, 0)),
                  pl.BlockSpec((d, tn), lambda j: (0, j)),
                  pl.BlockSpec((1, tn), lambda j: (0, j))],
        out_specs=pl.BlockSpec((SUBLANES, tn), lambda j: (0, j)),
        out_shape=jax.ShapeDtypeStruct((SUBLANES, n), F32),
        compiler_params=_cparams(("arbitrary",)),
        name="mod",
    )(cin, w, bias)


def _ctx_kernel(x_ref, g_ref, sh_ref, sc_ref, w_ref, k_ref, v_ref):
    h = _norm_mod(x_ref[...], g_ref[...], sh_ref[...], sc_ref[...])
    kv = _dot(h, w_ref[...])
    k_ref[...] = kv[:, :KV_WIDTH]
    v_ref[...] = kv[:, KV_WIDTH:]


def _ctx_call(ctx, g, sh, sc, w_kv):
    bsz, n, d = ctx.shape
    vec = pl.BlockSpec((1, d), lambda b: (0, 0))
    kv_spec = pl.BlockSpec((None, n, KV_WIDTH), lambda b: (b, 0, 0))
    return pl.pallas_call(
        _ctx_kernel,
        grid=(bsz,),
        in_specs=[pl.BlockSpec((None, n, d), lambda b: (b, 0, 0)), vec, vec, vec,
                  pl.BlockSpec((d, 2 * KV_WIDTH), lambda b: (0, 0))],
        out_specs=[kv_spec, kv_spec],
        out_shape=[jax.ShapeDtypeStruct((bsz, n, KV_WIDTH), F32)] * 2,
        compiler_params=_cparams(("arbitrary",)),
        name="ctx_kv",
    )(ctx, g, sh, sc, w_kv)


def _rope(t, cos, sin_signed):
    lane = lax.broadcasted_iota(jnp.int32, t.shape, 1)
    first = (lane % HEAD_DIM) < (HEAD_DIM // 2)
    swapped = jnp.where(first, pltpu.roll(t, LANES - HEAD_DIM // 2, 1), pltpu.roll(t, HEAD_DIM // 2, 1))
    return t * cos + swapped * sin_signed


def _in_kernel(x_ref, xp_ref, xn_ref, g_ref, sh_ref, sc_ref, w_ref, cw_ref, cos_ref, sin_ref,
               q_ref, k_ref, v_ref, conv_ref, *, conv_width, tm):
    i = pl.program_id(1)
    last = pl.num_programs(1) - 1
    x_ext = jnp.concatenate([x_ref[...], xp_ref[...], xn_ref[...]], axis=0)
    h_ext = _norm_mod(x_ext, g_ref[...], sh_ref[...], sc_ref[...])
    h = h_ext[0:tm]
    cos = cos_ref[...]
    sin = sin_ref[...]
    scale = HEAD_DIM ** -0.5 * LOG2E
    for j in range(ATT_WIDTH // MXU_COLS):
        qj = _dot(h, w_ref[:, j * MXU_COLS:(j + 1) * MXU_COLS])
        for jj in range(MXU_COLS // LANES):
            c0 = j * MXU_COLS + jj * LANES
            q_ref[:, c0:c0 + LANES] = _rope(qj[:, jj * LANES:(jj + 1) * LANES], cos, sin) * scale
    o = ATT_WIDTH
    kv = _dot(h, w_ref[:, o:o + 2 * KV_WIDTH])
    k_ref[...] = _rope(kv[:, :KV_WIDTH], cos, sin)
    v_ref[...] = kv[:, KV_WIDTH:]
    o += 2 * KV_WIDTH
    bg = _dot(h, w_ref[:, o:o + conv_width])
    o += conv_width
    cg = _dot(h_ext, w_ref[:, o:o + conv_width])
    o += conv_width
    hv = _dot(h_ext, w_ref[:, o:o + conv_width])
    u_ext = cg * hv
    u = u_ext[0:tm]
    row = lax.broadcasted_iota(jnp.int32, u.shape, 0)
    prev_row = jnp.where(i > 0, u_ext[tm + SUBLANES - 1:tm + SUBLANES], 0.0)
    next_row = jnp.where(i < last, u_ext[tm + SUBLANES:tm + SUBLANES + 1], 0.0)
    u_prev = jnp.where(row == 0, prev_row, pltpu.roll(u, 1, 0))
    u_next = jnp.where(row == tm - 1, next_row, pltpu.roll(u, tm - 1, 0))
    conv_ref[...] = bg * (cw_ref[0:1, :] * u_prev + cw_ref[1:2, :] * u + cw_ref[2:3, :] * u_next)


def _in_call(x, g, sh, sc, w_in, conv_w, cos_t, sin_t, tm=1024):
    bsz, seq, d = x.shape
    n_in = w_in.shape[1]
    cw = d - ATT_WIDTH
    nh = seq // SUBLANES
    vec = pl.BlockSpec((1, d), lambda b, i: (0, 0))
    bvec = pl.BlockSpec((None, 1, d), lambda b, i: (b, 0, 0))
    tab = pl.BlockSpec((tm, LANES), lambda b, i: (i, 0))
    halo_prev = pl.BlockSpec((None, SUBLANES, d),
                             lambda b, i: (b, jnp.maximum(i * (tm // SUBLANES) - 1, 0), 0))
    halo_next = pl.BlockSpec((None, SUBLANES, d),
                             lambda b, i: (b, jnp.minimum((i + 1) * (tm // SUBLANES), nh - 1), 0))

    def rows(width):
        return pl.BlockSpec((None, tm, width), lambda b, i: (b, i, 0))

    return pl.pallas_call(
        functools.partial(_in_kernel, conv_width=cw, tm=tm),
        grid=(bsz, seq // tm),
        in_specs=[rows(d), halo_prev, halo_next, vec, bvec, bvec,
                  pl.BlockSpec((d, n_in), lambda b, i: (0, 0)),
                  pl.BlockSpec((3, cw), lambda b, i: (0, 0)), tab, tab],
        out_specs=[rows(ATT_WIDTH), rows(KV_WIDTH), rows(KV_WIDTH), rows(cw)],
        out_shape=[jax.ShapeDtypeStruct((bsz, seq, ATT_WIDTH), F32),
                   jax.ShapeDtypeStruct((bsz, seq, KV_WIDTH), F32),
                   jax.ShapeDtypeStruct((bsz, seq, KV_WIDTH), F32),
                   jax.ShapeDtypeStruct((bsz, seq, cw), F32)],
        compiler_params=_cparams(("arbitrary", "arbitrary")),
        name="in_proj",
    )(x, x, x, g, sh, sc, w_in, conv_w, cos_t, sin_t)


ATTN_TILE = 32


def _attn_kernel(sink_ref, q_ref, k_ref, v_ref, kc_ref, vc_ref, o_ref,
                 s_scr, p_scr, bias_scr, m_scr, *, seq, tq):
    i = pl.program_id(1)
    nblk = tq // BLOCK
    span = 3 * BLOCK
    width = s_scr.shape[1]
    rows_all = GQA_GROUP * BLOCK
    tiles_per_head = BLOCK // ATTN_TILE
    kc = kc_ref[...]
    vc = vc_ref[...]
    for blk in range(nblk):
        n = i * nblk + blk
        ws = pl.multiple_of(jnp.clip((n - 1) * BLOCK, 0, seq - span), BLOCK)
        kw = k_ref[pl.ds(ws, span), :]
        vw = v_ref[pl.ds(ws, span), :]
        qb = q_ref[blk * BLOCK:(blk + 1) * BLOCK, :]
        qpos = n * BLOCK + lax.broadcasted_iota(jnp.int32, (BLOCK, span), 0)
        kpos = ws + lax.broadcasted_iota(jnp.int32, (BLOCK, span), 1)
        bias_scr[...] = jnp.where(jnp.abs(kpos - qpos) <= WINDOW, 0.0, NEG)
        outs = []
        for kh in range(N_KV_HEADS):
            heads = [kh * GQA_GROUP + g for g in range(GQA_GROUP)]
            qs = jnp.concatenate([qb[:, h * HEAD_DIM:(h + 1) * HEAD_DIM] for h in heads], axis=0)
            lo, hi = kh * HEAD_DIM, (kh + 1) * HEAD_DIM
            s_scr[:, 0:span] = _dot_nt(qs, kw[:, lo:hi])
            s_scr[:, span:width] = _dot_nt(qs, kc[:, lo:hi])

            for t in range(rows_all // ATTN_TILE):
                r0 = t * ATTN_TILE
                b0 = (t % tiles_per_head) * ATTN_TILE
                sk = sink_ref[kh * GQA_GROUP + t // tiles_per_head] * LOG2E
                s_loc = s_scr[pl.ds(r0, ATTN_TILE), 0:span] + bias_scr[pl.ds(b0, ATTN_TILE), :]
                s_ctx = s_scr[pl.ds(r0, ATTN_TILE), span:width]
                m = jnp.maximum(jnp.maximum(jnp.max(s_loc, axis=1, keepdims=True),
                                            jnp.max(s_ctx, axis=1, keepdims=True)), sk)
                m_scr[pl.ds(r0, ATTN_TILE), :] = m
                p_scr[pl.ds(r0, ATTN_TILE), 0:span] = jnp.exp2(s_loc - m)
                p_scr[pl.ds(r0, ATTN_TILE), span:width] = jnp.exp2(s_ctx - m)
            own = (lax.broadcasted_iota(jnp.int32, (1, KV_WIDTH), 1) // HEAD_DIM) == kh
            o_ext = (_dot(p_scr[:, 0:span], jnp.where(own, vw, 1.0))
                     + _dot(p_scr[:, span:width], jnp.where(own, vc, 1.0)))
            other = (1 - kh) * HEAD_DIM
            sk_col = jnp.concatenate([jnp.full((BLOCK, 1), sink_ref[h] * LOG2E, F32) for h in heads], axis=0)
            den = o_ext[:, other:other + 1] + jnp.exp2(sk_col - m_scr[...])
            o = o_ext[:, lo:hi] / den
            outs.extend(o[g * BLOCK:(g + 1) * BLOCK] for g in range(GQA_GROUP))
        o_ref[blk * BLOCK:(blk + 1) * BLOCK, :] = jnp.concatenate(outs, axis=1)


def _attn_call(sink, q, k, v, kc, vc, tq=1024):
    bsz, seq, _ = q.shape
    n_ctx = kc.shape[1]
    span = 3 * BLOCK
    rows_all = GQA_GROUP * BLOCK
    full_kv = pl.BlockSpec((None, seq, KV_WIDTH), lambda b, i: (b, 0, 0))
    ctx_kv = pl.BlockSpec((None, n_ctx, KV_WIDTH), lambda b, i: (b, 0, 0))
    qspec = pl.BlockSpec((None, tq, ATT_WIDTH), lambda b, i: (b, i, 0))
    return pl.pallas_call(
        functools.partial(_attn_kernel, seq=seq, tq=tq),
        grid=(bsz, seq // tq),
        in_specs=[pl.BlockSpec(memory_space=pltpu.SMEM), qspec, full_kv, full_kv, ctx_kv, ctx_kv],
        out_specs=qspec,
        out_shape=jax.ShapeDtypeStruct((bsz, seq, ATT_WIDTH), F32),
        scratch_shapes=[pltpu.VMEM((rows_all, span + n_ctx), F32),
                        pltpu.VMEM((rows_all, span + n_ctx), F32),
                        pltpu.VMEM((BLOCK, span), F32),
                        pltpu.VMEM((rows_all, 1), F32)],
        compiler_params=_cparams(("arbitrary", "arbitrary")),
        name="attn",
    )(sink, q, k, v, kc, vc)


OUT_RING = 3


def _out_kernel(att_hbm, conv_hbm, wo_ref, x_hbm, gt_ref,
                g_ref, sh_ref, sc_ref, wr_ref, rows_ref, aff_ref,
                att_ring, conv_ring, x_ring, sem, *, tm):
    n_i = pl.num_programs(1)
    total = pl.num_programs(0) * n_i
    g = pl.program_id(0) * n_i + pl.program_id(1)
    streams = ((att_hbm, att_ring), (conv_hbm, conv_ring), (x_hbm, x_ring))

    def copies(gg):
        rows = pl.ds(pl.multiple_of((gg % n_i) * tm, tm), tm)
        return [pltpu.make_async_copy(src.at[gg // n_i, rows], ring.at[gg % OUT_RING], sem.at[gg % OUT_RING])
                for src, ring in streams]

    @pl.when(g == 0)
    def _():
        for gg in range(OUT_RING - 1):
            for cp in copies(jnp.int32(gg)):
                cp.start()

    @pl.when(g + OUT_RING - 1 < total)
    def _():
        for cp in copies(g + OUT_RING - 1):
            cp.start()

    for cp in copies(g):
        cp.wait()
    slot = g % OUT_RING
    y = _dot(att_ring[slot], wo_ref[0:ATT_WIDTH, :]) + _dot(conv_ring[slot], wo_ref[ATT_WIDTH:, :])
    x1 = x_ring[slot] + gt_ref[...] * y
    d = x1.shape[-1]
    h2 = _norm_mod(x1, g_ref[...], sh_ref[...], sc_ref[...])
    rows_ref[:, 0:d] = x1
    rows_ref[:, d:2 * d] = h2
    def split3(t):
        hi = t.astype(BF16)
        r1 = t - hi.astype(F32)
        mid = r1.astype(BF16)
        return hi, mid, (r1 - mid.astype(F32)).astype(BF16)

    def nt_bf16(a, b):
        return lax.dot_general(a, b, (((1,), (1,)), ((), ())), preferred_element_type=F32)

    h_hi, h_mid, h_lo = split3(h2)
    w_hi, w_mid, w_lo = split3(wr_ref[...])
    n_exp = w_hi.shape[0]
    by_hi = nt_bf16(jnp.concatenate([w_hi, w_mid, w_lo], axis=0), h_hi)
    by_mid = nt_bf16(jnp.concatenate([w_hi, w_mid], axis=0), h_mid)
    logits = (((by_hi[2 * n_exp:] + by_mid[n_exp:]) + nt_bf16(w_hi, h_lo))
              + (by_hi[n_exp:2 * n_exp] + by_mid[:n_exp])) + by_hi[:n_exp]
    mx = jnp.max(logits, axis=0, keepdims=True)
    ex = jnp.exp(logits - mx)
    aff_ref[...] = ex / jnp.sum(ex, axis=0, keepdims=True)


def _out_call(att, conv, w_out, x, gt1, g_ffn, sh2, sc2, w_router_t, tm=512):
    bsz, seq, d = x.shape
    cw = conv.shape[-1]
    n_exp = w_router_t.shape[0]
    assert bsz * (seq // tm) >= OUT_RING

    def rows(width):
        return pl.BlockSpec((None, tm, width), lambda b, i: (b, i, 0))

    any_spec = pl.BlockSpec(memory_space=pl.ANY)
    vec = pl.BlockSpec((1, d), lambda b, i: (0, 0))
    bvec = pl.BlockSpec((None, 1, d), lambda b, i: (b, 0, 0))
    return pl.pallas_call(
        functools.partial(_out_kernel, tm=tm),
        grid=(bsz, seq // tm),
        in_specs=[any_spec, any_spec,
                  pl.BlockSpec((d, d), lambda b, i: (0, 0)),
                  any_spec, bvec, vec, bvec, bvec,
                  pl.BlockSpec((n_exp, d), lambda b, i: (0, 0))],
        out_specs=[rows(2 * d), pl.BlockSpec((None, n_exp, tm), lambda b, i: (b, 0, i))],
        out_shape=[jax.ShapeDtypeStruct((bsz, seq, 2 * d), F32),
                   jax.ShapeDtypeStruct((bsz, n_exp, seq), F32)],
        scratch_shapes=[pltpu.VMEM((OUT_RING, tm, ATT_WIDTH), F32), pltpu.VMEM((OUT_RING, tm, cw), F32),
                        pltpu.VMEM((OUT_RING, tm, d), F32), pltpu.SemaphoreType.DMA((OUT_RING,))],
        compiler_params=_cparams(("arbitrary", "arbitrary")),
        name="out_proj",
    )(att, conv, w_out, x, gt1, g_ffn, sh2, sc2, w_router_t)


def _prefix_rows(m, tri, ones, low):
    n_exp, n_chunk, _ = m.shape
    mb = m.astype(BF16).reshape(n_exp * n_chunk, LANES)
    within = jnp.dot(mb, tri, preferred_element_type=F32).reshape(n_exp, n_chunk, LANES)
    tot = jnp.dot(mb, ones, preferred_element_type=F32).astype(BF16).reshape(n_exp, n_chunk, LANES)
    carry = jnp.stack([jnp.dot(low, tot[e], preferred_element_type=F32) for e in range(n_exp)], axis=0)
    return carry + within, carry


def _sel_kernel(aff_ref, pos_ref, carry_ref, *, cap):
    a = aff_ref[...]
    n_exp, n_chunk, _ = a.shape
    bits = lax.bitcast_convert_type(a, jnp.int32)

    def count(pred):
        c = jnp.sum(pred.astype(F32), axis=1, keepdims=True)
        return jnp.sum(c, axis=2, keepdims=True)

    def body(it, thr):
        shift = 29 - 2 * it
        for k in (1, 2, 3):
            cand = thr | jnp.left_shift(jnp.int32(k), shift)
            ok = count(bits >= cand) >= cap
            best = jnp.where(ok, cand, thr) if k == 1 else jnp.where(ok, cand, best)
        return best

    thr = lax.fori_loop(0, 15, body, jnp.zeros((n_exp, 1, 1), jnp.int32))
    thr = jnp.where(count(bits >= (thr | 1)) >= cap, thr | 1, thr)
    gt = bits > thr
    eq = bits == thr
    need = cap - count(gt)

    r = lax.broadcasted_iota(jnp.int32, (LANES, LANES), 0)
    c = lax.broadcasted_iota(jnp.int32, (LANES, LANES), 1)
    tri = (r < c).astype(BF16)
    ones = jnp.ones((LANES, LANES), BF16)
    rr = lax.broadcasted_iota(jnp.int32, (n_chunk, n_chunk), 0)
    cc = lax.broadcasted_iota(jnp.int32, (n_chunk, n_chunk), 1)
    low = (cc < rr).astype(BF16)

    eq_rank, _ = _prefix_rows(eq, tri, ones, low)
    sel = gt | (eq & (eq_rank < need))
    pos, carry = _prefix_rows(sel, tri, ones, low)
    pos_ref[...] = jnp.where(sel, pos, -1.0)
    carry_ref[...] = carry


def _sel_call(aff4, cap):
    bsz, n_exp, n_chunk, _ = aff4.shape
    spec = pl.BlockSpec((None, n_exp, n_chunk, LANES), lambda b: (b, 0, 0, 0))
    return pl.pallas_call(
        functools.partial(_sel_kernel, cap=cap),
        grid=(bsz,),
        in_specs=[spec],
        out_specs=[spec, spec],
        out_shape=[jax.ShapeDtypeStruct(aff4.shape, F32)] * 2,
        compiler_params=_cparams(("arbitrary",)),
        name="select",
    )(aff4)


COMPACT_ROWS = 16
SLOT_BLOCK = LANES


def _compact_kernel(cum_ref, pos_ref, aff_ref, out_ref, *, n_exp, n_tchunk):
    b = pl.program_id(0)
    e = pl.program_id(1)
    base = (b * n_exp + e) * (n_tchunk + 1)
    out_ref[...] = jnp.zeros(out_ref.shape, F32)
    nt = (((1,), (1,)), ((), ()))
    rowi = lax.broadcasted_iota(jnp.int32, (COMPACT_ROWS, LANES), 0)
    lane = lax.broadcasted_iota(jnp.int32, (COMPACT_ROWS, LANES), 1).astype(F32)
    slot_b = lax.broadcasted_iota(jnp.int32, (SLOT_BLOCK, LANES), 0).astype(F32).astype(BF16)
    one_b = jnp.ones((), BF16)
    zero_b = jnp.zeros((), BF16)

    n_sblk = out_ref.shape[0] - 1

    def chunk_vals(c):
        a = aff_ref[c]
        g_hi = a.astype(BF16).astype(F32)
        r1 = a - g_hi
        g_mid = r1.astype(BF16).astype(F32)
        g_lo = r1 - g_mid
        pieces = []
        for r in range(SUBLANES):
            def bc(t):
                return jnp.broadcast_to(t[r:r + 1, :], (COMPACT_ROWS, LANES))

            piece = jnp.where(rowi == 0, float(c * SUBLANES + r),
                              jnp.where(rowi == 1, lane,
                                        jnp.where(rowi == 2, bc(g_hi),
                                                  jnp.where(rowi == 3, bc(g_mid),
                                                            jnp.where(rowi == 4, bc(g_lo), 0.0)))))
            pieces.append(piece)
        return jnp.concatenate(pieces, axis=1).astype(BF16)

    def visit(j, p, vals):
        rel = p - jnp.asarray(j * SLOT_BLOCK).astype(F32)
        rel = jnp.where((rel >= 0.0) & (rel < SLOT_BLOCK), rel, -1.0)
        pieces_oh = []
        for r in range(SUBLANES):
            row16 = jnp.broadcast_to(rel[r:r + 1, :], (2 * SUBLANES, LANES)).astype(BF16)
            blk = jnp.concatenate([row16] * (SLOT_BLOCK // (2 * SUBLANES)), axis=0)
            pieces_oh.append(jnp.where(blk == slot_b, one_b, zero_b))
        oh = jnp.concatenate(pieces_oh, axis=1)
        out_ref[j] += lax.dot_general(vals, oh, nt, preferred_element_type=F32)

    def first_block(c):
        return jnp.minimum(cum_ref[base + c] // SLOT_BLOCK, n_sblk - 1)

    for c in range(n_tchunk):
        p = pos_ref[c]
        vals = chunk_vals(c)
        j0 = first_block(c)
        visit(j0, p, vals)
        visit(j0 + 1, p, vals)

    for c in range(n_tchunk):
        lo = cum_ref[base + c]
        hi = cum_ref[base + c + 1]
        j_end = jnp.where(hi > lo, (hi - 1) // SLOT_BLOCK + 1, 0)

        def more(j, carry, c=c):
            visit(j, pos_ref[c], chunk_vals(c))
            return carry

        lax.fori_loop(first_block(c) + 2, j_end, more, 0)


def _compact_call(cum, pos5, aff5, cap):
    bsz, n_exp, n_tchunk, _, _ = pos5.shape
    n_sblk = cap // SLOT_BLOCK
    spec = pl.BlockSpec((None, None, n_tchunk, SUBLANES, LANES), lambda b, e, cum_ref: (b, e, 0, 0, 0))
    return pl.pallas_call(
        functools.partial(_compact_kernel, n_exp=n_exp, n_tchunk=n_tchunk),
        grid_spec=pltpu.PrefetchScalarGridSpec(
            num_scalar_prefetch=1,
            grid=(bsz, n_exp),
            in_specs=[spec, spec],
            out_specs=pl.BlockSpec((None, None, n_sblk + 1, COMPACT_ROWS, LANES),
                                   lambda b, e, cum_ref: (b, e, 0, 0, 0)),
        ),
        out_shape=jax.ShapeDtypeStruct((bsz, n_exp, n_sblk + 1, COMPACT_ROWS, LANES), F32),
        compiler_params=_cparams(("arbitrary", "arbitrary")),
        name="compact",
    )(cum, pos5, aff5)[:, :, :n_sblk]


FFN_ROWS = 256


FFN_COLS = 256


def _ffn_kernel(idx_ref, rows_in, g_ref, gt_ref, wg_ref, wu_ref, wd_ref, rows_hbm,
                buf0, buf1, wg32, wu32, wd32, wgb, wub, wdb, gsem, ssem, wsem, *, cap, n_exp, d):
    del rows_in
    e = pl.program_id(0)
    bufs = (buf0, buf1)
    n_chunk = cap // FFN_ROWS
    f = wgb.shape[1]
    acc_cols = pl.ds(0, d)

    def hbm_row(ee, b, s):
        return idx_ref[(b * n_exp + ee) * cap + s]

    gather_thread, scatter_thread = 0, 1

    def gather_start(ee, b, s):
        pltpu.make_async_copy(rows_hbm.at[pl.ds(hbm_row(ee, b, s), 1)], bufs[b].at[pl.ds(s, 1)],
                              gsem.at[b]).start(priority=gather_thread)

    def gather_wait(b):
        pltpu.make_async_copy(rows_hbm.at[pl.ds(0, cap)], bufs[b], gsem.at[b]).wait()

    def scatter_start(ee, b, s):
        pltpu.make_async_copy(bufs[b].at[pl.ds(s, 1), acc_cols],
                              rows_hbm.at[pl.ds(hbm_row(ee, b, s), 1), acc_cols],
                              ssem.at[b]).start(priority=scatter_thread)

    def scatter_wait(b):
        pltpu.make_async_copy(bufs[b].at[:, acc_cols], rows_hbm.at[pl.ds(0, cap), acc_cols],
                              ssem.at[b]).wait()

    def compute_chunk(b, ci, todo):
        buf = bufs[b]
        rows = slice(ci * FFN_ROWS, (ci + 1) * FFN_ROWS)
        n_piece = 3 * (f // FFN_COLS)
        per_piece = -(-len(todo) // n_piece)

        def issue_some():
            for _ in range(min(per_piece, len(todo))):
                todo.pop(0)()

        xc = buf[rows, d:2 * d].astype(BF16)
        g_col = jnp.concatenate(
            [jnp.broadcast_to(g_ref[b, j:j + 1, :], (LANES, LANES)).T
             for j in range(ci * FFN_ROWS // LANES, (ci + 1) * FFN_ROWS // LANES)], axis=0)
        g_col = jnp.tile(g_col, (1, FFN_COLS // LANES))
        gt = gt_ref[b]
        hm = []
        for nj in range(f // FFN_COLS):
            cols = slice(nj * FFN_COLS, (nj + 1) * FFN_COLS)
            a = jnp.dot(xc, wgb[:, cols], preferred_element_type=F32)
            issue_some()
            u = jnp.dot(xc, wub[:, cols], preferred_element_type=F32)
            issue_some()
            hm.append((a * jax.nn.sigmoid(a) * u).astype(BF16))
        hm = jnp.concatenate(hm, axis=1)
        for nj in range(d // FFN_COLS):
            cols = slice(nj * FFN_COLS, (nj + 1) * FFN_COLS)
            y = jnp.dot(hm, wdb[:, cols], preferred_element_type=F32) * g_col
            buf[rows, cols] = buf[rows, cols] + gt[:, cols] * y
            issue_some()
        assert not todo

    def chunk_rows(ci):
        return range(ci * FFN_ROWS, (ci + 1) * FFN_ROWS)

    def split(n, parts):
        bounds = [n * k // parts for k in range(parts + 1)]
        return [range(bounds[k], bounds[k + 1]) for k in range(parts)]

    slot = e % 2
    weights = ((wg_ref, wg32), (wu_ref, wu32), (wd_ref, wd32))

    def weights_copy(ee, sl):
        return [pltpu.make_async_copy(src.at[ee], dst.at[sl], wsem.at[sl]) for src, dst in weights]

    @pl.when(e == 0)
    def _():
        for cp in weights_copy(0, 0):
            cp.start(priority=1)

        def first(s, carry):
            gather_start(0, 0, s)
            return carry
        lax.fori_loop(0, cap, first, 0)

    for cp in weights_copy(e, slot):
        cp.wait()

    @pl.when(e + 1 < n_exp)
    def _():
        for cp in weights_copy(e + 1, 1 - slot):
            cp.start(priority=1)

    wgb[...] = wg32[slot].astype(BF16)
    wub[...] = wu32[slot].astype(BF16)
    wdb[...] = wd32[slot].astype(BF16)

    @pl.when(e > 0)
    def _():
        scatter_wait(1)

    gather_wait(0)
    for ci in range(n_chunk):
        todo = [functools.partial(gather_start, e, 1, s) for s in chunk_rows(ci)]
        if ci > 0:
            todo += [functools.partial(scatter_start, e, 0, s) for s in chunk_rows(ci - 1)]
        compute_chunk(0, ci, todo)

    ne = jnp.minimum(e + 1, n_exp - 1)
    gather_wait(1)
    compute_chunk(1, 0, [functools.partial(scatter_start, e, 0, s) for s in chunk_rows(n_chunk - 1)])
    scatter_wait(0)
    for ci, part in zip(range(1, n_chunk), split(cap, n_chunk - 1)):
        todo = [functools.partial(gather_start, ne, 0, s) for s in part]
        todo += [functools.partial(scatter_start, e, 1, s) for s in chunk_rows(ci - 1)]
        compute_chunk(1, ci, todo)
    for s in chunk_rows(n_chunk - 1):
        scatter_start(e, 1, s)

    @pl.when(e == n_exp - 1)
    def _():
        scatter_wait(1)
        gather_wait(0)


def _ffn_call(idx, rows, gate, gt2, w_gate, w_up, w_down, cap):
    bsz, seq, d2 = rows.shape
    d = d2 // 2
    n_exp, _, f = w_gate.shape
    assert bsz == 2, "ffn double-buffers exactly two samples per expert"
    kern = functools.partial(_ffn_kernel, cap=cap, n_exp=n_exp, d=d)
    any_spec = pl.BlockSpec(memory_space=pl.ANY)
    out = pl.pallas_call(
        kern,
        grid_spec=pltpu.PrefetchScalarGridSpec(
            num_scalar_prefetch=1,
            grid=(n_exp,),
            in_specs=[any_spec,
                      pl.BlockSpec((bsz, None, cap // LANES, LANES), lambda e, idx_ref: (0, e, 0, 0)),
                      pl.BlockSpec((bsz, 1, d), lambda e, idx_ref: (0, 0, 0)),
                      any_spec, any_spec, any_spec],
            out_specs=any_spec,
            scratch_shapes=[pltpu.VMEM((cap, d2), F32), pltpu.VMEM((cap, d2), F32),
                            pltpu.VMEM((2, d, f), F32), pltpu.VMEM((2, d, f), F32), pltpu.VMEM((2, f, d), F32),
                            pltpu.VMEM((d, f), BF16), pltpu.VMEM((d, f), BF16), pltpu.VMEM((f, d), BF16),
                            pltpu.SemaphoreType.DMA((bsz,)), pltpu.SemaphoreType.DMA((bsz,)),
                            pltpu.SemaphoreType.DMA((2,))],
        ),
        out_shape=jax.ShapeDtypeStruct((bsz * seq, d2), F32),
        input_output_aliases={1: 0},
        compiler_params=_cparams(("arbitrary",)),
        name="ffn",
    )(idx, rows.reshape(bsz * seq, d2), gate, gt2, w_gate, w_up, w_down)
    return out.reshape(bsz, seq, d2)


def _final_kernel(x_ref, g_ref, o_ref):
    xf = x_ref[...]
    o_ref[...] = xf * lax.rsqrt(jnp.mean(xf * xf, axis=-1, keepdims=True) + EPS) * g_ref[...]


def _final_call(rows, g, tm=2048):
    bsz, seq, d2 = rows.shape
    d = d2 // 2
    return pl.pallas_call(
        _final_kernel,
        grid=(bsz, seq // tm),
        in_specs=[pl.BlockSpec((None, tm, d), lambda b, i: (b, i, 0)),
                  pl.BlockSpec((1, d), lambda b, i: (0, 0))],
        out_specs=pl.BlockSpec((None, tm, d), lambda b, i: (b, i, 0)),
        out_shape=jax.ShapeDtypeStruct((bsz, seq, d), F32),
        compiler_params=_cparams(("arbitrary", "arbitrary")),
        name="final",
    )(rows, g)


def _rope_tables(seq):
    t = np.arange(seq)
    n_freq = HEAD_DIM // 4
    inv = ROPE_THETA ** (-np.arange(n_freq, dtype=np.float64) / n_freq)
    ang = np.concatenate([(t // GRID_W)[:, None] * inv, (t % GRID_W)[:, None] * inv], axis=1)
    cos_h = np.concatenate([np.cos(ang), np.cos(ang)], axis=1)
    sin_h = np.concatenate([-np.sin(ang), np.sin(ang)], axis=1)
    reps = LANES // HEAD_DIM
    return (jnp.asarray(np.tile(cos_h, (1, reps)), F32), jnp.asarray(np.tile(sin_h, (1, reps)), F32))


def kernel(x, c, ctx, c_ctx, w_ada, b_ada, g_mix, w_in, conv_w, sink, w_out, g_ffn,
           w_router, w_gate, w_up, w_down, g_final):
    bsz, seq, d = x.shape
    n_exp = w_router.shape[-1]
    cap = EC_CAPACITY_FACTOR * seq // n_exp
    assert w_ada.shape[0] == 1, "single trunk layer"
    assert bsz + 1 <= SUBLANES and seq % (SUBLANES * LANES) == 0 and cap % SLOT_BLOCK == 0

    cin = jnp.zeros((SUBLANES, d), F32).at[:bsz].set(c).at[bsz].set(c_ctx)
    mod = _mod_call(cin, w_ada[0], b_ada[0][None, :])
    sh1, sc1, gt1, sh2, sc2, gt2 = [m[:bsz, None, :] for m in jnp.split(mod, 6, axis=-1)]
    csh1 = mod[bsz:bsz + 1, 0:d]
    csc1 = mod[bsz:bsz + 1, d:2 * d]

    w_in0 = w_in[0]
    g_mix_r = g_mix[0][None, :]
    kv0 = ATT_WIDTH
    k_c, v_c = _ctx_call(ctx, g_mix_r, csh1, csc1, w_in0[:, kv0:kv0 + 2 * KV_WIDTH])

    cos_t, sin_t = _rope_tables(seq)
    q, k, v, conv = _in_call(x, g_mix_r, sh1, sc1, w_in0, conv_w[0], cos_t, sin_t)
    att = _attn_call(sink[0], q, k, v, k_c, v_c)
    rows, aff = _out_call(att, conv, w_out[0], x, gt1, g_ffn[0][None, :], sh2, sc2, w_router[0].T)

    n_chunk = seq // LANES
    aff4 = aff.reshape(bsz, n_exp, n_chunk, LANES)
    pos, carry = _sel_call(aff4, cap)
    n_tchunk = n_chunk // SUBLANES
    cum = jnp.concatenate([carry[:, :, ::SUBLANES, 0], jnp.full((bsz, n_exp, 1), cap, F32)], axis=-1)
    cum = cum.astype(jnp.int32).reshape(-1)
    comp = _compact_call(cum, pos.reshape(bsz, n_exp, n_tchunk, SUBLANES, LANES),
                         aff4.reshape(bsz, n_exp, n_tchunk, SUBLANES, LANES), cap)
    idx = (comp[:, :, :, 0, :] * LANES + comp[:, :, :, 1, :]).astype(jnp.int32)
    idx = (idx + (jnp.arange(bsz, dtype=jnp.int32) * seq)[:, None, None, None]).reshape(-1)
    gate = (comp[:, :, :, 2, :] + comp[:, :, :, 3, :]) + comp[:, :, :, 4, :]

    rows = _ffn_call(idx, rows, gate, gt2, w_gate[0], w_up[0], w_down[0], cap)
    return _final_call(rows, g_final[None, :])
```

```python
import functools

import numpy as np
import jax
import jax.numpy as jnp
from jax import lax
from jax.experimental import pallas as pl
from jax.experimental.pallas import tpu as pltpu

GRID_W = 64
N_HEADS = 8
N_KV_HEADS = 2
HEAD_DIM = 64
GQA_GROUP = N_HEADS // N_KV_HEADS
ATT_WIDTH = N_HEADS * HEAD_DIM
KV_WIDTH = N_KV_HEADS * HEAD_DIM
WINDOW = 128
BLOCK = 128
ROPE_THETA = 10000.0
N_EXPERTS = 16
EC_CAPACITY_FACTOR = 2
EPS = 1e-6

LANES = 128
SUBLANES = 8
MXU_COLS = 256
VMEM_LIMIT = 56 * 1024 * 1024

F32 = jnp.float32
BF16 = jnp.bfloat16
NEG = -1e30
LOG2E = 1.4426950408889634


def _cparams(sem):
    return pltpu.CompilerParams(dimension_semantics=sem, vmem_limit_bytes=VMEM_LIMIT)


def _dot(a, b):
    return jnp.dot(a, b, precision=lax.Precision.DEFAULT, preferred_element_type=F32)


def _dot_nt(a, b):
    return lax.dot_general(a, b, (((1,), (1,)), ((), ())), precision=lax.Precision.DEFAULT,
                           preferred_element_type=F32)


def _norm_mod(xf, g, shift, scale):
    r = xf * lax.rsqrt(jnp.mean(xf * xf, axis=-1, keepdims=True) + EPS)
    return (r * g) * (1.0 + scale) + shift


def _mod_kernel(c_ref, w_ref, b_ref, o_ref):
    c = c_ref[...]
    s = c * jax.nn.sigmoid(c)
    o_ref[...] = _dot(s, w_ref[...]) + b_ref[...]


def _mod_call(cin, w, bias, tn=1024):
    d, n = w.shape
    return pl.pallas_call(
        _mod_kernel,
        grid=(n // tn,),
        in_specs=[pl.BlockSpec((SUBLANES, d), lambda j: (0, 0)),
                  pl.BlockSpec((d, tn), lambda j: (0, j)),
                  pl.BlockSpec((1, tn), lambda j: (0, j))],
        out_specs=pl.BlockSpec((SUBLANES, tn), lambda j: (0, j)),
        out_shape=jax.ShapeDtypeStruct((SUBLANES, n), F32),
        compiler_params=_cparams(("arbitrary",)),
        name="mod",
    )(cin, w, bias)


def _ctx_kernel(x_ref, g_ref, sh_ref, sc_ref, w_ref, k_ref, v_ref):
    h = _norm_mod(x_ref[...], g_ref[...], sh_ref[...], sc_ref[...])
    kv = _dot(h, w_ref[...])
    k_ref[...] = kv[:, :KV_WIDTH]
    v_ref[...] = kv[:, KV_WIDTH:]


def _ctx_call(ctx, g, sh, sc, w_kv):
    bsz, n, d = ctx.shape
    vec = pl.BlockSpec((1, d), lambda b: (0, 0))
    kv_spec = pl.BlockSpec((None, n, KV_WIDTH), lambda b: (b, 0, 0))
    return pl.pallas_call(
        _ctx_kernel,
        grid=(bsz,),
        in_specs=[pl.BlockSpec((None, n, d), lambda b: (b, 0, 0)), vec, vec, vec,
                  pl.BlockSpec((d, 2 * KV_WIDTH), lambda b: (0, 0))],
        out_specs=[kv_spec, kv_spec],
        out_shape=[jax.ShapeDtypeStruct((bsz, n, KV_WIDTH), F32)] * 2,
        compiler_params=_cparams(("arbitrary",)),
        name="ctx_kv",
    )(ctx, g, sh, sc, w_kv)


def _rope(t, cos, sin_signed):
    lane = lax.broadcasted_iota(jnp.int32, t.shape, 1)
    first = (lane % HEAD_DIM) < (HEAD_DIM // 2)
    swapped = jnp.where(first, pltpu.roll(t, LANES - HEAD_DIM // 2, 1), pltpu.roll(t, HEAD_DIM // 2, 1))
    return t * cos + swapped * sin_signed


def _in_kernel(x_ref, xp_ref, xn_ref, g_ref, sh_ref, sc_ref, w_ref, cw_ref, cos_ref, sin_ref,
               q_ref, k_ref, v_ref, conv_ref, *, conv_width, tm):
    i = pl.program_id(1)
    last = pl.num_programs(1) - 1
    x_ext = jnp.concatenate([x_ref[...], xp_ref[...], xn_ref[...]], axis=0)
    h_ext = _norm_mod(x_ext, g_ref[...], sh_ref[...], sc_ref[...])
    h = h_ext[0:tm]
    cos = cos_ref[...]
    sin = sin_ref[...]
    scale = HEAD_DIM ** -0.5 * LOG2E
    for j in range(ATT_WIDTH // MXU_COLS):
        qj = _dot(h, w_ref[:, j * MXU_COLS:(j + 1) * MXU_COLS])
        for jj in range(MXU_COLS // LANES):
            c0 = j * MXU_COLS + jj * LANES
            q_ref[:, c0:c0 + LANES] = _rope(qj[:, jj * LANES:(jj + 1) * LANES], cos, sin) * scale
    o = ATT_WIDTH
    kv = _dot(h, w_ref[:, o:o + 2 * KV_WIDTH])
    k_ref[...] = _rope(kv[:, :KV_WIDTH], cos, sin)
    v_ref[...] = kv[:, KV_WIDTH:]
    o += 2 * KV_WIDTH
    bg = _dot(h, w_ref[:, o:o + conv_width])
    o += conv_width
    cg = _dot(h_ext, w_ref[:, o:o + conv_width])
    o += conv_width
    hv = _dot(h_ext, w_ref[:, o:o + conv_width])
    u_ext = cg * hv
    u = u_ext[0:tm]
    row = lax.broadcasted_iota(jnp.int32, u.shape, 0)
    prev_row = jnp.where(i > 0, u_ext[tm + SUBLANES - 1:tm + SUBLANES], 0.0)
    next_row = jnp.where(i < last, u_ext[tm + SUBLANES:tm + SUBLANES + 1], 0.0)
    u_prev = jnp.where(row == 0, prev_row, pltpu.roll(u, 1, 0))
    u_next = jnp.where(row == tm - 1, next_row, pltpu.roll(u, tm - 1, 0))
    conv_ref[...] = bg * (cw_ref[0:1, :] * u_prev + cw_ref[1:2, :] * u + cw_ref[2:3, :] * u_next)


def _in_call(x, g, sh, sc, w_in, conv_w, cos_t, sin_t, tm=1024):
    bsz, seq, d = x.shape
    n_in = w_in.shape[1]
    cw = d - ATT_WIDTH
    nh = seq // SUBLANES
    vec = pl.BlockSpec((1, d), lambda b, i: (0, 0))
    bvec = pl.BlockSpec((None, 1, d), lambda b, i: (b, 0, 0))
    tab = pl.BlockSpec((tm, LANES), lambda b, i: (i, 0))
    halo_prev = pl.BlockSpec((None, SUBLANES, d),
                             lambda b, i: (b, jnp.maximum(i * (tm // SUBLANES) - 1, 0), 0))
    halo_next = pl.BlockSpec((None, SUBLANES, d),
                             lambda b, i: (b, jnp.minimum((i + 1) * (tm // SUBLANES), nh - 1), 0))

    def rows(width):
        return pl.BlockSpec((None, tm, width), lambda b, i: (b, i, 0))

    return pl.pallas_call(
        functools.partial(_in_kernel, conv_width=cw, tm=tm),
        grid=(bsz, seq // tm),
        in_specs=[rows(d), halo_prev, halo_next, vec, bvec, bvec,
                  pl.BlockSpec((d, n_in), lambda b, i: (0, 0)),
                  pl.BlockSpec((3, cw), lambda b, i: (0, 0)), tab, tab],
        out_specs=[rows(ATT_WIDTH), rows(KV_WIDTH), rows(KV_WIDTH), rows(cw)],
        out_shape=[jax.ShapeDtypeStruct((bsz, seq, ATT_WIDTH), F32),
                   jax.ShapeDtypeStruct((bsz, seq, KV_WIDTH), F32),
                   jax.ShapeDtypeStruct((bsz, seq, KV_WIDTH), F32),
                   jax.ShapeDtypeStruct((bsz, seq, cw), F32)],
        compiler_params=_cparams(("arbitrary", "arbitrary")),
        name="in_proj",
    )(x, x, x, g, sh, sc, w_in, conv_w, cos_t, sin_t)


ATTN_TILE = 32


def _attn_kernel(sink_ref, q_ref, k_ref, v_ref, kc_ref, vc_ref, o_ref,
                 s_scr, p_scr, bias_scr, m_scr, *, seq, tq):
    i = pl.program_id(1)
    nblk = tq // BLOCK
    span = 3 * BLOCK
    width = s_scr.shape[1]
    rows_all = GQA_GROUP * BLOCK
    tiles_per_head = BLOCK // ATTN_TILE
    kc = kc_ref[...]
    vc = vc_ref[...]
    for blk in range(nblk):
        n = i * nblk + blk
        ws = pl.multiple_of(jnp.clip((n - 1) * BLOCK, 0, seq - span), BLOCK)
        kw = k_ref[pl.ds(ws, span), :]
        vw = v_ref[pl.ds(ws, span), :]
        qb = q_ref[blk * BLOCK:(blk + 1) * BLOCK, :]
        qpos = n * BLOCK + lax.broadcasted_iota(jnp.int32, (BLOCK, span), 0)
        kpos = ws + lax.broadcasted_iota(jnp.int32, (BLOCK, span), 1)
        bias_scr[...] = jnp.where(jnp.abs(kpos - qpos) <= WINDOW, 0.0, NEG)
        outs = []
        for kh in range(N_KV_HEADS):
            heads = [kh * GQA_GROUP + g for g in range(GQA_GROUP)]
            qs = jnp.concatenate([qb[:, h * HEAD_DIM:(h + 1) * HEAD_DIM] for h in heads], axis=0)
            lo, hi = kh * HEAD_DIM, (kh + 1) * HEAD_DIM
            s_scr[:, 0:span] = _dot_nt(qs, kw[:, lo:hi])
            s_scr[:, span:width] = _dot_nt(qs, kc[:, lo:hi])

            for t in range(rows_all // ATTN_TILE):
                r0 = t * ATTN_TILE
                b0 = (t % tiles_per_head) * ATTN_TILE
                sk = sink_ref[kh * GQA_GROUP + t // tiles_per_head] * LOG2E
                s_loc = s_scr[pl.ds(r0, ATTN_TILE), 0:span] + bias_scr[pl.ds(b0, ATTN_TILE), :]
                s_ctx = s_scr[pl.ds(r0, ATTN_TILE), span:width]
                m = jnp.maximum(jnp.maximum(jnp.max(s_loc, axis=1, keepdims=True),
                                            jnp.max(s_ctx, axis=1, keepdims=True)), sk)
                m_scr[pl.ds(r0, ATTN_TILE), :] = m
                p_scr[pl.ds(r0, ATTN_TILE), 0:span] = jnp.exp2(s_loc - m)
                p_scr[pl.ds(r0, ATTN_TILE), span:width] = jnp.exp2(s_ctx - m)
            own = (lax.broadcasted_iota(jnp.int32, (1, KV_WIDTH), 1) // HEAD_DIM) == kh
            o_ext = (_dot(p_scr[:, 0:span], jnp.where(own, vw, 1.0))
                     + _dot(p_scr[:, span:width], jnp.where(own, vc, 1.0)))
            other = (1 - kh) * HEAD_DIM
            sk_col = jnp.concatenate([jnp.full((BLOCK, 1), sink_ref[h] * LOG2E, F32) for h in heads], axis=0)
            den = o_ext[:, other:other + 1] + jnp.exp2(sk_col - m_scr[...])
            o = o_ext[:, lo:hi] / den
            outs.extend(o[g * BLOCK:(g + 1) * BLOCK] for g in range(GQA_GROUP))
        o_ref[blk * BLOCK:(blk + 1) * BLOCK, :] = jnp.concatenate(outs, axis=1)


def _attn_call(sink, q, k, v, kc, vc, tq=1024):
    bsz, seq, _ = q.shape
    n_ctx = kc.shape[1]
    span = 3 * BLOCK
    rows_all = GQA_GROUP * BLOCK
    full_kv = pl.BlockSpec((None, seq, KV_WIDTH), lambda b, i: (b, 0, 0))
    ctx_kv = pl.BlockSpec((None, n_ctx, KV_WIDTH), lambda b, i: (b, 0, 0))
    qspec = pl.BlockSpec((None, tq, ATT_WIDTH), lambda b, i: (b, i, 0))
    return pl.pallas_call(
        functools.partial(_attn_kernel, seq=seq, tq=tq),
        grid=(bsz, seq // tq),
        in_specs=[pl.BlockSpec(memory_space=pltpu.SMEM), qspec, full_kv, full_kv, ctx_kv, ctx_kv],
        out_specs=qspec,
        out_shape=jax.ShapeDtypeStruct((bsz, seq, ATT_WIDTH), F32),
        scratch_shapes=[pltpu.VMEM((rows_all, span + n_ctx), F32),
                        pltpu.VMEM((rows_all, span + n_ctx), F32),
                        pltpu.VMEM((BLOCK, span), F32),
                        pltpu.VMEM((rows_all, 1), F32)],
        compiler_params=_cparams(("arbitrary", "arbitrary")),
        name="attn",
    )(sink, q, k, v, kc, vc)


def _out_kernel(att_ref, conv_ref, wo_ref, x_ref, gt_ref,
                g_ref, sh_ref, sc_ref, wr_ref, rows_ref, aff_ref):
    y = _dot(att_ref[...], wo_ref[0:ATT_WIDTH, :]) + _dot(conv_ref[...], wo_ref[ATT_WIDTH:, :])
    x1 = x_ref[...] + gt_ref[...] * y
    d = x1.shape[-1]
    h2 = _norm_mod(x1, g_ref[...], sh_ref[...], sc_ref[...])
    rows_ref[:, 0:d] = x1
    rows_ref[:, d:2 * d] = h2
    def split3(t):
        hi = t.astype(BF16)
        r1 = t - hi.astype(F32)
        mid = r1.astype(BF16)
        return hi, mid, (r1 - mid.astype(F32)).astype(BF16)

    def nt_bf16(a, b):
        return lax.dot_general(a, b, (((1,), (1,)), ((), ())), preferred_element_type=F32)

    h_hi, h_mid, h_lo = split3(h2)
    w_hi, w_mid, w_lo = split3(wr_ref[...])
    n_exp = w_hi.shape[0]
    by_hi = nt_bf16(jnp.concatenate([w_hi, w_mid, w_lo], axis=0), h_hi)
    by_mid = nt_bf16(jnp.concatenate([w_hi, w_mid], axis=0), h_mid)
    logits = (((by_hi[2 * n_exp:] + by_mid[n_exp:]) + nt_bf16(w_hi, h_lo))
              + (by_hi[n_exp:2 * n_exp] + by_mid[:n_exp])) + by_hi[:n_exp]
    mx = jnp.max(logits, axis=0, keepdims=True)
    ex = jnp.exp(logits - mx)
    aff_ref[...] = ex / jnp.sum(ex, axis=0, keepdims=True)


def _out_call(att, conv, w_out, x, gt1, g_ffn, sh2, sc2, w_router_t, tm=1024):
    bsz, seq, d = x.shape
    cw = conv.shape[-1]
    n_exp = w_router_t.shape[0]

    def rows(width):
        return pl.BlockSpec((None, tm, width), lambda b, i: (b, i, 0))

    vec = pl.BlockSpec((1, d), lambda b, i: (0, 0))
    bvec = pl.BlockSpec((None, 1, d), lambda b, i: (b, 0, 0))
    return pl.pallas_call(
        _out_kernel,
        grid=(bsz, seq // tm),
        in_specs=[rows(ATT_WIDTH), rows(cw),
                  pl.BlockSpec((d, d), lambda b, i: (0, 0)),
                  rows(d), bvec, vec, bvec, bvec,
                  pl.BlockSpec((n_exp, d), lambda b, i: (0, 0))],
        out_specs=[rows(2 * d), pl.BlockSpec((None, n_exp, tm), lambda b, i: (b, 0, i))],
        out_shape=[jax.ShapeDtypeStruct((bsz, seq, 2 * d), F32),
                   jax.ShapeDtypeStruct((bsz, n_exp, seq), F32)],
        compiler_params=_cparams(("arbitrary", "arbitrary")),
        name="out_proj",
    )(att, conv, w_out, x, gt1, g_ffn, sh2, sc2, w_router_t)


def _prefix_rows(m, tri, ones, low):
    n_exp, n_chunk, _ = m.shape
    mb = m.astype(BF16).reshape(n_exp * n_chunk, LANES)
    within = jnp.dot(mb, tri, preferred_element_type=F32).reshape(n_exp, n_chunk, LANES)
    tot = jnp.dot(mb, ones, preferred_element_type=F32).astype(BF16).reshape(n_exp, n_chunk, LANES)
    carry = jnp.stack([jnp.dot(low, tot[e], preferred_element_type=F32) for e in range(n_exp)], axis=0)
    return carry + within, carry


def _sel_kernel(aff_ref, pos_ref, carry_ref, *, cap):
    a = aff_ref[...]
    n_exp, n_chunk, _ = a.shape
    bits = lax.bitcast_convert_type(a, jnp.int32)

    def count(pred):
        c = jnp.sum(pred.astype(F32), axis=1, keepdims=True)
        return jnp.sum(c, axis=2, keepdims=True)

    def body(it, thr):
        shift = 29 - 2 * it
        for k in (1, 2, 3):
            cand = thr | jnp.left_shift(jnp.int32(k), shift)
            ok = count(bits >= cand) >= cap
            best = jnp.where(ok, cand, thr) if k == 1 else jnp.where(ok, cand, best)
        return best

    thr = lax.fori_loop(0, 15, body, jnp.zeros((n_exp, 1, 1), jnp.int32))
    thr = jnp.where(count(bits >= (thr | 1)) >= cap, thr | 1, thr)
    gt = bits > thr
    eq = bits == thr
    need = cap - count(gt)

    r = lax.broadcasted_iota(jnp.int32, (LANES, LANES), 0)
    c = lax.broadcasted_iota(jnp.int32, (LANES, LANES), 1)
    tri = (r < c).astype(BF16)
    ones = jnp.ones((LANES, LANES), BF16)
    rr = lax.broadcasted_iota(jnp.int32, (n_chunk, n_chunk), 0)
    cc = lax.broadcasted_iota(jnp.int32, (n_chunk, n_chunk), 1)
    low = (cc < rr).astype(BF16)

    eq_rank, _ = _prefix_rows(eq, tri, ones, low)
    sel = gt | (eq & (eq_rank < need))
    pos, carry = _prefix_rows(sel, tri, ones, low)
    pos_ref[...] = jnp.where(sel, pos, -1.0)
    carry_ref[...] = carry


def _sel_call(aff4, cap):
    bsz, n_exp, n_chunk, _ = aff4.shape
    spec = pl.BlockSpec((None, n_exp, n_chunk, LANES), lambda b: (b, 0, 0, 0))
    return pl.pallas_call(
        functools.partial(_sel_kernel, cap=cap),
        grid=(bsz,),
        in_specs=[spec],
        out_specs=[spec, spec],
        out_shape=[jax.ShapeDtypeStruct(aff4.shape, F32)] * 2,
        compiler_params=_cparams(("arbitrary",)),
        name="select",
    )(aff4)


COMPACT_ROWS = 16
SLOT_BLOCK = LANES


def _compact_kernel(cum_ref, pos_ref, aff_ref, out_ref, *, n_exp, n_tchunk):
    b = pl.program_id(0)
    e = pl.program_id(1)
    base = (b * n_exp + e) * (n_tchunk + 1)
    out_ref[...] = jnp.zeros(out_ref.shape, F32)
    nt = (((1,), (1,)), ((), ()))
    rowi = lax.broadcasted_iota(jnp.int32, (COMPACT_ROWS, LANES), 0)
    lane = lax.broadcasted_iota(jnp.int32, (COMPACT_ROWS, LANES), 1).astype(F32)
    slot_b = lax.broadcasted_iota(jnp.int32, (SLOT_BLOCK, LANES), 0).astype(F32).astype(BF16)
    one_b = jnp.ones((), BF16)
    zero_b = jnp.zeros((), BF16)

    n_sblk = out_ref.shape[0] - 1

    def chunk_vals(c):
        a = aff_ref[c]
        g_hi = a.astype(BF16).astype(F32)
        r1 = a - g_hi
        g_mid = r1.astype(BF16).astype(F32)
        g_lo = r1 - g_mid
        pieces = []
        for r in range(SUBLANES):
            def bc(t):
                return jnp.broadcast_to(t[r:r + 1, :], (COMPACT_ROWS, LANES))

            piece = jnp.where(rowi == 0, float(c * SUBLANES + r),
                              jnp.where(rowi == 1, lane,
                                        jnp.where(rowi == 2, bc(g_hi),
                                                  jnp.where(rowi == 3, bc(g_mid),
                                                            jnp.where(rowi == 4, bc(g_lo), 0.0)))))
            pieces.append(piece)
        return jnp.concatenate(pieces, axis=1).astype(BF16)

    def visit(j, p, vals):
        rel = p - jnp.asarray(j * SLOT_BLOCK).astype(F32)
        rel = jnp.where((rel >= 0.0) & (rel < SLOT_BLOCK), rel, -1.0)
        pieces_oh = []
        for r in range(SUBLANES):
            row16 = jnp.broadcast_to(rel[r:r + 1, :], (2 * SUBLANES, LANES)).astype(BF16)
            blk = jnp.concatenate([row16] * (SLOT_BLOCK // (2 * SUBLANES)), axis=0)
            pieces_oh.append(jnp.where(blk == slot_b, one_b, zero_b))
        oh = jnp.concatenate(pieces_oh, axis=1)
        out_ref[j] += lax.dot_general(vals, oh, nt, preferred_element_type=F32)

    def first_block(c):
        return jnp.minimum(cum_ref[base + c] // SLOT_BLOCK, n_sblk - 1)

    for c in range(n_tchunk):
        p = pos_ref[c]
        vals = chunk_vals(c)
        j0 = first_block(c)
        visit(j0, p, vals)
        visit(j0 + 1, p, vals)

    for c in range(n_tchunk):
        lo = cum_ref[base + c]
        hi = cum_ref[base + c + 1]
        j_end = jnp.where(hi > lo, (hi - 1) // SLOT_BLOCK + 1, 0)

        def more(j, carry, c=c):
            visit(j, pos_ref[c], chunk_vals(c))
            return carry

        lax.fori_loop(first_block(c) + 2, j_end, more, 0)


def _compact_call(cum, pos5, aff5, cap):
    bsz, n_exp, n_tchunk, _, _ = pos5.shape
    n_sblk = cap // SLOT_BLOCK
    spec = pl.BlockSpec((None, None, n_tchunk, SUBLANES, LANES), lambda b, e, cum_ref: (b, e, 0, 0, 0))
    return pl.pallas_call(
        functools.partial(_compact_kernel, n_exp=n_exp, n_tchunk=n_tchunk),
        grid_spec=pltpu.PrefetchScalarGridSpec(
            num_scalar_prefetch=1,
            grid=(bsz, n_exp),
            in_specs=[spec, spec],
            out_specs=pl.BlockSpec((None, None, n_sblk + 1, COMPACT_ROWS, LANES),
                                   lambda b, e, cum_ref: (b, e, 0, 0, 0)),
        ),
        out_shape=jax.ShapeDtypeStruct((bsz, n_exp, n_sblk + 1, COMPACT_ROWS, LANES), F32),
        compiler_params=_cparams(("arbitrary", "arbitrary")),
        name="compact",
    )(cum, pos5, aff5)[:, :, :n_sblk]


FFN_ROWS = 256


FFN_COLS = 256


def _ffn_kernel(idx_ref, rows_in, g_ref, gt_ref, wg_ref, wu_ref, wd_ref, rows_hbm,
                buf0, buf1, wg32, wu32, wd32, wgb, wub, wdb, gsem, ssem, wsem, *, cap, n_exp, d):
    del rows_in
    e = pl.program_id(0)
    bufs = (buf0, buf1)
    n_chunk = cap // FFN_ROWS
    f = wgb.shape[1]
    acc_cols = pl.ds(0, d)

    def hbm_row(ee, b, s):
        return idx_ref[(b * n_exp + ee) * cap + s]

    gather_thread, scatter_thread = 0, 1

    def gather_start(ee, b, s):
        pltpu.make_async_copy(rows_hbm.at[pl.ds(hbm_row(ee, b, s), 1)], bufs[b].at[pl.ds(s, 1)],
                              gsem.at[b]).start(priority=gather_thread)

    def gather_wait(b):
        pltpu.make_async_copy(rows_hbm.at[pl.ds(0, cap)], bufs[b], gsem.at[b]).wait()

    def scatter_start(ee, b, s):
        pltpu.make_async_copy(bufs[b].at[pl.ds(s, 1), acc_cols],
                              rows_hbm.at[pl.ds(hbm_row(ee, b, s), 1), acc_cols],
                              ssem.at[b]).start(priority=scatter_thread)

    def scatter_wait(b):
        pltpu.make_async_copy(bufs[b].at[:, acc_cols], rows_hbm.at[pl.ds(0, cap), acc_cols],
                              ssem.at[b]).wait()

    def compute_chunk(b, ci, todo):
        buf = bufs[b]
        rows = slice(ci * FFN_ROWS, (ci + 1) * FFN_ROWS)
        n_piece = 3 * (f // FFN_COLS)
        per_piece = -(-len(todo) // n_piece)

        def issue_some():
            for _ in range(min(per_piece, len(todo))):
                todo.pop(0)()

        xc = buf[rows, d:2 * d].astype(BF16)
        g_col = jnp.concatenate(
            [jnp.broadcast_to(g_ref[b, j:j + 1, :], (LANES, LANES)).T
             for j in range(ci * FFN_ROWS // LANES, (ci + 1) * FFN_ROWS // LANES)], axis=0)
        g_col = jnp.tile(g_col, (1, FFN_COLS // LANES))
        gt = gt_ref[b]
        hm = []
        for nj in range(f // FFN_COLS):
            cols = slice(nj * FFN_COLS, (nj + 1) * FFN_COLS)
            a = jnp.dot(xc, wgb[:, cols], preferred_element_type=F32)
            issue_some()
            u = jnp.dot(xc, wub[:, cols], preferred_element_type=F32)
            issue_some()
            hm.append((a * jax.nn.sigmoid(a) * u).astype(BF16))
        hm = jnp.concatenate(hm, axis=1)
        for nj in range(d // FFN_COLS):
            cols = slice(nj * FFN_COLS, (nj + 1) * FFN_COLS)
            y = jnp.dot(hm, wdb[:, cols], preferred_element_type=F32) * g_col
            issue_some()
            buf[rows, cols] = buf[rows, cols] + gt[:, cols] * y
        assert not todo

    def chunk_rows(ci):
        return range(ci * FFN_ROWS, (ci + 1) * FFN_ROWS)

    def split(n, parts):
        bounds = [n * k // parts for k in range(parts + 1)]
        return [range(bounds[k], bounds[k + 1]) for k in range(parts)]

    slot = e % 2
    weights = ((wg_ref, wg32), (wu_ref, wu32), (wd_ref, wd32))

    def weights_copy(ee, sl):
        return [pltpu.make_async_copy(src.at[ee], dst.at[sl], wsem.at[sl]) for src, dst in weights]

    @pl.when(e == 0)
    def _():
        for cp in weights_copy(0, 0):
            cp.start(priority=1)

        def first(s, carry):
            gather_start(0, 0, s)
            return carry
        lax.fori_loop(0, cap, first, 0)

    for cp in weights_copy(e, slot):
        cp.wait()

    @pl.when(e + 1 < n_exp)
    def _():
        for cp in weights_copy(e + 1, 1 - slot):
            cp.start(priority=1)

    wgb[...] = wg32[slot].astype(BF16)
    wub[...] = wu32[slot].astype(BF16)
    wdb[...] = wd32[slot].astype(BF16)

    @pl.when(e > 0)
    def _():
        scatter_wait(1)

    gather_wait(0)
    for ci in range(n_chunk):
        todo = [functools.partial(gather_start, e, 1, s) for s in chunk_rows(ci)]
        if ci > 0:
            todo += [functools.partial(scatter_start, e, 0, s) for s in chunk_rows(ci - 1)]
        compute_chunk(0, ci, todo)

    ne = jnp.minimum(e + 1, n_exp - 1)
    gather_wait(1)
    compute_chunk(1, 0, [functools.partial(scatter_start, e, 0, s) for s in chunk_rows(n_chunk - 1)])
    scatter_wait(0)
    for ci, part in zip(range(1, n_chunk), split(cap, n_chunk - 1)):
        todo = [functools.partial(gather_start, ne, 0, s) for s in part]
        todo += [functools.partial(scatter_start, e, 1, s) for s in chunk_rows(ci - 1)]
        compute_chunk(1, ci, todo)
    for s in chunk_rows(n_chunk - 1):
        scatter_start(e, 1, s)

    @pl.when(e == n_exp - 1)
    def _():
        scatter_wait(1)
        gather_wait(0)


def _ffn_call(idx, rows, gate, gt2, w_gate, w_up, w_down, cap):
    bsz, seq, d2 = rows.shape
    d = d2 // 2
    n_exp, _, f = w_gate.shape
    assert bsz == 2, "ffn double-buffers exactly two samples per expert"
    kern = functools.partial(_ffn_kernel, cap=cap, n_exp=n_exp, d=d)
    any_spec = pl.BlockSpec(memory_space=pl.ANY)
    out = pl.pallas_call(
        kern,
        grid_spec=pltpu.PrefetchScalarGridSpec(
            num_scalar_prefetch=1,
            grid=(n_exp,),
            in_specs=[any_spec,
                      pl.BlockSpec((bsz, None, cap // LANES, LANES), lambda e, idx_ref: (0, e, 0, 0)),
                      pl.BlockSpec((bsz, 1, d), lambda e, idx_ref: (0, 0, 0)),
                      any_spec, any_spec, any_spec],
            out_specs=any_spec,
            scratch_shapes=[pltpu.VMEM((cap, d2), F32), pltpu.VMEM((cap, d2), F32),
                            pltpu.VMEM((2, d, f), F32), pltpu.VMEM((2, d, f), F32), pltpu.VMEM((2, f, d), F32),
                            pltpu.VMEM((d, f), BF16), pltpu.VMEM((d, f), BF16), pltpu.VMEM((f, d), BF16),
                            pltpu.SemaphoreType.DMA((bsz,)), pltpu.SemaphoreType.DMA((bsz,)),
                            pltpu.SemaphoreType.DMA((2,))],
        ),
        out_shape=jax.ShapeDtypeStruct((bsz * seq, d2), F32),
        input_output_aliases={1: 0},
        compiler_params=_cparams(("arbitrary",)),
        name="ffn",
    )(idx, rows.reshape(bsz * seq, d2), gate, gt2, w_gate, w_up, w_down)
    return out.reshape(bsz, seq, d2)


def _final_kernel(x_ref, g_ref, o_ref):
    xf = x_ref[...]
    o_ref[...] = xf * lax.rsqrt(jnp.mean(xf * xf, axis=-1, keepdims=True) + EPS) * g_ref[...]


def _final_call(rows, g, tm=2048):
    bsz, seq, d2 = rows.shape
    d = d2 // 2
    return pl.pallas_call(
        _final_kernel,
        grid=(bsz, seq // tm),
        in_specs=[pl.BlockSpec((None, tm, d), lambda b, i: (b, i, 0)),
                  pl.BlockSpec((1, d), lambda b, i: (0, 0))],
        out_specs=pl.BlockSpec((None, tm, d), lambda b, i: (b, i, 0)),
        out_shape=jax.ShapeDtypeStruct((bsz, seq, d), F32),
        compiler_params=_cparams(("arbitrary", "arbitrary")),
        name="final",
    )(rows, g)


def _rope_tables(seq):
    t = np.arange(seq)
    n_freq = HEAD_DIM // 4
    inv = ROPE_THETA ** (-np.arange(n_freq, dtype=np.float64) / n_freq)
    ang = np.concatenate([(t // GRID_W)[:, None] * inv, (t % GRID_W)[:, None] * inv], axis=1)
    cos_h = np.concatenate([np.cos(ang), np.cos(ang)], axis=1)
    sin_h = np.concatenate([-np.sin(ang), np.sin(ang)], axis=1)
    reps = LANES // HEAD_DIM
    return (jnp.asarray(np.tile(cos_h, (1, reps)), F32), jnp.asarray(np.tile(sin_h, (1, reps)), F32))


def kernel(x, c, ctx, c_ctx, w_ada, b_ada, g_mix, w_in, conv_w, sink, w_out, g_ffn,
           w_router, w_gate, w_up, w_down, g_final):
    bsz, seq, d = x.shape
    n_exp = w_router.shape[-1]
    cap = EC_CAPACITY_FACTOR * seq // n_exp
    assert w_ada.shape[0] == 1, "single trunk layer"
    assert bsz + 1 <= SUBLANES and seq % (SUBLANES * LANES) == 0 and cap % SLOT_BLOCK == 0

    cin = jnp.zeros((SUBLANES, d), F32).at[:bsz].set(c).at[bsz].set(c_ctx)
    mod = _mod_call(cin, w_ada[0], b_ada[0][None, :])
    sh1, sc1, gt1, sh2, sc2, gt2 = [m[:bsz, None, :] for m in jnp.split(mod, 6, axis=-1)]
    csh1 = mod[bsz:bsz + 1, 0:d]
    csc1 = mod[bsz:bsz + 1, d:2 * d]

    w_in0 = w_in[0]
    g_mix_r = g_mix[0][None, :]
    kv0 = ATT_WIDTH
    k_c, v_c = _ctx_call(ctx, g_mix_r, csh1, csc1, w_in0[:, kv0:kv0 + 2 * KV_WIDTH])

    cos_t, sin_t = _rope_tables(seq)
    q, k, v, conv = _in_call(x, g_mix_r, sh1, sc1, w_in0, conv_w[0], cos_t, sin_t)
    att = _attn_call(sink[0], q, k, v, k_c, v_c)
    rows, aff = _out_call(att, conv, w_out[0], x, gt1, g_ffn[0][None, :], sh2, sc2, w_router[0].T)

    n_chunk = seq // LANES
    aff4 = aff.reshape(bsz, n_exp, n_chunk, LANES)
    pos, carry = _sel_call(aff4, cap)
    n_tchunk = n_chunk // SUBLANES
    cum = jnp.concatenate([carry[:, :, ::SUBLANES, 0], jnp.full((bsz, n_exp, 1), cap, F32)], axis=-1)
    cum = cum.astype(jnp.int32).reshape(-1)
    comp = _compact_call(cum, pos.reshape(bsz, n_exp, n_tchunk, SUBLANES, LANES),
                         aff4.reshape(bsz, n_exp, n_tchunk, SUBLANES, LANES), cap)
    idx = (comp[:, :, :, 0, :] * LANES + comp[:, :, :, 1, :]).astype(jnp.int32)
    idx = (idx + (jnp.arange(bsz, dtype=jnp.int32) * seq)[:, None, None, None]).reshape(-1)
    gate = (comp[:, :, :, 2, :] + comp[:, :, :, 3, :]) + comp[:, :, :, 4, :]

    rows = _ffn_call(idx, rows, gate, gt2, w_gate[0], w_up[0], w_down[0], cap)
    return _final_call(rows, g_final[None, :])
```
